```python
import math
import jax, jax.numpy as jnp
from jax import lax
import numpy as np

D_MODEL = 1024
BATCH = 8
SEQ = 2048
DEPTH = 1
DEC_BATCH = 128
DEC_SEQ = 1
PAST_LEN = 16384
PAGE_SIZE = 128

D_CONV = D_MODEL
CONV_W = 3
N_RET_HEADS = 4
RET_DK = D_MODEL // N_RET_HEADS
RET_DV = 2 * RET_DK
RET_QK = N_RET_HEADS * RET_DK
RET_VDIM = N_RET_HEADS * RET_DV
RET_CHUNK = 128
ROPE_BASE = 10000.0
N_EXPERTS = 32
TOP_K = 4
D_FF = D_MODEL
SWIGLU_ALPHA = 1.702
SWIGLU_LIMIT = 7.0
EXPERT_BLOCK = 128
NORM_EPS = 1e-5
SPLIT_WIDTHS = (D_CONV, D_CONV, D_CONV, RET_QK, RET_QK, RET_VDIM, RET_VDIM, D_MODEL, D_MODEL)
N_IN_COLS = sum(SPLIT_WIDTHS)

kernel_name = "hybrid_conv_retention_moe_step"


def rmsnorm(x, g):
    x32 = x.astype(jnp.float32)
    y = x32 * lax.rsqrt(jnp.mean(x32 * x32, axis=-1, keepdims=True) + NORM_EPS)
    return (y * g.astype(jnp.float32)).astype(x.dtype)


def rotary(x, pos):
    half = x.shape[-1] // 2
    inv = ROPE_BASE ** (-jnp.linspace(0.0, 1.0, half, dtype=jnp.float32))
    ang = pos.astype(jnp.float32)[:, None] * inv[None, :]
    cos = jnp.cos(ang)[None, :, None, :]
    sin = jnp.sin(ang)[None, :, None, :]
    x32 = x.astype(jnp.float32)
    x1, x2 = x32[..., :half], x32[..., half:]
    return jnp.concatenate([x1 * cos - x2 * sin, x1 * sin + x2 * cos], axis=-1)


def retention_chunkwise(q, k, v, s0):
    B, L, H, DK = q.shape
    DV = v.shape[-1]
    C = math.gcd(L, RET_CHUNK)
    n = L // C
    log_g = jnp.log1p(-jnp.exp2(-5.0 - jnp.arange(H, dtype=jnp.float32)))
    pos = jnp.arange(C, dtype=jnp.float32)
    diff = pos[:, None] - pos[None, :]
    decay_mask = jnp.where(diff >= 0, jnp.exp(jnp.maximum(diff, 0.0)[None] * log_g[:, None, None]), 0.0)
    q_decay = jnp.exp((pos + 1.0)[None, :] * log_g[:, None])[..., None]
    k_decay = jnp.exp((C - 1.0 - pos)[None, :] * log_g[:, None])[..., None]
    chunk_decay = jnp.exp(C * log_g)[:, None, None]

    def to_chunks(t):
        return t.reshape(B, n, C, H, t.shape[-1]).transpose(1, 0, 3, 2, 4)

    qc, kc, vc = to_chunks(q), to_chunks(k), to_chunks(v.astype(jnp.float32))

    def step(S, inp):
        qb, kb, vb = inp
        scores = jnp.einsum('bhqd,bhkd->bhqk', qb, kb) * decay_mask
        intra = jnp.einsum('bhqk,bhkv->bhqv', scores, vb)
        cross = jnp.einsum('bhqd,bhdv->bhqv', qb * q_decay, S)
        S_new = chunk_decay * S + jnp.einsum('bhkd,bhkv->bhdv', kb * k_decay, vb)
        return S_new, intra + cross

    s_final, out = lax.scan(step, s0.astype(jnp.float32), (qc, kc, vc))
    out = out.transpose(1, 0, 3, 2, 4).reshape(B, L, H, DV)
    return out, s_final.astype(s0.dtype)


def short_conv(u, buf, w):
    L = u.shape[1]
    full = jnp.concatenate([buf.astype(u.dtype), u], axis=1)
    y = w[0] * full[:, 0:L]
    for i in range(1, CONV_W):
        y = y + w[i] * full[:, i:i + L]
    return y, full[:, -(CONV_W - 1):]


def mixer(xn, pos, conv_buf, ret_state, w_in, conv_w, w_a, w_b, w_o):
    B, L, _ = xn.shape
    proj = jnp.einsum('bld,dc->blc', xn, w_in)
    idx = [int(i) for i in np.cumsum(SPLIT_WIDTHS)[:-1]]
    cb, cc, ch, q, k, v, g, ga, gb = jnp.split(proj, idx, axis=-1)
    conv_out, conv_new = short_conv(cc * ch, conv_buf, conv_w)
    o_a = jnp.einsum('blc,cd->bld', cb * conv_out, w_a)
    qh = rotary(q.reshape(B, L, N_RET_HEADS, RET_DK), pos)
    kh = rotary(k.reshape(B, L, N_RET_HEADS, RET_DK), pos) * (RET_DK ** -0.5)
    vh = v.reshape(B, L, N_RET_HEADS, RET_DV)
    r, ret_new = retention_chunkwise(qh, kh, vh, ret_state)
    r = r * lax.rsqrt(jnp.mean(r * r, axis=-1, keepdims=True) + NORM_EPS)
    r = (jax.nn.silu(g.astype(jnp.float32)) * r.reshape(B, L, RET_VDIM)).astype(xn.dtype)
    o_b = jnp.einsum('blv,vd->bld', r, w_b)
    merged = jax.nn.sigmoid(ga) * o_a + jax.nn.sigmoid(gb) * o_b
    return jnp.einsum('bld,de->ble', merged, w_o), conv_new, ret_new


def expert_ffn(xb, w1, b1, w2, b2):
    h = xb @ w1 + b1
    gate, up = h[:, :D_FF], h[:, D_FF:]
    gate = jnp.minimum(gate, SWIGLU_LIMIT)
    up = jnp.clip(up, -SWIGLU_LIMIT, SWIGLU_LIMIT)
    glu = gate * jax.nn.sigmoid(SWIGLU_ALPHA * gate)
    return ((up + 1.0) * glu) @ w2 + b2


def moe_ffn(x, w_router, b_router, w_e1, b_e1, w_e2, b_e2):
    shp = x.shape
    xt = x.reshape(-1, D_MODEL)
    T = xt.shape[0]
    logits = xt.astype(jnp.float32) @ w_router.astype(jnp.float32) + b_router.astype(jnp.float32)
    top_vals, top_idx = lax.top_k(logits, TOP_K)
    gates = jax.nn.softmax(top_vals, axis=-1)
    TK = T * TOP_K
    flat_e = top_idx.reshape(TK)
    order = jnp.argsort(flat_e, stable=True)
    sorted_e = flat_e[order]
    counts = jnp.zeros((N_EXPERTS,), jnp.int32).at[flat_e].add(1)
    padded = (counts + EXPERT_BLOCK - 1) // EXPERT_BLOCK * EXPERT_BLOCK
    pad_end = jnp.cumsum(padded)
    pad_start = pad_end - padded
    start = jnp.cumsum(counts) - counts
    dest = pad_start[sorted_e] + jnp.arange(TK, dtype=jnp.int32) - start[sorted_e]
    n_blocks = -(-TK // EXPERT_BLOCK) + N_EXPERTS
    rows = n_blocks * EXPERT_BLOCK
    token_of = (order // TOP_K).astype(jnp.int32)
    row_token = jnp.zeros((rows,), jnp.int32).at[dest].set(token_of)
    block_expert = jnp.minimum(
        jnp.searchsorted(pad_end, jnp.arange(n_blocks, dtype=jnp.int32) * EXPERT_BLOCK, side='right'),
        N_EXPERTS - 1).astype(jnp.int32)
    xb = xt[row_token].reshape(n_blocks, EXPERT_BLOCK, D_MODEL)

    def run_block(args):
        blk, e = args
        return expert_ffn(blk, w_e1[e], b_e1[e], w_e2[e], b_e2[e])

    yb = lax.map(run_block, (xb, block_expert)).reshape(rows, D_MODEL)
    y_assign = yb[dest] * gates.reshape(TK)[order][:, None].astype(x.dtype)
    y = jnp.zeros((T, D_MODEL), x.dtype).at[token_of].add(y_assign.astype(x.dtype))
    return y.reshape(shp)


def layer(x, pos, conv_buf, ret_state, norm1, w_in, conv_w, w_a, w_b, w_o,
          norm2, w_router, b_router, w_e1, b_e1, w_e2, b_e2):
    m, conv_new, ret_new = mixer(rmsnorm(x, norm1), pos, conv_buf, ret_state, w_in, conv_w, w_a, w_b, w_o)
    h = x + m
    h = h + moe_ffn(rmsnorm(h, norm2), w_router, b_router, w_e1, b_e1, w_e2, b_e2)
    return h, conv_new, ret_new


def setup_inputs(seed: int = 0) -> dict:
    key = jax.random.key(seed)
    ks = jax.random.split(key, 20)
    f32 = jnp.float32
    nrm = lambda k, s, sc: jax.random.normal(k, s, f32) * sc
    return {
        "x_prompt": nrm(ks[0], (BATCH, SEQ, D_MODEL), 1.0),
        "x_sample": nrm(ks[1], (DEC_BATCH, DEC_SEQ, D_MODEL), 1.0),
        "state_conv": nrm(ks[2], (DEPTH, DEC_BATCH, CONV_W - 1, D_CONV), 0.5),
        "state_ret": nrm(ks[3], (DEPTH, DEC_BATCH, N_RET_HEADS, RET_DK, RET_DV), 0.1),
        "norm1": 1.0 + nrm(ks[4], (DEPTH, D_MODEL), 0.02),
        "w_in": nrm(ks[5], (DEPTH, D_MODEL, N_IN_COLS), D_MODEL ** -0.5),
        "conv_w": nrm(ks[6], (DEPTH, CONV_W, D_CONV), CONV_W ** -0.5),
        "w_a": nrm(ks[7], (DEPTH, D_CONV, D_MODEL), D_CONV ** -0.5),
        "w_b": nrm(ks[8], (DEPTH, RET_VDIM, D_MODEL), RET_VDIM ** -0.5),
        "w_o": nrm(ks[9], (DEPTH, D_MODEL, D_MODEL), D_MODEL ** -0.5),
        "norm2": 1.0 + nrm(ks[10], (DEPTH, D_MODEL), 0.02),
        "w_router": nrm(ks[11], (DEPTH, D_MODEL, N_EXPERTS), D_MODEL ** -0.5),
        "b_router": nrm(ks[12], (DEPTH, N_EXPERTS), 0.01),
        "w_e1": nrm(ks[13], (DEPTH, N_EXPERTS, D_MODEL, 2 * D_FF), D_MODEL ** -0.5),
        "b_e1": nrm(ks[14], (DEPTH, N_EXPERTS, 2 * D_FF), 0.02),
        "w_e2": nrm(ks[15], (DEPTH, N_EXPERTS, D_FF, D_MODEL), D_FF ** -0.5),
        "b_e2": nrm(ks[16], (DEPTH, N_EXPERTS, D_MODEL), 0.02),
        "norm_f": 1.0 + nrm(ks[17], (D_MODEL,), 0.02),
    }


def reference(x_prompt, x_sample, state_conv, state_ret, norm1, w_in, conv_w, w_a, w_b, w_o,
              norm2, w_router, b_router, w_e1, b_e1, w_e2, b_e2, norm_f):
    B, L = x_prompt.shape[0], x_prompt.shape[1]
    pos_p = jnp.arange(L, dtype=jnp.int32)
    pos_s = PAST_LEN + jnp.arange(x_sample.shape[1], dtype=jnp.int32)
    h_p, h_s = x_prompt, x_sample
    conv_p_l, ret_p_l, conv_s_l, ret_s_l = [], [], [], []
    for l in range(DEPTH):
        conv0 = jnp.zeros((B, CONV_W - 1, D_CONV), x_prompt.dtype)
        ret0 = jnp.zeros((B, N_RET_HEADS, RET_DK, RET_DV), state_ret.dtype)
        h_p, cp, rp = layer(h_p, pos_p, conv0, ret0, norm1[l], w_in[l], conv_w[l], w_a[l], w_b[l], w_o[l],
                            norm2[l], w_router[l], b_router[l], w_e1[l], b_e1[l], w_e2[l], b_e2[l])
        h_s, cs, rs = layer(h_s, pos_s, state_conv[l], state_ret[l], norm1[l], w_in[l], conv_w[l], w_a[l],
                            w_b[l], w_o[l], norm2[l], w_router[l], b_router[l], w_e1[l], b_e1[l], w_e2[l], b_e2[l])
        conv_p_l.append(cp); ret_p_l.append(rp); conv_s_l.append(cs); ret_s_l.append(rs)
    y_prompt = rmsnorm(h_p, norm_f)
    y_sample = rmsnorm(h_s, norm_f)
    conv_prompt = jnp.stack(conv_p_l)
    ret_prompt = jnp.stack(ret_p_l)
    conv_sample = jnp.stack(conv_s_l)
    ret_sample = jnp.stack(ret_s_l)
    return (y_prompt, y_sample, conv_prompt, ret_prompt, conv_sample, ret_sample)
```

```python
import functools
import math

import jax
import jax.numpy as jnp
from jax import lax
from jax.experimental import pallas as pl
from jax.experimental.pallas import tpu as pltpu

F32 = jnp.float32
BF16 = jnp.bfloat16

D_MODEL = 1024
CONV_W = 3
N_HEADS = 4
DK = 256
DV = 512
HALF = DK // 2
RET_QK = N_HEADS * DK
RET_V = N_HEADS * DV
ROPE_BASE = 10000.0
N_EXPERTS = 32
TOP_K = 4
D_FF = 1024
SWIGLU_ALPHA = 1.702
SWIGLU_LIMIT = 7.0
NORM_EPS = 1e-5
N_IN_COLS = 3 * D_MODEL + 2 * RET_QK + 2 * RET_V + 2 * D_MODEL
COL_CB, COL_CC, COL_CH = 0, D_MODEL, 2 * D_MODEL
COL_Q = 3 * D_MODEL
COL_K = COL_Q + RET_QK
COL_V = COL_K + RET_QK
COL_G = COL_V + RET_V
COL_GA = COL_G + RET_V
COL_GB = COL_GA + D_MODEL

RET_CHUNK = 256
MERGE_TM = 256
INPROJ_TN = 1024
MOE_BM = 256
COMB_TT = 128
SAMPLE_GROUP = 8
LANES = 128
VMEM_LIMIT = 56 * 1024 * 1024


def _params(sem):
    return pltpu.CompilerParams(dimension_semantics=sem, vmem_limit_bytes=VMEM_LIMIT)


def _inproj_kernel(x_ref, g_ref, w_ref, o_ref, xn_ref):
    @pl.when(pl.program_id(1) == 0)
    def _():
        x = x_ref[...]
        ms = jnp.mean(x * x, axis=-1, keepdims=True)
        xn_ref[...] = ((x * lax.rsqrt(ms + NORM_EPS)) * g_ref[...]).astype(BF16)

    o_ref[...] = jnp.dot(xn_ref[...], w_ref[...], preferred_element_type=F32).astype(o_ref.dtype)


def _inproj(x, gain, w_bf16, tm, out_dtype):
    m = x.shape[0]
    n = w_bf16.shape[1]
    return pl.pallas_call(
        _inproj_kernel,
        grid=(m // tm, n // INPROJ_TN),
        in_specs=[
            pl.BlockSpec((tm, D_MODEL), lambda i, j: (i, 0)),
            pl.BlockSpec((1, D_MODEL), lambda i, j: (0, 0)),
            pl.BlockSpec((D_MODEL, INPROJ_TN), lambda i, j: (0, j)),
        ],
        out_specs=pl.BlockSpec((tm, INPROJ_TN), lambda i, j: (i, j)),
        out_shape=jax.ShapeDtypeStruct((m, n), out_dtype),
        scratch_shapes=[pltpu.VMEM((tm, D_MODEL), BF16)],
        compiler_params=_params(("arbitrary", "arbitrary")),
        name="inproj",
    )(x, gain, w_bf16)


def _rotary(x, cos, sin):
    x1 = x[:, :HALF]
    x2 = x[:, HALF:]
    return jnp.concatenate([x1 * cos - x2 * sin, x1 * sin + x2 * cos], axis=-1)


def _group_norm_gate(o, g):
    o = o * lax.rsqrt(jnp.mean(o * o, axis=-1, keepdims=True) + NORM_EPS)
    return (g * jax.nn.sigmoid(g)) * o


def _ret_chunk_kernel(q_ref, k_ref, v0_ref, v1_ref, g0_ref, g1_ref, cos_ref, sin_ref, mask_ref, qd_ref, kd_ref,
                      cd_ref, r_ref, s_ref):
    @pl.when(pl.program_id(1) == 0)
    def _():
        s_ref[...] = jnp.zeros_like(s_ref)

    cos = cos_ref[...]
    sin = sin_ref[...]
    v_refs = (v0_ref, v1_ref)
    g_refs = (g0_ref, g1_ref)
    for h in range(N_HEADS):
        half_cols = slice((h % 2) * DV, (h % 2 + 1) * DV)
        q = _rotary(q_ref[:, h * DK:(h + 1) * DK].astype(F32), cos, sin)
        k = _rotary(k_ref[:, h * DK:(h + 1) * DK].astype(F32), cos, sin) * (DK ** -0.5)
        v = v_refs[h // 2][:, half_cols]
        qd = jnp.concatenate([qd_ref[h], qd_ref[h]], axis=-1)
        kd = jnp.concatenate([kd_ref[h], kd_ref[h]], axis=-1)
        state = s_ref[0, h]
        scores = lax.dot_general(q.astype(BF16), k.astype(BF16), (((1,), (1,)), ((), ())),
                                 preferred_element_type=F32) * mask_ref[h]
        intra = jnp.dot(scores.astype(BF16), v, preferred_element_type=F32)
        cross = jnp.dot((q * qd).astype(BF16), state.astype(BF16), preferred_element_type=F32)
        kv = jnp.dot((k * kd).T.astype(BF16), v, preferred_element_type=F32)
        s_ref[0, h] = cd_ref[h] * state + kv
        g = g_refs[h // 2][:, half_cols].astype(F32)
        r_ref[:, h * DV:(h + 1) * DV] = _group_norm_gate(intra + cross, g).astype(r_ref.dtype)


def _retention_tables(chunk, seq_len, pos0):
    log_g = jnp.log1p(-jnp.exp2(-5.0 - jnp.arange(N_HEADS, dtype=F32)))
    pos = jnp.arange(chunk, dtype=F32)
    diff = pos[:, None] - pos[None, :]
    mask = jnp.where(diff >= 0, jnp.exp(jnp.maximum(diff, 0.0)[None] * log_g[:, None, None]), 0.0)
    q_decay = jnp.exp((pos + 1.0)[None, :] * log_g[:, None])[..., None]
    k_decay = jnp.exp((chunk - 1.0 - pos)[None, :] * log_g[:, None])[..., None]
    chunk_decay = jnp.exp(chunk * log_g)[:, None, None]
    qd = jnp.broadcast_to(q_decay, (N_HEADS, chunk, LANES))
    kd = jnp.broadcast_to(k_decay, (N_HEADS, chunk, LANES))
    cd = jnp.broadcast_to(chunk_decay, (N_HEADS, 1, DV))
    positions = pos0 + jnp.arange(seq_len, dtype=jnp.int32)
    inv = ROPE_BASE ** (-jnp.linspace(0.0, 1.0, HALF, dtype=F32))
    ang = positions.astype(F32)[:, None] * inv[None, :]
    return mask, qd, kd, cd, jnp.cos(ang), jnp.sin(ang)


def _retention_prompt(proj, batch, seq_len):
    c = RET_CHUNK
    nc = seq_len // c
    mask, qd, kd, cd, cos, sin = _retention_tables(c, seq_len, 0)
    row = lambda b, j: b * nc + j
    full3 = lambda b, j: (0, 0, 0)
    return pl.pallas_call(
        _ret_chunk_kernel,
        grid=(batch, nc),
        in_specs=[
            pl.BlockSpec((c, RET_QK), lambda b, j: (row(b, j), COL_Q // RET_QK)),
            pl.BlockSpec((c, RET_QK), lambda b, j: (row(b, j), COL_K // RET_QK)),
            pl.BlockSpec((c, RET_QK), lambda b, j: (row(b, j), COL_V // RET_QK)),
            pl.BlockSpec((c, RET_QK), lambda b, j: (row(b, j), COL_V // RET_QK + 1)),
            pl.BlockSpec((c, RET_QK), lambda b, j: (row(b, j), COL_G // RET_QK)),
            pl.BlockSpec((c, RET_QK), lambda b, j: (row(b, j), COL_G // RET_QK + 1)),
            pl.BlockSpec((c, HALF), lambda b, j: (j, 0)),
            pl.BlockSpec((c, HALF), lambda b, j: (j, 0)),
            pl.BlockSpec((N_HEADS, c, c), full3),
            pl.BlockSpec((N_HEADS, c, LANES), full3),
            pl.BlockSpec((N_HEADS, c, LANES), full3),
            pl.BlockSpec((N_HEADS, 1, DV), full3),
        ],
        out_specs=[
            pl.BlockSpec((c, RET_V), lambda b, j: (row(b, j), 0)),
            pl.BlockSpec((1, N_HEADS, DK, DV), lambda b, j: (b, 0, 0, 0)),
        ],
        out_shape=[
            jax.ShapeDtypeStruct((batch * seq_len, RET_V), BF16),
            jax.ShapeDtypeStruct((batch, N_HEADS, DK, DV), F32),
        ],
        compiler_params=_params(("arbitrary", "arbitrary")),
        name="retention_chunk",
    )(proj, proj, proj, proj, proj, proj, cos, sin, mask, qd, kd, cd)


def _ret_step_kernel(q_ref, k_ref, v_ref, g_ref, cos_ref, sin_ref, gam_ref, s_ref, r_ref, so_ref):
    cos = cos_ref[...]
    sin = sin_ref[...]
    gam = gam_ref[0]
    q = _rotary(q_ref[...], cos, sin)
    k = _rotary(k_ref[...], cos, sin) * (DK ** -0.5)
    v = v_ref[...]
    intra = jnp.sum(q * k, axis=-1, keepdims=True) * v
    q_t = (q * gam[:, :DK]).T
    k_t = k.T
    cross = []
    for s in range(SAMPLE_GROUP):
        state = s_ref[s, 0]
        cross.append(jnp.sum(q_t[:, s:s + 1] * state, axis=0, keepdims=True))
        so_ref[s, 0] = gam * state + k_t[:, s:s + 1] * v[s:s + 1, :]
    o = intra + jnp.concatenate(cross, axis=0)
    r_ref[...] = _group_norm_gate(o, g_ref[...])


def _retention_sample(proj, state, pos0):
    n = proj.shape[0]
    grp = SAMPLE_GROUP
    log_g = jnp.log1p(-jnp.exp2(-5.0 - jnp.arange(N_HEADS, dtype=F32)))
    gam = jnp.broadcast_to(jnp.exp(log_g)[:, None, None], (N_HEADS, 1, DV))
    _, _, _, _, cos, sin = _retention_tables(1, 1, pos0)
    return pl.pallas_call(
        _ret_step_kernel,
        grid=(n // grp, N_HEADS),
        in_specs=[
            pl.BlockSpec((grp, DK), lambda j, h: (j, COL_Q // DK + h)),
            pl.BlockSpec((grp, DK), lambda j, h: (j, COL_K // DK + h)),
            pl.BlockSpec((grp, DV), lambda j, h: (j, COL_V // DV + h)),
            pl.BlockSpec((grp, DV), lambda j, h: (j, COL_G // DV + h)),
            pl.BlockSpec((1, HALF), lambda j, h: (0, 0)),
            pl.BlockSpec((1, HALF), lambda j, h: (0, 0)),
            pl.BlockSpec((1, 1, DV), lambda j, h: (h, 0, 0)),
            pl.BlockSpec((grp, 1, DK, DV), lambda j, h: (j, h, 0, 0)),
        ],
        out_specs=[
            pl.BlockSpec((grp, DV), lambda j, h: (j, h)),
            pl.BlockSpec((grp, 1, DK, DV), lambda j, h: (j, h, 0, 0)),
        ],
        out_shape=[
            jax.ShapeDtypeStruct((n, RET_V), F32),
            jax.ShapeDtypeStruct(state.shape, F32),
        ],
        compiler_params=_params(("arbitrary", "arbitrary")),
        name="retention_step",
    )(proj, proj, proj, proj, cos, sin, gam, state)


def _merge_kernel(per_token_hist, tiles_per_seq, *refs):
    if per_token_hist:
        (cb_ref, cc_ref, ch_ref, ga_ref, gb_ref, r_ref, x_ref, h0_ref, h1_ref, cw_ref, wa_ref, wb_ref,
         wo_ref, n2_ref, wr_ref, br_ref, h_out, hn_out, idx_out, gate_out, u_out) = refs
    else:
        (cb_ref, cc_ref, ch_ref, ga_ref, gb_ref, r_ref, x_ref, cw_ref, wa_ref, wb_ref,
         wo_ref, n2_ref, wr_ref, br_ref, h_out, hn_out, idx_out, gate_out, u_out, hist_ref) = refs
    tm = x_ref.shape[0]
    u = cc_ref[...].astype(F32) * ch_ref[...].astype(F32)
    if per_token_hist:
        u2 = h0_ref[...]
        u1 = h1_ref[...]
        u_out[...] = u
    else:
        @pl.when(pl.program_id(0) % tiles_per_seq == 0)
        def _():
            hist_ref[...] = jnp.zeros_like(hist_ref)

        hm2 = hist_ref[0:1, :]
        hm1 = hist_ref[1:2, :]
        row = lax.broadcasted_iota(jnp.int32, u.shape, 0)
        u1 = jnp.where(row == 0, hm1, pltpu.roll(u, 1, axis=0))
        u2 = jnp.where(row == 0, hm2, jnp.where(row == 1, hm1, pltpu.roll(u, 2, axis=0)))
        last2 = u[tm - (CONV_W - 1):, :]
        hist_ref[...] = last2
        u_out[0] = last2
    conv = cw_ref[0:1, :] * u2 + cw_ref[1:2, :] * u1 + cw_ref[2:3, :] * u
    z = (cb_ref[...].astype(F32) * conv).astype(BF16)
    o_a = jnp.dot(z, wa_ref[...], preferred_element_type=F32)
    o_b = jnp.dot(r_ref[...].astype(BF16), wb_ref[...], preferred_element_type=F32)
    merged = (jax.nn.sigmoid(ga_ref[...].astype(F32)) * o_a
              + jax.nn.sigmoid(gb_ref[...].astype(F32)) * o_b)
    m = jnp.dot(merged.astype(BF16), wo_ref[...], preferred_element_type=F32)
    h = x_ref[...] + m
    h_out[...] = h
    hn = (h * lax.rsqrt(jnp.mean(h * h, axis=-1, keepdims=True) + NORM_EPS)) * n2_ref[...]
    hn_out[...] = hn
    logits = jnp.dot(hn, wr_ref[...], preferred_element_type=F32,
                     precision=lax.Precision.HIGHEST) + br_ref[...]
    lane = lax.broadcasted_iota(jnp.int32, logits.shape, 1)
    vals, idxs = [], []
    for _ in range(TOP_K):
        top = jnp.max(logits, axis=-1, keepdims=True)
        arg = jnp.min(jnp.where(logits == top, lane, N_EXPERTS), axis=-1, keepdims=True)
        vals.append(top)
        idxs.append(arg)
        logits = jnp.where(lane == arg, -jnp.inf, logits)
    exps = [jnp.exp(val - vals[0]) for val in vals]
    denom = exps[0] + exps[1] + exps[2] + exps[3]
    out_lane = lax.broadcasted_iota(jnp.int32, idx_out.shape, 1)
    idx_full = jnp.zeros(idx_out.shape, jnp.int32)
    gate_full = jnp.zeros(gate_out.shape, F32)
    for kk in range(TOP_K):
        idx_full = jnp.where(out_lane == kk, idxs[kk], idx_full)
        gate_full = jnp.where(out_lane == kk, exps[kk] / denom, gate_full)
    idx_out[...] = idx_full
    gate_out[...] = gate_full


def _merge(proj, r, x, hist, conv_w, wa, wb, wo, norm2, w_router, b_router, tm, tiles_per_seq):
    t = x.shape[0]
    per_token = hist is not None
    col = lambda c: (lambda i: (i, c // D_MODEL))
    const2 = lambda i: (0, 0)
    in_specs = [
        pl.BlockSpec((tm, D_MODEL), col(COL_CB)),
        pl.BlockSpec((tm, D_MODEL), col(COL_CC)),
        pl.BlockSpec((tm, D_MODEL), col(COL_CH)),
        pl.BlockSpec((tm, D_MODEL), col(COL_GA)),
        pl.BlockSpec((tm, D_MODEL), col(COL_GB)),
        pl.BlockSpec((tm, RET_V), lambda i: (i, 0)),
        pl.BlockSpec((tm, D_MODEL), lambda i: (i, 0)),
    ]
    args = [proj, proj, proj, proj, proj, r, x]
    if per_token:
        in_specs += [pl.BlockSpec((tm, D_MODEL), lambda i: (i, 0))] * 2
        args += list(hist)
    in_specs += [
        pl.BlockSpec((CONV_W, D_MODEL), const2),
        pl.BlockSpec((D_MODEL, D_MODEL), const2),
        pl.BlockSpec((RET_V, D_MODEL), const2),
        pl.BlockSpec((D_MODEL, D_MODEL), const2),
        pl.BlockSpec((1, D_MODEL), const2),
        pl.BlockSpec((D_MODEL, N_EXPERTS), const2),
        pl.BlockSpec((1, N_EXPERTS), const2),
    ]
    args += [conv_w, wa, wb, wo, norm2, w_router, b_router]
    out_specs = [
        pl.BlockSpec((tm, D_MODEL), lambda i: (i, 0)),
        pl.BlockSpec((tm, D_MODEL), lambda i: (i, 0)),
        pl.BlockSpec((tm, LANES), lambda i: (i, 0)),
        pl.BlockSpec((tm, LANES), lambda i: (i, 0)),
    ]
    out_shape = [
        jax.ShapeDtypeStruct((t, D_MODEL), F32),
        jax.ShapeDtypeStruct((t, D_MODEL), F32),
        jax.ShapeDtypeStruct((t, LANES), jnp.int32),
        jax.ShapeDtypeStruct((t, LANES), F32),
    ]
    if per_token:
        out_specs.append(pl.BlockSpec((tm, D_MODEL), lambda i: (i, 0)))
        out_shape.append(jax.ShapeDtypeStruct((t, D_MODEL), F32))
        scratch = []
    else:
        n_seq = t // (tm * tiles_per_seq)
        out_specs.append(pl.BlockSpec((1, CONV_W - 1, D_MODEL), lambda i: (i // tiles_per_seq, 0, 0)))
        out_shape.append(jax.ShapeDtypeStruct((n_seq, CONV_W - 1, D_MODEL), F32))
        scratch = [pltpu.VMEM((CONV_W - 1, D_MODEL), F32)]
    return pl.pallas_call(
        functools.partial(_merge_kernel, per_token, tiles_per_seq),
        grid=(t // tm,),
        in_specs=in_specs,
        out_specs=out_specs,
        out_shape=out_shape,
        scratch_shapes=scratch,
        compiler_params=_params(("arbitrary",)),
        name="merge_sample" if per_token else "merge_prompt",
    )(*args)


def _moe_kernel(be_ref, rt_ref, nused_ref, hn_hbm, w1_ref, b1_ref, w2_ref, b2_ref, o_ref,
                xbuf, w1b, w2b, sem):
    i = pl.program_id(0)
    n_used = nused_ref[0]
    slot = i % 2

    def row_copy(tok, sl, r):
        return pltpu.make_async_copy(hn_hbm.at[pl.ds(tok, 1), :], xbuf.at[sl, pl.ds(r, 1), :], sem.at[sl])

    def issue(blk, sl):
        def body(r, carry):
            row_copy(rt_ref[blk * MOE_BM + r], sl, r).start()
            return carry

        lax.fori_loop(0, MOE_BM, body, 0)

    @pl.when(i == 0)
    def _():
        issue(0, 0)

    @pl.when(i + 1 < n_used)
    def _():
        issue(i + 1, 1 - slot)

    @pl.when(i < n_used)
    def _():
        pltpu.make_async_copy(hn_hbm.at[pl.ds(0, MOE_BM), :], xbuf.at[slot], sem.at[slot]).wait()
        prev = be_ref[jnp.maximum(i - 1, 0)]

        @pl.when((i == 0) | (be_ref[i] != prev))
        def _():
            w1b[...] = w1_ref[0].astype(BF16)
            w2b[...] = w2_ref[0].astype(BF16)

        x = xbuf[slot].astype(BF16)
        h1 = jnp.dot(x, w1b[...], preferred_element_type=F32) + b1_ref[0]
        gate = jnp.minimum(h1[:, :D_FF], SWIGLU_LIMIT)
        up = jnp.clip(h1[:, D_FF:], -SWIGLU_LIMIT, SWIGLU_LIMIT)
        glu = gate * jax.nn.sigmoid(SWIGLU_ALPHA * gate)
        act = ((up + 1.0) * glu).astype(BF16)
        o_ref[...] = jnp.dot(act, w2b[...], preferred_element_type=F32) + b2_ref[0]

    @pl.when(i >= n_used)
    def _():
        o_ref[...] = jnp.zeros_like(o_ref)


def _moe_blocks(hn, block_expert, row_token, n_used, w1, b1, w2, b2):
    n_blocks = block_expert.shape[0]
    grid_spec = pltpu.PrefetchScalarGridSpec(
        num_scalar_prefetch=3,
        grid=(n_blocks,),
        in_specs=[
            pl.BlockSpec(memory_space=pl.ANY),
            pl.BlockSpec((1, D_MODEL, 2 * D_FF), lambda i, be, rt, nu: (be[i], 0, 0)),
            pl.BlockSpec((1, 1, 2 * D_FF), lambda i, be, rt, nu: (be[i], 0, 0)),
            pl.BlockSpec((1, D_FF, D_MODEL), lambda i, be, rt, nu: (be[i], 0, 0)),
            pl.BlockSpec((1, 1, D_MODEL), lambda i, be, rt, nu: (be[i], 0, 0)),
        ],
        out_specs=pl.BlockSpec((MOE_BM, D_MODEL), lambda i, be, rt, nu: (i, 0)),
        scratch_shapes=[
            pltpu.VMEM((2, MOE_BM, D_MODEL), F32),
            pltpu.VMEM((D_MODEL, 2 * D_FF), BF16),
            pltpu.VMEM((D_FF, D_MODEL), BF16),
            pltpu.SemaphoreType.DMA((2,)),
        ],
    )
    return pl.pallas_call(
        _moe_kernel,
        grid_spec=grid_spec,
        out_shape=jax.ShapeDtypeStruct((n_blocks * MOE_BM, D_MODEL), F32),
        compiler_params=_params(("arbitrary",)),
        name="moe_experts",
    )(block_expert, row_token, n_used, hn, w1, b1[:, None, :], w2, b2[:, None, :])


def _combine_kernel(dest_ref, h_ref, gate_ref, nf_ref, yb_hbm, o_ref, ybuf, sem):
    i = pl.program_id(0)
    slot = i % 2

    def issue(tile, sl):
        def body(t, carry):
            base = (tile * COMB_TT + t) * TOP_K
            for kk in range(TOP_K):
                pltpu.make_async_copy(yb_hbm.at[pl.ds(dest_ref[base + kk], 1), :],
                                      ybuf.at[sl, kk, pl.ds(t, 1), :], sem.at[sl]).start()
            return carry

        lax.fori_loop(0, COMB_TT, body, 0)

    @pl.when(i == 0)
    def _():
        issue(0, 0)

    @pl.when(i + 1 < pl.num_programs(0))
    def _():
        issue(i + 1, 1 - slot)

    for kk in range(TOP_K):
        pltpu.make_async_copy(yb_hbm.at[pl.ds(0, COMB_TT), :], ybuf.at[slot, kk], sem.at[slot]).wait()
    gates = gate_ref[...]
    y = gates[:, 0:1] * ybuf[slot, 0]
    for kk in range(1, TOP_K):
        y = y + gates[:, kk:kk + 1] * ybuf[slot, kk]
    h = h_ref[...] + y
    o_ref[...] = (h * lax.rsqrt(jnp.mean(h * h, axis=-1, keepdims=True) + NORM_EPS)) * nf_ref[...]


def _combine(dest, h, gates, norm_f, yb):
    t = h.shape[0]
    grid_spec = pltpu.PrefetchScalarGridSpec(
        num_scalar_prefetch=1,
        grid=(t // COMB_TT,),
        in_specs=[
            pl.BlockSpec((COMB_TT, D_MODEL), lambda i, d: (i, 0)),
            pl.BlockSpec((COMB_TT, LANES), lambda i, d: (i, 0)),
            pl.BlockSpec((1, D_MODEL), lambda i, d: (0, 0)),
            pl.BlockSpec(memory_space=pl.ANY),
        ],
        out_specs=pl.BlockSpec((COMB_TT, D_MODEL), lambda i, d: (i, 0)),
        scratch_shapes=[
            pltpu.VMEM((2, TOP_K, COMB_TT, D_MODEL), F32),
            pltpu.SemaphoreType.DMA((2,)),
        ],
    )
    return pl.pallas_call(
        _combine_kernel,
        grid_spec=grid_spec,
        out_shape=jax.ShapeDtypeStruct((t, D_MODEL), F32),
        compiler_params=_params(("arbitrary",)),
        name="combine",
    )(dest, h, gates, norm_f, yb)


def _routing_plan(top_idx):
    tk = top_idx.shape[0] * TOP_K
    flat_e = top_idx.reshape(tk)
    onehot = (flat_e[:, None] == jnp.arange(N_EXPERTS, dtype=jnp.int32)[None, :]).astype(jnp.int32)
    csum = jnp.cumsum(onehot, axis=0)
    rank = jnp.take_along_axis(csum, flat_e[:, None], axis=1)[:, 0] - 1
    counts = csum[-1]
    padded = (counts + MOE_BM - 1) // MOE_BM * MOE_BM
    pad_end = jnp.cumsum(padded)
    pad_start = pad_end - padded
    dest = (pad_start[flat_e] + rank).astype(jnp.int32)
    n_blocks = -(-tk // MOE_BM) + N_EXPERTS
    token_of = jnp.arange(tk, dtype=jnp.int32) // TOP_K
    row_token = jnp.zeros((n_blocks * MOE_BM,), jnp.int32).at[dest].set(token_of)
    block_expert = jnp.minimum(
        jnp.searchsorted(pad_end, jnp.arange(n_blocks, dtype=jnp.int32) * MOE_BM, side="right"),
        N_EXPERTS - 1).astype(jnp.int32)
    n_used = (pad_end[-1:] // MOE_BM).astype(jnp.int32)
    return dest, row_token, block_expert, n_used


def kernel(x_prompt, x_sample, state_conv, state_ret, norm1, w_in, conv_w, w_a, w_b, w_o, norm2, w_router,
           b_router, w_e1, b_e1, w_e2, b_e2, norm_f):
    batch, seq_len, _ = x_prompt.shape
    n_dec, dec_seq, _ = x_sample.shape
    depth = norm1.shape[0]
    assert dec_seq == 1 and depth == 1, "single-token decode step of a one-layer trunk"
    past_len = 16384
    t_p = batch * seq_len
    lyr = 0
    xp = x_prompt.reshape(t_p, D_MODEL)
    xs = x_sample.reshape(n_dec, D_MODEL)
    g1 = norm1[lyr][None, :]
    w_in_b = w_in[lyr].astype(BF16)
    wa, wb, wo = w_a[lyr].astype(BF16), w_b[lyr].astype(BF16), w_o[lyr].astype(BF16)
    g2 = norm2[lyr][None, :]
    br = b_router[lyr][None, :]

    proj_p = _inproj(xp, g1, w_in_b, 1024, BF16)
    r_p, ret_p = _retention_prompt(proj_p, batch, seq_len)
    h_p, hn_p, idx_p, gate_p, conv_p = _merge(proj_p, r_p, xp, None, conv_w[lyr], wa, wb, wo, g2,
                                              w_router[lyr], br, MERGE_TM, seq_len // MERGE_TM)
    proj_s = _inproj(xs, g1, w_in_b, n_dec, F32)
    r_s, ret_s = _retention_sample(proj_s, state_ret[lyr], past_len)
    hist = (state_conv[lyr][:, 0, :], state_conv[lyr][:, 1, :])
    h_s, hn_s, idx_s, gate_s, u_s = _merge(proj_s, r_s, xs, hist, conv_w[lyr], wa, wb, wo, g2,
                                           w_router[lyr], br, n_dec, 1)
    conv_s = jnp.stack([state_conv[lyr][:, 1, :], u_s], axis=1)

    h_all = jnp.concatenate([h_p, h_s], axis=0)
    hn_all = jnp.concatenate([hn_p, hn_s], axis=0)
    idx_all = jnp.concatenate([idx_p, idx_s], axis=0)[:, :TOP_K]
    gate_all = jnp.concatenate([gate_p, gate_s], axis=0)
    dest, row_token, block_expert, n_used = _routing_plan(idx_all)
    yb = _moe_blocks(hn_all, block_expert, row_token, n_used, w_e1[lyr], b_e1[lyr], w_e2[lyr], b_e2[lyr])
    y_all = _combine(dest, h_all, gate_all, norm_f[None, :], yb)

    y_prompt = y_all[:t_p].reshape(batch, seq_len, D_MODEL)
    y_sample = y_all[t_p:].reshape(n_dec, 1, D_MODEL)
    return (y_prompt, y_sample, conv_p[None], ret_p[None], conv_s[None], ret_s[None])
```

```python
import functools

import jax
import jax.numpy as jnp
from jax import lax
from jax.experimental import pallas as pl
from jax.experimental.pallas import tpu as pltpu

F32 = jnp.float32
BF16 = jnp.bfloat16
I32 = jnp.int32

D_MODEL = 1024
CONV_W = 3
N_HEADS = 4
DK = 256
DV = 512
HALF = DK // 2
RET_QK = N_HEADS * DK
RET_V = N_HEADS * DV
ROPE_BASE = 10000.0
N_EXPERTS = 32
TOP_K = 4
D_FF = 1024
SWIGLU_ALPHA = 1.702
SWIGLU_LIMIT = 7.0
NORM_EPS = 1e-5
N_IN_COLS = 3 * D_MODEL + 2 * RET_QK + 2 * RET_V + 2 * D_MODEL
COL_CB, COL_CC, COL_CH = 0, D_MODEL, 2 * D_MODEL
COL_Q = 3 * D_MODEL
COL_K = COL_Q + RET_QK
COL_V = COL_K + RET_QK
COL_G = COL_V + RET_V
COL_GA = COL_G + RET_V
COL_GB = COL_GA + D_MODEL

RET_CHUNK = 256
MERGE_TM = 256
INPROJ_TN = 1024
MOE_BM = 256
COMB_TT = 128
SAMPLE_GROUP = 8
LANES = 128
SUBLANES = 8
ISSUE_UNROLL = 8
VMEM_LIMIT = 56 * 1024 * 1024


def _params(sem):
    return pltpu.CompilerParams(dimension_semantics=sem, vmem_limit_bytes=VMEM_LIMIT)


def _expert_region_rows(n_tokens):
    return -(-n_tokens // MOE_BM) * MOE_BM


def _inproj_kernel(x_ref, g_ref, w_ref, o_ref, xn_ref):
    @pl.when(pl.program_id(1) == 0)
    def _():
        x = x_ref[...]
        ms = jnp.mean(x * x, axis=-1, keepdims=True)
        xn_ref[...] = ((x * lax.rsqrt(ms + NORM_EPS)) * g_ref[...]).astype(BF16)

    o_ref[...] = jnp.dot(xn_ref[...], w_ref[...], preferred_element_type=F32).astype(o_ref.dtype)


def _inproj(x, gain, w_bf16, tm, out_dtype):
    m = x.shape[0]
    n = w_bf16.shape[1]
    return pl.pallas_call(
        _inproj_kernel,
        grid=(m // tm, n // INPROJ_TN),
        in_specs=[
            pl.BlockSpec((tm, D_MODEL), lambda i, j: (i, 0)),
            pl.BlockSpec((1, D_MODEL), lambda i, j: (0, 0)),
            pl.BlockSpec((D_MODEL, INPROJ_TN), lambda i, j: (0, j)),
        ],
        out_specs=pl.BlockSpec((tm, INPROJ_TN), lambda i, j: (i, j)),
        out_shape=jax.ShapeDtypeStruct((m, n), out_dtype),
        scratch_shapes=[pltpu.VMEM((tm, D_MODEL), BF16)],
        compiler_params=_params(("arbitrary", "arbitrary")),
        name="inproj",
    )(x, gain, w_bf16)


def _rotary(x, cos, sin):
    x1 = x[:, :HALF]
    x2 = x[:, HALF:]
    return jnp.concatenate([x1 * cos - x2 * sin, x1 * sin + x2 * cos], axis=-1)


def _group_norm_gate(o, g):
    o = o * lax.rsqrt(jnp.mean(o * o, axis=-1, keepdims=True) + NORM_EPS)
    return (g * jax.nn.sigmoid(g)) * o


def _ret_chunk_kernel(q_ref, k_ref, v0_ref, v1_ref, g0_ref, g1_ref, cos_ref, sin_ref, mask_ref, qd_ref, kd_ref,
                      cd_ref, r_ref, s_ref):
    @pl.when(pl.program_id(1) == 0)
    def _():
        s_ref[...] = jnp.zeros_like(s_ref)

    cos = cos_ref[...]
    sin = sin_ref[...]
    v_refs = (v0_ref, v1_ref)
    g_refs = (g0_ref, g1_ref)
    for h in range(N_HEADS):
        half_cols = slice((h % 2) * DV, (h % 2 + 1) * DV)
        q = _rotary(q_ref[:, h * DK:(h + 1) * DK].astype(F32), cos, sin)
        k = _rotary(k_ref[:, h * DK:(h + 1) * DK].astype(F32), cos, sin) * (DK ** -0.5)
        v = v_refs[h // 2][:, half_cols]
        qd = jnp.concatenate([qd_ref[h], qd_ref[h]], axis=-1)
        kd = jnp.concatenate([kd_ref[h], kd_ref[h]], axis=-1)
        state = s_ref[0, h]
        scores = lax.dot_general(q.astype(BF16), k.astype(BF16), (((1,), (1,)), ((), ())),
                                 preferred_element_type=F32) * mask_ref[h]
        intra = jnp.dot(scores.astype(BF16), v, preferred_element_type=F32)
        cross = jnp.dot((q * qd).astype(BF16), state.astype(BF16), preferred_element_type=F32)
        kv = jnp.dot((k * kd).T.astype(BF16), v, preferred_element_type=F32)
        s_ref[0, h] = cd_ref[h] * state + kv
        g = g_refs[h // 2][:, half_cols].astype(F32)
        r_ref[:, h * DV:(h + 1) * DV] = _group_norm_gate(intra + cross, g).astype(r_ref.dtype)


def _retention_tables(chunk, seq_len, pos0):
    log_g = jnp.log1p(-jnp.exp2(-5.0 - jnp.arange(N_HEADS, dtype=F32)))
    pos = jnp.arange(chunk, dtype=F32)
    diff = pos[:, None] - pos[None, :]
    mask = jnp.where(diff >= 0, jnp.exp(jnp.maximum(diff, 0.0)[None] * log_g[:, None, None]), 0.0)
    q_decay = jnp.exp((pos + 1.0)[None, :] * log_g[:, None])[..., None]
    k_decay = jnp.exp((chunk - 1.0 - pos)[None, :] * log_g[:, None])[..., None]
    chunk_decay = jnp.exp(chunk * log_g)[:, None, None]
    qd = jnp.broadcast_to(q_decay, (N_HEADS, chunk, LANES))
    kd = jnp.broadcast_to(k_decay, (N_HEADS, chunk, LANES))
    cd = jnp.broadcast_to(chunk_decay, (N_HEADS, 1, DV))
    positions = pos0 + jnp.arange(seq_len, dtype=jnp.int32)
    inv = ROPE_BASE ** (-jnp.linspace(0.0, 1.0, HALF, dtype=F32))
    ang = positions.astype(F32)[:, None] * inv[None, :]
    return mask, qd, kd, cd, jnp.cos(ang), jnp.sin(ang)


def _retention_prompt(proj, batch, seq_len):
    c = RET_CHUNK
    nc = seq_len // c
    mask, qd, kd, cd, cos, sin = _retention_tables(c, seq_len, 0)
    row = lambda b, j: b * nc + j
    full3 = lambda b, j: (0, 0, 0)
    return pl.pallas_call(
        _ret_chunk_kernel,
        grid=(batch, nc),
        in_specs=[
            pl.BlockSpec((c, RET_QK), lambda b, j: (row(b, j), COL_Q // RET_QK)),
            pl.BlockSpec((c, RET_QK), lambda b, j: (row(b, j), COL_K // RET_QK)),
            pl.BlockSpec((c, RET_QK), lambda b, j: (row(b, j), COL_V // RET_QK)),
            pl.BlockSpec((c, RET_QK), lambda b, j: (row(b, j), COL_V // RET_QK + 1)),
            pl.BlockSpec((c, RET_QK), lambda b, j: (row(b, j), COL_G // RET_QK)),
            pl.BlockSpec((c, RET_QK), lambda b, j: (row(b, j), COL_G // RET_QK + 1)),
            pl.BlockSpec((c, HALF), lambda b, j: (j, 0)),
            pl.BlockSpec((c, HALF), lambda b, j: (j, 0)),
            pl.BlockSpec((N_HEADS, c, c), full3),
            pl.BlockSpec((N_HEADS, c, LANES), full3),
            pl.BlockSpec((N_HEADS, c, LANES), full3),
            pl.BlockSpec((N_HEADS, 1, DV), full3),
        ],
        out_specs=[
            pl.BlockSpec((c, RET_V), lambda b, j: (row(b, j), 0)),
            pl.BlockSpec((1, N_HEADS, DK, DV), lambda b, j: (b, 0, 0, 0)),
        ],
        out_shape=[
            jax.ShapeDtypeStruct((batch * seq_len, RET_V), BF16),
            jax.ShapeDtypeStruct((batch, N_HEADS, DK, DV), F32),
        ],
        compiler_params=_params(("arbitrary", "arbitrary")),
        name="retention_chunk",
    )(proj, proj, proj, proj, proj, proj, cos, sin, mask, qd, kd, cd)


def _ret_step_kernel(q_ref, k_ref, v_ref, g_ref, cos_ref, sin_ref, gam_ref, s_ref, r_ref, so_ref):
    cos = cos_ref[...]
    sin = sin_ref[...]
    gam = gam_ref[0]
    q = _rotary(q_ref[...], cos, sin)
    k = _rotary(k_ref[...], cos, sin) * (DK ** -0.5)
    v = v_ref[...]
    intra = jnp.sum(q * k, axis=-1, keepdims=True) * v
    q_t = (q * gam[:, :DK]).T
    k_t = k.T
    cross = []
    for s in range(SAMPLE_GROUP):
        state = s_ref[s, 0]
        cross.append(jnp.sum(q_t[:, s:s + 1] * state, axis=0, keepdims=True))
        so_ref[s, 0] = gam * state + k_t[:, s:s + 1] * v[s:s + 1, :]
    o = intra + jnp.concatenate(cross, axis=0)
    r_ref[...] = _group_norm_gate(o, g_ref[...])


def _retention_sample(proj, state, pos0):
    n = proj.shape[0]
    grp = SAMPLE_GROUP
    log_g = jnp.log1p(-jnp.exp2(-5.0 - jnp.arange(N_HEADS, dtype=F32)))
    gam = jnp.broadcast_to(jnp.exp(log_g)[:, None, None], (N_HEADS, 1, DV))
    _, _, _, _, cos, sin = _retention_tables(1, 1, pos0)
    return pl.pallas_call(
        _ret_step_kernel,
        grid=(n // grp, N_HEADS),
        in_specs=[
            pl.BlockSpec((grp, DK), lambda j, h: (j, COL_Q // DK + h)),
            pl.BlockSpec((grp, DK), lambda j, h: (j, COL_K // DK + h)),
            pl.BlockSpec((grp, DV), lambda j, h: (j, COL_V // DV + h)),
            pl.BlockSpec((grp, DV), lambda j, h: (j, COL_G // DV + h)),
            pl.BlockSpec((1, HALF), lambda j, h: (0, 0)),
            pl.BlockSpec((1, HALF), lambda j, h: (0, 0)),
            pl.BlockSpec((1, 1, DV), lambda j, h: (h, 0, 0)),
            pl.BlockSpec((grp, 1, DK, DV), lambda j, h: (j, h, 0, 0)),
        ],
        out_specs=[
            pl.BlockSpec((grp, DV), lambda j, h: (j, h)),
            pl.BlockSpec((grp, 1, DK, DV), lambda j, h: (j, h, 0, 0)),
        ],
        out_shape=[
            jax.ShapeDtypeStruct((n, RET_V), F32),
            jax.ShapeDtypeStruct(state.shape, F32),
        ],
        compiler_params=_params(("arbitrary", "arbitrary")),
        name="retention_step",
    )(proj, proj, proj, proj, cos, sin, gam, state)


def _merge_kernel(per_token_hist, chained, tiles_per_seq, region_rows, *refs):
    refs = list(refs)
    cb_ref, cc_ref, ch_ref, ga_ref, gb_ref, r_ref, x_ref = refs[:7]
    del refs[:7]
    if per_token_hist:
        h0_ref, h1_ref = refs[:2]
        del refs[:2]
    cw_ref, wa_ref, wb_ref, wo_ref, n2_ref, wr_ref, br_ref, cnt_in = refs[:8]
    del refs[:8]
    if chained:
        del refs[:1]
    h_out, route_out, gate_out, u_out, cnt_out, xs_out = refs[:6]
    del refs[:6]
    if not per_token_hist:
        hist_ref = refs.pop(0)
    cnt_ref, hn_scr, dvm, dsm, sem = refs

    i = pl.program_id(0)
    tm = x_ref.shape[0]

    def wait_scatters():
        for _ in range(TOP_K):
            pltpu.make_async_copy(hn_scr, xs_out.at[pl.ds(0, tm), :], sem.at[0]).wait()

    u = cc_ref[...].astype(F32) * ch_ref[...].astype(F32)
    if per_token_hist:
        u2 = h0_ref[...]
        u1 = h1_ref[...]
        u_out[...] = u
    else:
        @pl.when(i % tiles_per_seq == 0)
        def _():
            hist_ref[...] = jnp.zeros_like(hist_ref)

        hm2 = hist_ref[0:1, :]
        hm1 = hist_ref[1:2, :]
        row = lax.broadcasted_iota(I32, u.shape, 0)
        u1 = jnp.where(row == 0, hm1, pltpu.roll(u, 1, axis=0))
        u2 = jnp.where(row == 0, hm2, jnp.where(row == 1, hm1, pltpu.roll(u, 2, axis=0)))
        last2 = u[tm - (CONV_W - 1):, :]
        hist_ref[...] = last2
        u_out[0] = last2
    conv = cw_ref[0:1, :] * u2 + cw_ref[1:2, :] * u1 + cw_ref[2:3, :] * u
    z = (cb_ref[...].astype(F32) * conv).astype(BF16)
    o_a = jnp.dot(z, wa_ref[...], preferred_element_type=F32)
    o_b = jnp.dot(r_ref[...].astype(BF16), wb_ref[...], preferred_element_type=F32)
    merged = (jax.nn.sigmoid(ga_ref[...].astype(F32)) * o_a
              + jax.nn.sigmoid(gb_ref[...].astype(F32)) * o_b)
    m = jnp.dot(merged.astype(BF16), wo_ref[...], preferred_element_type=F32)
    h = x_ref[...] + m
    h_out[...] = h
    hn = (h * lax.rsqrt(jnp.mean(h * h, axis=-1, keepdims=True) + NORM_EPS)) * n2_ref[...]
    logits = jnp.dot(hn, wr_ref[...], preferred_element_type=F32,
                     precision=lax.Precision.HIGHEST) + br_ref[...]
    lane_e = lax.broadcasted_iota(I32, logits.shape, 1)
    vals, idxs = [], []
    for _ in range(TOP_K):
        top = jnp.max(logits, axis=-1, keepdims=True)
        arg = jnp.min(jnp.where(logits == top, lane_e, N_EXPERTS), axis=-1, keepdims=True)
        vals.append(top)
        idxs.append(arg)
        logits = jnp.where(lane_e == arg, -jnp.inf, logits)
    exps = [jnp.exp(val - vals[0]) for val in vals]
    denom = exps[0] + exps[1] + exps[2] + exps[3]

    @pl.when(i == 0)
    def _():
        cnt_ref[...] = cnt_in[...]

    lane = lax.broadcasted_iota(I32, (tm, LANES), 1)
    picks = [lane == idx for idx in idxs]
    chosen = jnp.zeros((tm, LANES), F32)
    for pick in picks:
        chosen = jnp.where(pick, 1.0, chosen)
    before = (lax.broadcasted_iota(I32, (tm, tm), 0) > lax.broadcasted_iota(I32, (tm, tm), 1))
    prefix = jnp.dot(jnp.where(before, 1.0, 0.0).astype(BF16), chosen.astype(BF16),
                     preferred_element_type=F32)
    base = cnt_ref[...] + prefix
    ranks = [jnp.sum(jnp.where(pick, base, 0.0), axis=-1, keepdims=True).astype(I32) for pick in picks]
    counts = cnt_ref[...] + jnp.sum(chosen, axis=0, keepdims=True)
    cnt_ref[...] = counts
    cnt_out[...] = counts

    route = jnp.zeros((tm, LANES), I32)
    gates = jnp.zeros((tm, LANES), F32)
    dests = jnp.zeros((tm, LANES), F32)
    for kk in range(TOP_K):
        route = jnp.where(lane == kk, idxs[kk], route)
        route = jnp.where(lane == TOP_K + kk, ranks[kk], route)
        gates = jnp.where(lane == kk, exps[kk] / denom, gates)
        dests = jnp.where(lane == kk, (idxs[kk] * region_rows + ranks[kk]).astype(F32), dests)
    route_out[...] = route
    gate_out[...] = gates

    dvm[...] = dests.T[:SUBLANES, :].astype(I32)
    to_smem = pltpu.make_async_copy(dvm, dsm, sem.at[1])
    to_smem.start()

    @pl.when(i > 0)
    def _():
        wait_scatters()

    hn_scr[...] = hn
    to_smem.wait()

    def issue(g, carry):
        for uu in range(ISSUE_UNROLL):
            t = g * ISSUE_UNROLL + uu
            for kk in range(TOP_K):
                pltpu.make_async_copy(hn_scr.at[pl.ds(t, 1), :], xs_out.at[pl.ds(dsm[kk, t], 1), :],
                                      sem.at[0]).start()
        return carry

    lax.fori_loop(0, tm // ISSUE_UNROLL, issue, 0)

    @pl.when(i == pl.num_programs(0) - 1)
    def _():
        wait_scatters()


def _merge(proj, r, x, hist, conv_w, wa, wb, wo, norm2, w_router, b_router, counts_in, xs_prev, region_rows,
           tm, tiles_per_seq):
    t = x.shape[0]
    per_token = hist is not None
    chained = xs_prev is not None
    col = lambda c: (lambda i: (i, c // D_MODEL))
    const2 = lambda i: (0, 0)
    tile = pl.BlockSpec((tm, D_MODEL), lambda i: (i, 0))
    in_specs = [
        pl.BlockSpec((tm, D_MODEL), col(COL_CB)),
        pl.BlockSpec((tm, D_MODEL), col(COL_CC)),
        pl.BlockSpec((tm, D_MODEL), col(COL_CH)),
        pl.BlockSpec((tm, D_MODEL), col(COL_GA)),
        pl.BlockSpec((tm, D_MODEL), col(COL_GB)),
        pl.BlockSpec((tm, RET_V), lambda i: (i, 0)),
        tile,
    ]
    args = [proj, proj, proj, proj, proj, r, x]
    if per_token:
        in_specs += [tile, tile]
        args += list(hist)
    in_specs += [
        pl.BlockSpec((CONV_W, D_MODEL), const2),
        pl.BlockSpec((D_MODEL, D_MODEL), const2),
        pl.BlockSpec((RET_V, D_MODEL), const2),
        pl.BlockSpec((D_MODEL, D_MODEL), const2),
        pl.BlockSpec((1, D_MODEL), const2),
        pl.BlockSpec((D_MODEL, N_EXPERTS), const2),
        pl.BlockSpec((1, N_EXPERTS), const2),
        pl.BlockSpec((1, LANES), const2),
    ]
    args += [conv_w, wa, wb, wo, norm2, w_router, b_router, counts_in]
    aliases = {}
    if chained:
        aliases = {len(args): 5}
        in_specs.append(pl.BlockSpec(memory_space=pl.ANY))
        args.append(xs_prev)
    lanes_tile = pl.BlockSpec((tm, LANES), lambda i: (i, 0))
    if per_token:
        u_spec = tile
        u_shape = jax.ShapeDtypeStruct((t, D_MODEL), F32)
        scratch = []
    else:
        n_seq = t // (tm * tiles_per_seq)
        u_spec = pl.BlockSpec((1, CONV_W - 1, D_MODEL), lambda i: (i // tiles_per_seq, 0, 0))
        u_shape = jax.ShapeDtypeStruct((n_seq, CONV_W - 1, D_MODEL), F32)
        scratch = [pltpu.VMEM((CONV_W - 1, D_MODEL), F32)]
    out_specs = [tile, lanes_tile, lanes_tile, u_spec, pl.BlockSpec((1, LANES), const2),
                 pl.BlockSpec(memory_space=pl.ANY)]
    out_shape = [
        jax.ShapeDtypeStruct((t, D_MODEL), F32),
        jax.ShapeDtypeStruct((t, LANES), I32),
        jax.ShapeDtypeStruct((t, LANES), F32),
        u_shape,
        jax.ShapeDtypeStruct((1, LANES), F32),
        jax.ShapeDtypeStruct((N_EXPERTS * region_rows, D_MODEL), F32),
    ]
    scratch += [
        pltpu.VMEM((1, LANES), F32),
        pltpu.VMEM((tm, D_MODEL), F32),
        pltpu.VMEM((SUBLANES, tm), I32),
        pltpu.SMEM((SUBLANES, tm), I32),
        pltpu.SemaphoreType.DMA((2,)),
    ]
    return pl.pallas_call(
        functools.partial(_merge_kernel, per_token, chained, tiles_per_seq, region_rows),
        grid=(t // tm,),
        in_specs=in_specs,
        out_specs=out_specs,
        out_shape=out_shape,
        scratch_shapes=scratch,
        input_output_aliases=aliases,
        compiler_params=_params(("arbitrary",)),
        name="merge_sample" if per_token else "merge_prompt",
    )(*args)


def _moe_kernel(be_ref, bx_ref, vr_ref, nused_ref, x_ref, w1_ref, b1_ref, w2_ref, b2_ref, o_ref, w1b, w2b):
    i = pl.program_id(0)
    n_used = nused_ref[0]

    @pl.when(i < n_used)
    def _():
        prev = be_ref[jnp.maximum(i - 1, 0)]

        @pl.when((i == 0) | (be_ref[i] != prev))
        def _():
            w1b[...] = w1_ref[0].astype(BF16)
            w2b[...] = w2_ref[0].astype(BF16)

        live = lax.broadcasted_iota(I32, (MOE_BM, 1), 0) < vr_ref[i]
        x = jnp.where(live, x_ref[...], 0.0).astype(BF16)
        h1 = jnp.dot(x, w1b[...], preferred_element_type=F32) + b1_ref[0]
        gate = jnp.minimum(h1[:, :D_FF], SWIGLU_LIMIT)
        up = jnp.clip(h1[:, D_FF:], -SWIGLU_LIMIT, SWIGLU_LIMIT)
        glu = gate * jax.nn.sigmoid(SWIGLU_ALPHA * gate)
        act = ((up + 1.0) * glu).astype(BF16)
        o_ref[...] = jnp.dot(act, w2b[...], preferred_element_type=F32) + b2_ref[0]

    @pl.when(i >= n_used)
    def _():
        o_ref[...] = jnp.zeros_like(o_ref)


def _moe_blocks(xs, block_expert, block_xs, valid_rows, n_used, w1, b1, w2, b2):
    n_blocks = block_expert.shape[0]
    grid_spec = pltpu.PrefetchScalarGridSpec(
        num_scalar_prefetch=4,
        grid=(n_blocks,),
        in_specs=[
            pl.BlockSpec((MOE_BM, D_MODEL), lambda i, be, bx, vr, nu: (bx[i], 0)),
            pl.BlockSpec((1, D_MODEL, 2 * D_FF), lambda i, be, bx, vr, nu: (be[i], 0, 0)),
            pl.BlockSpec((1, 1, 2 * D_FF), lambda i, be, bx, vr, nu: (be[i], 0, 0)),
            pl.BlockSpec((1, D_FF, D_MODEL), lambda i, be, bx, vr, nu: (be[i], 0, 0)),
            pl.BlockSpec((1, 1, D_MODEL), lambda i, be, bx, vr, nu: (be[i], 0, 0)),
        ],
        out_specs=pl.BlockSpec((MOE_BM, D_MODEL), lambda i, be, bx, vr, nu: (i, 0)),
        scratch_shapes=[
            pltpu.VMEM((D_MODEL, 2 * D_FF), BF16),
            pltpu.VMEM((D_FF, D_MODEL), BF16),
        ],
    )
    return pl.pallas_call(
        _moe_kernel,
        grid_spec=grid_spec,
        out_shape=jax.ShapeDtypeStruct((n_blocks * MOE_BM, D_MODEL), F32),
        compiler_params=_params(("arbitrary",)),
        name="moe_experts",
    )(block_expert, block_xs, valid_rows, n_used, xs, w1, b1[:, None, :], w2, b2[:, None, :])


def _block_plan(counts, n_tokens, region_rows):
    n_blocks = -(-n_tokens * TOP_K // MOE_BM) + N_EXPERTS
    blocks_e = (counts + MOE_BM - 1) // MOE_BM
    blk_end = jnp.cumsum(blocks_e)
    blk_start = blk_end - blocks_e
    n_used = blk_end[-1:]
    blk = jnp.minimum(jnp.arange(n_blocks, dtype=I32), n_used - 1)
    expert = jnp.minimum(jnp.searchsorted(blk_end, blk, side="right"), N_EXPERTS - 1).astype(I32)
    within = blk - blk_start[expert]
    block_xs = expert * (region_rows // MOE_BM) + within
    valid = jnp.clip(counts[expert] - within * MOE_BM, 0, MOE_BM)
    return expert, block_xs.astype(I32), valid.astype(I32), n_used.astype(I32), (blk_start * MOE_BM).astype(I32)


def _combine_kernel(n_main, dest_ref, hp_ref, hs_ref, gp_ref, gs_ref, nf_ref, yb_hbm, op_ref, os_ref, ybuf, sem):
    i = pl.program_id(0)
    slot = i % 2

    def issue(tile, sl):
        def body(g, carry):
            for uu in range(ISSUE_UNROLL):
                t = g * ISSUE_UNROLL + uu
                base = (tile * COMB_TT + t) * TOP_K
                for kk in range(TOP_K):
                    pltpu.make_async_copy(yb_hbm.at[pl.ds(dest_ref[base + kk], 1), :],
                                          ybuf.at[sl, kk, pl.ds(t, 1), :], sem.at[sl]).start()
            return carry

        lax.fori_loop(0, COMB_TT // ISSUE_UNROLL, body, 0)

    @pl.when(i == 0)
    def _():
        issue(0, 0)

    @pl.when(i + 1 < pl.num_programs(0))
    def _():
        issue(i + 1, 1 - slot)

    for kk in range(TOP_K):
        pltpu.make_async_copy(yb_hbm.at[pl.ds(0, COMB_TT), :], ybuf.at[slot, kk], sem.at[slot]).wait()

    def finish(h_ref, gate_ref, o_ref):
        gates = gate_ref[...]
        y = gates[:, 0:1] * ybuf[slot, 0]
        for kk in range(1, TOP_K):
            y = y + gates[:, kk:kk + 1] * ybuf[slot, kk]
        h = h_ref[...] + y
        o_ref[...] = (h * lax.rsqrt(jnp.mean(h * h, axis=-1, keepdims=True) + NORM_EPS)) * nf_ref[...]

    @pl.when(i < n_main)
    def _():
        finish(hp_ref, gp_ref, op_ref)

    @pl.when(i >= n_main)
    def _():
        finish(hs_ref, gs_ref, os_ref)


def _combine(dest, h_p, h_s, gate_p, gate_s, norm_f, yb):
    n_main = h_p.shape[0] // COMB_TT
    n_tail = h_s.shape[0] // COMB_TT
    main = lambda i, d: (jnp.minimum(i, n_main - 1), 0)
    tail = lambda i, d: (jnp.maximum(i - n_main, 0), 0)
    grid_spec = pltpu.PrefetchScalarGridSpec(
        num_scalar_prefetch=1,
        grid=(n_main + n_tail,),
        in_specs=[
            pl.BlockSpec((COMB_TT, D_MODEL), main),
            pl.BlockSpec((COMB_TT, D_MODEL), tail),
            pl.BlockSpec((COMB_TT, LANES), main),
            pl.BlockSpec((COMB_TT, LANES), tail),
            pl.BlockSpec((1, D_MODEL), lambda i, d: (0, 0)),
            pl.BlockSpec(memory_space=pl.ANY),
        ],
        out_specs=[
            pl.BlockSpec((COMB_TT, D_MODEL), main),
            pl.BlockSpec((COMB_TT, D_MODEL), tail),
        ],
        scratch_shapes=[
            pltpu.VMEM((2, TOP_K, COMB_TT, D_MODEL), F32),
            pltpu.SemaphoreType.DMA((2,)),
        ],
    )
    return pl.pallas_call(
        functools.partial(_combine_kernel, n_main),
        grid_spec=grid_spec,
        out_shape=[jax.ShapeDtypeStruct(h_p.shape, F32), jax.ShapeDtypeStruct(h_s.shape, F32)],
        compiler_params=_params(("arbitrary",)),
        name="combine",
    )(dest, h_p, h_s, gate_p, gate_s, norm_f, yb)


def kernel(x_prompt, x_sample, state_conv, state_ret, norm1, w_in, conv_w, w_a, w_b, w_o, norm2, w_router,
           b_router, w_e1, b_e1, w_e2, b_e2, norm_f):
    batch, seq_len, _ = x_prompt.shape
    n_dec, dec_seq, _ = x_sample.shape
    depth = norm1.shape[0]
    assert dec_seq == 1 and depth == 1, "single-token decode step of a one-layer trunk"
    past_len = 16384
    t_p = batch * seq_len
    n_tokens = t_p + n_dec
    region_rows = _expert_region_rows(n_tokens)
    lyr = 0
    xp = x_prompt.reshape(t_p, D_MODEL)
    xs = x_sample.reshape(n_dec, D_MODEL)
    g1 = norm1[lyr][None, :]
    w_in_b = w_in[lyr].astype(BF16)
    wa, wb, wo = w_a[lyr].astype(BF16), w_b[lyr].astype(BF16), w_o[lyr].astype(BF16)
    g2 = norm2[lyr][None, :]
    br = b_router[lyr][None, :]

    proj_p = _inproj(xp, g1, w_in_b, 1024, BF16)
    r_p, ret_p = _retention_prompt(proj_p, batch, seq_len)
    h_p, route_p, gate_p, conv_p, counts, dispatch = _merge(
        proj_p, r_p, xp, None, conv_w[lyr], wa, wb, wo, g2, w_router[lyr], br,
        jnp.zeros((1, LANES), F32), None, region_rows, MERGE_TM, seq_len // MERGE_TM)
    proj_s = _inproj(xs, g1, w_in_b, n_dec, F32)
    r_s, ret_s = _retention_sample(proj_s, state_ret[lyr], past_len)
    hist = (state_conv[lyr][:, 0, :], state_conv[lyr][:, 1, :])
    h_s, route_s, gate_s, u_s, counts, dispatch = _merge(
        proj_s, r_s, xs, hist, conv_w[lyr], wa, wb, wo, g2, w_router[lyr], br,
        counts, dispatch, region_rows, n_dec, 1)
    conv_s = jnp.stack([state_conv[lyr][:, 1, :], u_s], axis=1)

    counts_i = counts[0, :N_EXPERTS].astype(I32)
    block_expert, block_xs, valid_rows, n_used, row_start = _block_plan(counts_i, n_tokens, region_rows)
    yb = _moe_blocks(dispatch, block_expert, block_xs, valid_rows, n_used,
                     w_e1[lyr], b_e1[lyr], w_e2[lyr], b_e2[lyr])
    route = jnp.concatenate([route_p[:, :2 * TOP_K], route_s[:, :2 * TOP_K]], axis=0)
    dest = (row_start[route[:, :TOP_K]] + route[:, TOP_K:]).reshape(n_tokens * TOP_K)
    y_p, y_s = _combine(dest, h_p, h_s, gate_p, gate_s, norm_f[None, :], yb)

    y_prompt = y_p.reshape(batch, seq_len, D_MODEL)
    y_sample = y_s.reshape(n_dec, 1, D_MODEL)
    return (y_prompt, y_sample, conv_p[None], ret_p[None], conv_s[None], ret_s[None])
```

```python
import functools

import jax
import jax.numpy as jnp
from jax import lax
from jax.experimental import pallas as pl
from jax.experimental.pallas import tpu as pltpu

F32 = jnp.float32
BF16 = jnp.bfloat16
I32 = jnp.int32

D_MODEL = 1024
CONV_W = 3
N_HEADS = 4
DK = 256
DV = 512
HALF = DK // 2
RET_QK = N_HEADS * DK
RET_V = N_HEADS * DV
ROPE_BASE = 10000.0
N_EXPERTS = 32
TOP_K = 4
D_FF = 1024
SWIGLU_ALPHA = 1.702
SWIGLU_LIMIT = 7.0
NORM_EPS = 1e-5
N_IN_COLS = 3 * D_MODEL + 2 * RET_QK + 2 * RET_V + 2 * D_MODEL
COL_CB, COL_CC, COL_CH = 0, D_MODEL, 2 * D_MODEL
COL_Q = 3 * D_MODEL
COL_K = COL_Q + RET_QK
COL_V = COL_K + RET_QK
COL_G = COL_V + RET_V
COL_GA = COL_G + RET_V
COL_GB = COL_GA + D_MODEL

RET_CHUNK = 256
MERGE_TM = 256
INPROJ_TN = 1024
MOE_BM = 256
COMB_TT = 128
SAMPLE_GROUP = 8
LANES = 128
SUBLANES = 8
ISSUE_UNROLL = 8
VMEM_LIMIT = 56 * 1024 * 1024


def _params(sem):
    return pltpu.CompilerParams(dimension_semantics=sem, vmem_limit_bytes=VMEM_LIMIT)


def _expert_region_rows(n_tokens):
    return -(-n_tokens // MOE_BM) * MOE_BM


def _inproj_kernel(x_ref, g_ref, w_ref, o_ref, xn_ref):
    @pl.when(pl.program_id(1) == 0)
    def _():
        x = x_ref[...]
        ms = jnp.mean(x * x, axis=-1, keepdims=True)
        xn_ref[...] = ((x * lax.rsqrt(ms + NORM_EPS)) * g_ref[...]).astype(BF16)

    o_ref[...] = jnp.dot(xn_ref[...], w_ref[...], preferred_element_type=F32).astype(o_ref.dtype)


def _inproj(x, gain, w_bf16, tm, out_dtype):
    m = x.shape[0]
    n = w_bf16.shape[1]
    return pl.pallas_call(
        _inproj_kernel,
        grid=(m // tm, n // INPROJ_TN),
        in_specs=[
            pl.BlockSpec((tm, D_MODEL), lambda i, j: (i, 0)),
            pl.BlockSpec((1, D_MODEL), lambda i, j: (0, 0)),
            pl.BlockSpec((D_MODEL, INPROJ_TN), lambda i, j: (0, j)),
        ],
        out_specs=pl.BlockSpec((tm, INPROJ_TN), lambda i, j: (i, j)),
        out_shape=jax.ShapeDtypeStruct((m, n), out_dtype),
        scratch_shapes=[pltpu.VMEM((tm, D_MODEL), BF16)],
        compiler_params=_params(("arbitrary", "arbitrary")),
        name="inproj",
    )(x, gain, w_bf16)


def _rotary(x, cos, sin):
    x1 = x[:, :HALF]
    x2 = x[:, HALF:]
    return jnp.concatenate([x1 * cos - x2 * sin, x1 * sin + x2 * cos], axis=-1)


def _group_norm_gate(o, g):
    o = o * lax.rsqrt(jnp.mean(o * o, axis=-1, keepdims=True) + NORM_EPS)
    return (g * jax.nn.sigmoid(g)) * o


def _ret_chunk_kernel(q_ref, k_ref, v0_ref, v1_ref, g0_ref, g1_ref, cos_ref, sin_ref, mask_ref, qd_ref, kd_ref,
                      cd_ref, r_ref, s_ref):
    @pl.when(pl.program_id(1) == 0)
    def _():
        s_ref[...] = jnp.zeros_like(s_ref)

    cos = cos_ref[...]
    sin = sin_ref[...]
    v_refs = (v0_ref, v1_ref)
    g_refs = (g0_ref, g1_ref)
    for h in range(N_HEADS):
        half_cols = slice((h % 2) * DV, (h % 2 + 1) * DV)
        q = _rotary(q_ref[:, h * DK:(h + 1) * DK].astype(F32), cos, sin)
        k = _rotary(k_ref[:, h * DK:(h + 1) * DK].astype(F32), cos, sin) * (DK ** -0.5)
        v = v_refs[h // 2][:, half_cols]
        qd = jnp.concatenate([qd_ref[h], qd_ref[h]], axis=-1)
        kd = jnp.concatenate([kd_ref[h], kd_ref[h]], axis=-1)
        state = s_ref[0, h]
        scores = lax.dot_general(q.astype(BF16), k.astype(BF16), (((1,), (1,)), ((), ())),
                                 preferred_element_type=F32) * mask_ref[h]
        intra = jnp.dot(scores.astype(BF16), v, preferred_element_type=F32)
        cross = jnp.dot((q * qd).astype(BF16), state.astype(BF16), preferred_element_type=F32)
        kv = jnp.dot((k * kd).T.astype(BF16), v, preferred_element_type=F32)
        s_ref[0, h] = cd_ref[h] * state + kv
        g = g_refs[h // 2][:, half_cols].astype(F32)
        r_ref[:, h * DV:(h + 1) * DV] = _group_norm_gate(intra + cross, g).astype(r_ref.dtype)


def _retention_tables(chunk, seq_len, pos0):
    log_g = jnp.log1p(-jnp.exp2(-5.0 - jnp.arange(N_HEADS, dtype=F32)))
    pos = jnp.arange(chunk, dtype=F32)
    diff = pos[:, None] - pos[None, :]
    mask = jnp.where(diff >= 0, jnp.exp(jnp.maximum(diff, 0.0)[None] * log_g[:, None, None]), 0.0)
    q_decay = jnp.exp((pos + 1.0)[None, :] * log_g[:, None])[..., None]
    k_decay = jnp.exp((chunk - 1.0 - pos)[None, :] * log_g[:, None])[..., None]
    chunk_decay = jnp.exp(chunk * log_g)[:, None, None]
    qd = jnp.broadcast_to(q_decay, (N_HEADS, chunk, LANES))
    kd = jnp.broadcast_to(k_decay, (N_HEADS, chunk, LANES))
    cd = jnp.broadcast_to(chunk_decay, (N_HEADS, 1, DV))
    positions = pos0 + jnp.arange(seq_len, dtype=jnp.int32)
    inv = ROPE_BASE ** (-jnp.linspace(0.0, 1.0, HALF, dtype=F32))
    ang = positions.astype(F32)[:, None] * inv[None, :]
    return mask, qd, kd, cd, jnp.cos(ang), jnp.sin(ang)


def _retention_prompt(proj, batch, seq_len):
    c = RET_CHUNK
    nc = seq_len // c
    mask, qd, kd, cd, cos, sin = _retention_tables(c, seq_len, 0)
    row = lambda b, j: b * nc + j
    full3 = lambda b, j: (0, 0, 0)
    return pl.pallas_call(
        _ret_chunk_kernel,
        grid=(batch, nc),
        in_specs=[
            pl.BlockSpec((c, RET_QK), lambda b, j: (row(b, j), COL_Q // RET_QK)),
            pl.BlockSpec((c, RET_QK), lambda b, j: (row(b, j), COL_K // RET_QK)),
            pl.BlockSpec((c, RET_QK), lambda b, j: (row(b, j), COL_V // RET_QK)),
            pl.BlockSpec((c, RET_QK), lambda b, j: (row(b, j), COL_V // RET_QK + 1)),
            pl.BlockSpec((c, RET_QK), lambda b, j: (row(b, j), COL_G // RET_QK)),
            pl.BlockSpec((c, RET_QK), lambda b, j: (row(b, j), COL_G // RET_QK + 1)),
            pl.BlockSpec((c, HALF), lambda b, j: (j, 0)),
            pl.BlockSpec((c, HALF), lambda b, j: (j, 0)),
            pl.BlockSpec((N_HEADS, c, c), full3),
            pl.BlockSpec((N_HEADS, c, LANES), full3),
            pl.BlockSpec((N_HEADS, c, LANES), full3),
            pl.BlockSpec((N_HEADS, 1, DV), full3),
        ],
        out_specs=[
            pl.BlockSpec((c, RET_V), lambda b, j: (row(b, j), 0)),
            pl.BlockSpec((1, N_HEADS, DK, DV), lambda b, j: (b, 0, 0, 0)),
        ],
        out_shape=[
            jax.ShapeDtypeStruct((batch * seq_len, RET_V), BF16),
            jax.ShapeDtypeStruct((batch, N_HEADS, DK, DV), F32),
        ],
        compiler_params=_params(("arbitrary", "arbitrary")),
        name="retention_chunk",
    )(proj, proj, proj, proj, proj, proj, cos, sin, mask, qd, kd, cd)


def _ret_step_kernel(q_ref, k_ref, v_ref, g_ref, cos_ref, sin_ref, gam_ref, s_ref, r_ref, so_ref):
    cos = cos_ref[...]
    sin = sin_ref[...]
    gam = gam_ref[0]
    q = _rotary(q_ref[...], cos, sin)
    k = _rotary(k_ref[...], cos, sin) * (DK ** -0.5)
    v = v_ref[...]
    intra = jnp.sum(q * k, axis=-1, keepdims=True) * v
    q_t = (q * gam[:, :DK]).T
    k_t = k.T
    cross = []
    for s in range(SAMPLE_GROUP):
        state = s_ref[s, 0]
        cross.append(jnp.sum(q_t[:, s:s + 1] * state, axis=0, keepdims=True))
        so_ref[s, 0] = gam * state + k_t[:, s:s + 1] * v[s:s + 1, :]
    o = intra + jnp.concatenate(cross, axis=0)
    r_ref[...] = _group_norm_gate(o, g_ref[...])


def _retention_sample(proj, state, pos0):
    n = proj.shape[0]
    grp = SAMPLE_GROUP
    log_g = jnp.log1p(-jnp.exp2(-5.0 - jnp.arange(N_HEADS, dtype=F32)))
    gam = jnp.broadcast_to(jnp.exp(log_g)[:, None, None], (N_HEADS, 1, DV))
    _, _, _, _, cos, sin = _retention_tables(1, 1, pos0)
    return pl.pallas_call(
        _ret_step_kernel,
        grid=(n // grp, N_HEADS),
        in_specs=[
            pl.BlockSpec((grp, DK), lambda j, h: (j, COL_Q // DK + h)),
            pl.BlockSpec((grp, DK), lambda j, h: (j, COL_K // DK + h)),
            pl.BlockSpec((grp, DV), lambda j, h: (j, COL_V // DV + h)),
            pl.BlockSpec((grp, DV), lambda j, h: (j, COL_G // DV + h)),
            pl.BlockSpec((1, HALF), lambda j, h: (0, 0)),
            pl.BlockSpec((1, HALF), lambda j, h: (0, 0)),
            pl.BlockSpec((1, 1, DV), lambda j, h: (h, 0, 0)),
            pl.BlockSpec((grp, 1, DK, DV), lambda j, h: (j, h, 0, 0)),
        ],
        out_specs=[
            pl.BlockSpec((grp, DV), lambda j, h: (j, h)),
            pl.BlockSpec((grp, 1, DK, DV), lambda j, h: (j, h, 0, 0)),
        ],
        out_shape=[
            jax.ShapeDtypeStruct((n, RET_V), F32),
            jax.ShapeDtypeStruct(state.shape, F32),
        ],
        compiler_params=_params(("arbitrary", "arbitrary")),
        name="retention_step",
    )(proj, proj, proj, proj, cos, sin, gam, state)


def _merge_kernel(per_token_hist, chained, tiles_per_seq, region_rows, *refs):
    refs = list(refs)
    cb_ref, cc_ref, ch_ref, ga_ref, gb_ref, r_ref, x_ref = refs[:7]
    del refs[:7]
    if per_token_hist:
        h0_ref, h1_ref = refs[:2]
        del refs[:2]
    cw_ref, wa_ref, wb_ref, wo_ref, n2_ref, wr_ref, br_ref, cnt_in = refs[:8]
    del refs[:8]
    if chained:
        del refs[:1]
    h_out, route_out, gate_out, u_out, cnt_out, xs_out = refs[:6]
    del refs[:6]
    if not per_token_hist:
        hist_ref = refs.pop(0)
    cnt_ref, hn_scr, dvm, dsm, sem = refs

    i = pl.program_id(0)
    tm = x_ref.shape[0]

    def wait_scatters():
        for _ in range(TOP_K):
            pltpu.make_async_copy(hn_scr, xs_out.at[pl.ds(0, tm), :], sem.at[0]).wait()

    u = cc_ref[...].astype(F32) * ch_ref[...].astype(F32)
    if per_token_hist:
        u2 = h0_ref[...]
        u1 = h1_ref[...]
        u_out[...] = u
    else:
        @pl.when(i % tiles_per_seq == 0)
        def _():
            hist_ref[...] = jnp.zeros_like(hist_ref)

        hm2 = hist_ref[0:1, :]
        hm1 = hist_ref[1:2, :]
        row = lax.broadcasted_iota(I32, u.shape, 0)
        u1 = jnp.where(row == 0, hm1, pltpu.roll(u, 1, axis=0))
        u2 = jnp.where(row == 0, hm2, jnp.where(row == 1, hm1, pltpu.roll(u, 2, axis=0)))
        last2 = u[tm - (CONV_W - 1):, :]
        hist_ref[...] = last2
        u_out[0] = last2
    conv = cw_ref[0:1, :] * u2 + cw_ref[1:2, :] * u1 + cw_ref[2:3, :] * u
    z = (cb_ref[...].astype(F32) * conv).astype(BF16)
    o_a = jnp.dot(z, wa_ref[...], preferred_element_type=F32)
    o_b = jnp.dot(r_ref[...].astype(BF16), wb_ref[...], preferred_element_type=F32)
    merged = (jax.nn.sigmoid(ga_ref[...].astype(F32)) * o_a
              + jax.nn.sigmoid(gb_ref[...].astype(F32)) * o_b)
    m = jnp.dot(merged.astype(BF16), wo_ref[...], preferred_element_type=F32)
    h = x_ref[...] + m
    h_out[...] = h
    hn = (h * lax.rsqrt(jnp.mean(h * h, axis=-1, keepdims=True) + NORM_EPS)) * n2_ref[...]
    hn_hi = hn.astype(BF16)
    hn_lo = (hn - hn_hi.astype(F32)).astype(BF16)
    nt = (((1,), (1,)), ((), ()))
    part = lax.dot_general(wr_ref[...], hn_hi, nt, preferred_element_type=F32)
    logits = (part[:N_EXPERTS] + part[N_EXPERTS:] + br_ref[...]
              + lax.dot_general(wr_ref[:N_EXPERTS, :], hn_lo, nt, preferred_element_type=F32))
    expert = lax.broadcasted_iota(I32, logits.shape, 0)
    vals, idxs = [], []
    for _ in range(TOP_K):
        top = jnp.max(logits, axis=0, keepdims=True)
        arg = jnp.min(jnp.where(logits == top, expert, N_EXPERTS), axis=0, keepdims=True)
        vals.append(top)
        idxs.append(arg)
        logits = jnp.where(expert == arg, -jnp.inf, logits)
    exps = [jnp.exp(val - vals[0]) for val in vals]
    denom = exps[0] + exps[1] + exps[2] + exps[3]

    @pl.when(i == 0)
    def _():
        cnt_ref[...] = cnt_in[...]

    picks = [expert == idx for idx in idxs]
    chosen = jnp.zeros(logits.shape, F32)
    for pick in picks:
        chosen = jnp.where(pick, 1.0, chosen)
    before = (lax.broadcasted_iota(I32, (tm, tm), 0) < lax.broadcasted_iota(I32, (tm, tm), 1))
    prefix = jnp.dot(chosen.astype(BF16), jnp.where(before, 1.0, 0.0).astype(BF16),
                     preferred_element_type=F32)
    counts = cnt_ref[...]
    base = jnp.concatenate([counts] * (tm // LANES), axis=1) + prefix
    ranks = [jnp.sum(jnp.where(pick, base, 0.0), axis=0, keepdims=True).astype(I32) for pick in picks]
    counts = counts + jnp.sum(chosen, axis=1, keepdims=True)
    cnt_ref[...] = counts
    cnt_out[...] = counts

    slot_row = lax.broadcasted_iota(I32, (SUBLANES, tm), 0)
    route = jnp.zeros((SUBLANES, tm), I32)
    gates = jnp.zeros((SUBLANES, tm), F32)
    dests = jnp.zeros((SUBLANES, tm), I32)
    for kk in range(TOP_K):
        route = jnp.where(slot_row == kk, idxs[kk], route)
        route = jnp.where(slot_row == TOP_K + kk, ranks[kk], route)
        gates = jnp.where(slot_row == kk, exps[kk] / denom, gates)
        dests = jnp.where(slot_row == kk, idxs[kk] * region_rows + ranks[kk], dests)
    route_out[...] = route
    gate_out[...] = gates

    dvm[...] = dests
    to_smem = pltpu.make_async_copy(dvm, dsm, sem.at[1])
    to_smem.start()

    @pl.when(i > 0)
    def _():
        wait_scatters()

    hn_scr[...] = hn
    to_smem.wait()

    def issue(g, carry):
        for uu in range(ISSUE_UNROLL):
            t = g * ISSUE_UNROLL + uu
            for kk in range(TOP_K):
                pltpu.make_async_copy(hn_scr.at[pl.ds(t, 1), :], xs_out.at[pl.ds(dsm[kk, t], 1), :],
                                      sem.at[0]).start()
        return carry

    lax.fori_loop(0, tm // ISSUE_UNROLL, issue, 0)

    @pl.when(i == pl.num_programs(0) - 1)
    def _():
        wait_scatters()


def _merge(proj, r, x, hist, conv_w, wa, wb, wo, norm2, w_router, b_router, counts_in, xs_prev, region_rows,
           tm, tiles_per_seq):
    t = x.shape[0]
    per_token = hist is not None
    chained = xs_prev is not None
    col = lambda c: (lambda i: (i, c // D_MODEL))
    const2 = lambda i: (0, 0)
    tile = pl.BlockSpec((tm, D_MODEL), lambda i: (i, 0))
    in_specs = [
        pl.BlockSpec((tm, D_MODEL), col(COL_CB)),
        pl.BlockSpec((tm, D_MODEL), col(COL_CC)),
        pl.BlockSpec((tm, D_MODEL), col(COL_CH)),
        pl.BlockSpec((tm, D_MODEL), col(COL_GA)),
        pl.BlockSpec((tm, D_MODEL), col(COL_GB)),
        pl.BlockSpec((tm, RET_V), lambda i: (i, 0)),
        tile,
    ]
    args = [proj, proj, proj, proj, proj, r, x]
    if per_token:
        in_specs += [tile, tile]
        args += list(hist)
    in_specs += [
        pl.BlockSpec((CONV_W, D_MODEL), const2),
        pl.BlockSpec((D_MODEL, D_MODEL), const2),
        pl.BlockSpec((RET_V, D_MODEL), const2),
        pl.BlockSpec((D_MODEL, D_MODEL), const2),
        pl.BlockSpec((1, D_MODEL), const2),
        pl.BlockSpec((2 * N_EXPERTS, D_MODEL), const2),
        pl.BlockSpec((N_EXPERTS, tm), const2),
        pl.BlockSpec((N_EXPERTS, LANES), const2),
    ]
    args += [conv_w, wa, wb, wo, norm2, w_router, b_router, counts_in]
    aliases = {}
    if chained:
        aliases = {len(args): 5}
        in_specs.append(pl.BlockSpec(memory_space=pl.ANY))
        args.append(xs_prev)
    lanes_tile = pl.BlockSpec((SUBLANES, tm), lambda i: (0, i))
    if per_token:
        u_spec = tile
        u_shape = jax.ShapeDtypeStruct((t, D_MODEL), F32)
        scratch = []
    else:
        n_seq = t // (tm * tiles_per_seq)
        u_spec = pl.BlockSpec((1, CONV_W - 1, D_MODEL), lambda i: (i // tiles_per_seq, 0, 0))
        u_shape = jax.ShapeDtypeStruct((n_seq, CONV_W - 1, D_MODEL), F32)
        scratch = [pltpu.VMEM((CONV_W - 1, D_MODEL), F32)]
    out_specs = [tile, lanes_tile, lanes_tile, u_spec, pl.BlockSpec((N_EXPERTS, LANES), const2),
                 pl.BlockSpec(memory_space=pl.ANY)]
    out_shape = [
        jax.ShapeDtypeStruct((t, D_MODEL), F32),
        jax.ShapeDtypeStruct((SUBLANES, t), I32),
        jax.ShapeDtypeStruct((SUBLANES, t), F32),
        u_shape,
        jax.ShapeDtypeStruct((N_EXPERTS, LANES), F32),
        jax.ShapeDtypeStruct((N_EXPERTS * region_rows, D_MODEL), F32),
    ]
    scratch += [
        pltpu.VMEM((N_EXPERTS, LANES), F32),
        pltpu.VMEM((tm, D_MODEL), F32),
        pltpu.VMEM((SUBLANES, tm), I32),
        pltpu.SMEM((SUBLANES, tm), I32),
        pltpu.SemaphoreType.DMA((2,)),
    ]
    return pl.pallas_call(
        functools.partial(_merge_kernel, per_token, chained, tiles_per_seq, region_rows),
        grid=(t // tm,),
        in_specs=in_specs,
        out_specs=out_specs,
        out_shape=out_shape,
        scratch_shapes=scratch,
        input_output_aliases=aliases,
        compiler_params=_params(("arbitrary",)),
        name="merge_sample" if per_token else "merge_prompt",
    )(*args)


def _moe_kernel(be_ref, bx_ref, vr_ref, nused_ref, x_ref, w1_ref, b1_ref, w2_ref, b2_ref, o_ref, w1b, w2b):
    i = pl.program_id(0)
    n_used = nused_ref[0]

    @pl.when(i < n_used)
    def _():
        prev = be_ref[jnp.maximum(i - 1, 0)]

        @pl.when((i == 0) | (be_ref[i] != prev))
        def _():
            w1b[...] = w1_ref[0].astype(BF16)
            w2b[...] = w2_ref[0].astype(BF16)

        live = lax.broadcasted_iota(I32, (MOE_BM, 1), 0) < vr_ref[i]
        x = jnp.where(live, x_ref[...], 0.0).astype(BF16)
        h1 = jnp.dot(x, w1b[...], preferred_element_type=F32) + b1_ref[0]
        gate = jnp.minimum(h1[:, :D_FF], SWIGLU_LIMIT)
        up = jnp.clip(h1[:, D_FF:], -SWIGLU_LIMIT, SWIGLU_LIMIT)
        glu = gate * jax.nn.sigmoid(SWIGLU_ALPHA * gate)
        act = ((up + 1.0) * glu).astype(BF16)
        o_ref[...] = jnp.dot(act, w2b[...], preferred_element_type=F32) + b2_ref[0]

    @pl.when(i >= n_used)
    def _():
        o_ref[...] = jnp.zeros_like(o_ref)


def _moe_blocks(xs, block_expert, block_xs, valid_rows, n_used, w1, b1, w2, b2):
    n_blocks = block_expert.shape[0]
    grid_spec = pltpu.PrefetchScalarGridSpec(
        num_scalar_prefetch=4,
        grid=(n_blocks,),
        in_specs=[
            pl.BlockSpec((MOE_BM, D_MODEL), lambda i, be, bx, vr, nu: (bx[i], 0)),
            pl.BlockSpec((1, D_MODEL, 2 * D_FF), lambda i, be, bx, vr, nu: (be[i], 0, 0)),
            pl.BlockSpec((1, 1, 2 * D_FF), lambda i, be, bx, vr, nu: (be[i], 0, 0)),
            pl.BlockSpec((1, D_FF, D_MODEL), lambda i, be, bx, vr, nu: (be[i], 0, 0)),
            pl.BlockSpec((1, 1, D_MODEL), lambda i, be, bx, vr, nu: (be[i], 0, 0)),
        ],
        out_specs=pl.BlockSpec((MOE_BM, D_MODEL), lambda i, be, bx, vr, nu: (i, 0)),
        scratch_shapes=[
            pltpu.VMEM((D_MODEL, 2 * D_FF), BF16),
            pltpu.VMEM((D_FF, D_MODEL), BF16),
        ],
    )
    return pl.pallas_call(
        _moe_kernel,
        grid_spec=grid_spec,
        out_shape=jax.ShapeDtypeStruct((n_blocks * MOE_BM, D_MODEL), F32),
        compiler_params=_params(("arbitrary",)),
        name="moe_experts",
    )(block_expert, block_xs, valid_rows, n_used, xs, w1, b1[:, None, :], w2, b2[:, None, :])


def _block_plan(counts, n_tokens, region_rows):
    n_blocks = -(-n_tokens * TOP_K // MOE_BM) + N_EXPERTS
    blocks_e = (counts + MOE_BM - 1) // MOE_BM
    blk_end = jnp.cumsum(blocks_e)
    blk_start = blk_end - blocks_e
    n_used = blk_end[-1:]
    blk = jnp.maximum(jnp.minimum(jnp.arange(n_blocks, dtype=I32), n_used - 1), 0)
    expert = jnp.minimum(jnp.sum((blk_end[None, :] <= blk[:, None]).astype(I32), axis=1), N_EXPERTS - 1)
    within = blk - blk_start[expert]
    block_xs = expert * (region_rows // MOE_BM) + within
    valid = jnp.clip(counts[expert] - within * MOE_BM, 0, MOE_BM)
    return expert, block_xs.astype(I32), valid.astype(I32), n_used.astype(I32), (blk_start * MOE_BM).astype(I32)


def _combine_kernel(n_main, n_tokens, dest_ref, hp_ref, hs_ref, gp_ref, gs_ref, nf_ref, yb_hbm, op_ref, os_ref,
                    ybuf, sem):
    i = pl.program_id(0)
    slot = i % 2

    def issue(tile, sl):
        def body(g, carry):
            for uu in range(ISSUE_UNROLL):
                t = g * ISSUE_UNROLL + uu
                for kk in range(TOP_K):
                    row = dest_ref[kk * n_tokens + tile * COMB_TT + t]
                    pltpu.make_async_copy(yb_hbm.at[pl.ds(row, 1), :],
                                          ybuf.at[sl, kk, pl.ds(t, 1), :], sem.at[sl]).start()
            return carry

        lax.fori_loop(0, COMB_TT // ISSUE_UNROLL, body, 0)

    @pl.when(i == 0)
    def _():
        issue(0, 0)

    @pl.when(i + 1 < pl.num_programs(0))
    def _():
        issue(i + 1, 1 - slot)

    for kk in range(TOP_K):
        pltpu.make_async_copy(yb_hbm.at[pl.ds(0, COMB_TT), :], ybuf.at[slot, kk], sem.at[slot]).wait()

    def finish(h_ref, gate_ref, o_ref):
        pad = jnp.zeros((COMB_TT - SUBLANES, COMB_TT), F32)
        gates = jnp.concatenate([gate_ref[...], pad], axis=0).T
        y = gates[:, 0:1] * ybuf[slot, 0]
        for kk in range(1, TOP_K):
            y = y + gates[:, kk:kk + 1] * ybuf[slot, kk]
        h = h_ref[...] + y
        o_ref[...] = (h * lax.rsqrt(jnp.mean(h * h, axis=-1, keepdims=True) + NORM_EPS)) * nf_ref[...]

    @pl.when(i < n_main)
    def _():
        finish(hp_ref, gp_ref, op_ref)

    @pl.when(i >= n_main)
    def _():
        finish(hs_ref, gs_ref, os_ref)


def _combine(dest, h_p, h_s, gate_p, gate_s, norm_f, yb):
    n_main = h_p.shape[0] // COMB_TT
    n_tail = h_s.shape[0] // COMB_TT
    main = lambda i, d: (jnp.minimum(i, n_main - 1), 0)
    tail = lambda i, d: (jnp.maximum(i - n_main, 0), 0)
    grid_spec = pltpu.PrefetchScalarGridSpec(
        num_scalar_prefetch=1,
        grid=(n_main + n_tail,),
        in_specs=[
            pl.BlockSpec((COMB_TT, D_MODEL), main),
            pl.BlockSpec((COMB_TT, D_MODEL), tail),
            pl.BlockSpec((SUBLANES, COMB_TT), lambda i, d: (0, jnp.minimum(i, n_main - 1))),
            pl.BlockSpec((SUBLANES, COMB_TT), lambda i, d: (0, jnp.maximum(i - n_main, 0))),
            pl.BlockSpec((1, D_MODEL), lambda i, d: (0, 0)),
            pl.BlockSpec(memory_space=pl.ANY),
        ],
        out_specs=[
            pl.BlockSpec((COMB_TT, D_MODEL), main),
            pl.BlockSpec((COMB_TT, D_MODEL), tail),
        ],
        scratch_shapes=[
            pltpu.VMEM((2, TOP_K, COMB_TT, D_MODEL), F32),
            pltpu.SemaphoreType.DMA((2,)),
        ],
    )
    return pl.pallas_call(
        functools.partial(_combine_kernel, n_main, h_p.shape[0] + h_s.shape[0]),
        grid_spec=grid_spec,
        out_shape=[jax.ShapeDtypeStruct(h_p.shape, F32), jax.ShapeDtypeStruct(h_s.shape, F32)],
        compiler_params=_params(("arbitrary",)),
        name="combine",
    )(dest, h_p, h_s, gate_p, gate_s, norm_f, yb)


def kernel(x_prompt, x_sample, state_conv, state_ret, norm1, w_in, conv_w, w_a, w_b, w_o, norm2, w_router,
           b_router, w_e1, b_e1, w_e2, b_e2, norm_f):
    batch, seq_len, _ = x_prompt.shape
    n_dec, dec_seq, _ = x_sample.shape
    depth = norm1.shape[0]
    assert dec_seq == 1 and depth == 1, "single-token decode step of a one-layer trunk"
    past_len = 16384
    t_p = batch * seq_len
    n_tokens = t_p + n_dec
    region_rows = _expert_region_rows(n_tokens)
    lyr = 0
    xp = x_prompt.reshape(t_p, D_MODEL)
    xs = x_sample.reshape(n_dec, D_MODEL)
    g1 = norm1[lyr][None, :]
    w_in_b = w_in[lyr].astype(BF16)
    wa, wb, wo = w_a[lyr].astype(BF16), w_b[lyr].astype(BF16), w_o[lyr].astype(BF16)
    g2 = norm2[lyr][None, :]
    wr_t = w_router[lyr].T
    wr_hi = wr_t.astype(BF16)
    wr_split = jnp.concatenate([wr_hi, (wr_t - wr_hi.astype(F32)).astype(BF16)], axis=0)
    br_col = b_router[lyr][:, None]

    proj_p = _inproj(xp, g1, w_in_b, 1024, BF16)
    r_p, ret_p = _retention_prompt(proj_p, batch, seq_len)
    h_p, route_p, gate_p, conv_p, counts, dispatch = _merge(
        proj_p, r_p, xp, None, conv_w[lyr], wa, wb, wo, g2, wr_split, jnp.broadcast_to(br_col, (N_EXPERTS, MERGE_TM)),
        jnp.zeros((N_EXPERTS, LANES), F32), None, region_rows, MERGE_TM, seq_len // MERGE_TM)
    proj_s = _inproj(xs, g1, w_in_b, n_dec, F32)
    r_s, ret_s = _retention_sample(proj_s, state_ret[lyr], past_len)
    hist = (state_conv[lyr][:, 0, :], state_conv[lyr][:, 1, :])
    h_s, route_s, gate_s, u_s, counts, dispatch = _merge(
        proj_s, r_s, xs, hist, conv_w[lyr], wa, wb, wo, g2, wr_split, jnp.broadcast_to(br_col, (N_EXPERTS, n_dec)),
        counts, dispatch, region_rows, n_dec, 1)
    conv_s = jnp.stack([state_conv[lyr][:, 1, :], u_s], axis=1)

    counts_i = counts[:, 0].astype(I32)
    block_expert, block_xs, valid_rows, n_used, row_start = _block_plan(counts_i, n_tokens, region_rows)
    yb = _moe_blocks(dispatch, block_expert, block_xs, valid_rows, n_used,
                     w_e1[lyr], b_e1[lyr], w_e2[lyr], b_e2[lyr])
    route = jnp.concatenate([route_p, route_s], axis=1)
    dest = (row_start[route[:TOP_K]] + route[TOP_K:]).reshape(TOP_K * n_tokens)
    y_p, y_s = _combine(dest, h_p, h_s, gate_p, gate_s, norm_f[None, :], yb)

    y_prompt = y_p.reshape(batch, seq_len, D_MODEL)
    y_sample = y_s.reshape(n_dec, 1, D_MODEL)
    return (y_prompt, y_sample, conv_p[None], ret_p[None], conv_s[None], ret_s[None])
```

```python
import functools

import jax
import jax.numpy as jnp
from jax import lax
from jax.experimental import pallas as pl
from jax.experimental.pallas import tpu as pltpu

F32 = jnp.float32
BF16 = jnp.bfloat16
I32 = jnp.int32

D_MODEL = 1024
CONV_W = 3
N_HEADS = 4
DK = 256
DV = 512
HALF = DK // 2
RET_QK = N_HEADS * DK
RET_V = N_HEADS * DV
ROPE_BASE = 10000.0
N_EXPERTS = 32
TOP_K = 4
D_FF = 1024
SWIGLU_ALPHA = 1.702
SWIGLU_LIMIT = 7.0
NORM_EPS = 1e-5
N_IN_COLS = 3 * D_MODEL + 2 * RET_QK + 2 * RET_V + 2 * D_MODEL
COL_CB, COL_CC, COL_CH = 0, D_MODEL, 2 * D_MODEL
COL_Q = 3 * D_MODEL
COL_K = COL_Q + RET_QK
COL_V = COL_K + RET_QK
COL_G = COL_V + RET_V
COL_GA = COL_G + RET_V
COL_GB = COL_GA + D_MODEL

RET_CHUNK = 256
MERGE_TM = 256
INPROJ_TN = 1024
MOE_BM = 256
COMB_TT = 128
SAMPLE_GROUP = 8
LANES = 128
SUBLANES = 8
ISSUE_UNROLL = 8
VMEM_LIMIT = 56 * 1024 * 1024


def _params(sem):
    return pltpu.CompilerParams(dimension_semantics=sem, vmem_limit_bytes=VMEM_LIMIT)


def _expert_region_rows(n_tokens):
    return -(-n_tokens // MOE_BM) * MOE_BM


def _inproj_kernel(x_ref, g_ref, w_ref, o_ref, xn_ref):
    @pl.when(pl.program_id(1) == 0)
    def _():
        x = x_ref[...]
        ms = jnp.mean(x * x, axis=-1, keepdims=True)
        xn_ref[...] = ((x * lax.rsqrt(ms + NORM_EPS)) * g_ref[...]).astype(BF16)

    o_ref[...] = jnp.dot(xn_ref[...], w_ref[...], preferred_element_type=F32).astype(o_ref.dtype)


def _inproj(x, gain, w_bf16, tm, out_dtype):
    m = x.shape[0]
    n = w_bf16.shape[1]
    return pl.pallas_call(
        _inproj_kernel,
        grid=(m // tm, n // INPROJ_TN),
        in_specs=[
            pl.BlockSpec((tm, D_MODEL), lambda i, j: (i, 0)),
            pl.BlockSpec((1, D_MODEL), lambda i, j: (0, 0)),
            pl.BlockSpec((D_MODEL, INPROJ_TN), lambda i, j: (0, j)),
        ],
        out_specs=pl.BlockSpec((tm, INPROJ_TN), lambda i, j: (i, j)),
        out_shape=jax.ShapeDtypeStruct((m, n), out_dtype),
        scratch_shapes=[pltpu.VMEM((tm, D_MODEL), BF16)],
        compiler_params=_params(("arbitrary", "arbitrary")),
        name="inproj",
    )(x, gain, w_bf16)


def _rotary(x, cos, sin):
    x1 = x[:, :HALF]
    x2 = x[:, HALF:]
    return jnp.concatenate([x1 * cos - x2 * sin, x1 * sin + x2 * cos], axis=-1)


def _group_norm_gate(o, g):
    o = o * lax.rsqrt(jnp.mean(o * o, axis=-1, keepdims=True) + NORM_EPS)
    return (g * jax.nn.sigmoid(g)) * o


def _ret_chunk_kernel(q_ref, k_ref, v0_ref, v1_ref, g0_ref, g1_ref, cos_ref, sin_ref, mask_ref, qd_ref, kd_ref,
                      cd_ref, r_ref, s_ref):
    @pl.when(pl.program_id(1) == 0)
    def _():
        s_ref[...] = jnp.zeros_like(s_ref)

    cos = cos_ref[...]
    sin = sin_ref[...]
    v_refs = (v0_ref, v1_ref)
    g_refs = (g0_ref, g1_ref)
    for h in range(N_HEADS):
        half_cols = slice((h % 2) * DV, (h % 2 + 1) * DV)
        q = _rotary(q_ref[:, h * DK:(h + 1) * DK].astype(F32), cos, sin)
        k = _rotary(k_ref[:, h * DK:(h + 1) * DK].astype(F32), cos, sin) * (DK ** -0.5)
        v = v_refs[h // 2][:, half_cols]
        qd = jnp.concatenate([qd_ref[h], qd_ref[h]], axis=-1)
        kd = jnp.concatenate([kd_ref[h], kd_ref[h]], axis=-1)
        state = s_ref[0, h]
        scores = lax.dot_general(q.astype(BF16), k.astype(BF16), (((1,), (1,)), ((), ())),
                                 preferred_element_type=F32) * mask_ref[h]
        intra = jnp.dot(scores.astype(BF16), v, preferred_element_type=F32)
        cross = jnp.dot((q * qd).astype(BF16), state.astype(BF16), preferred_element_type=F32)
        kv = jnp.dot((k * kd).T.astype(BF16), v, preferred_element_type=F32)
        s_ref[0, h] = cd_ref[h] * state + kv
        g = g_refs[h // 2][:, half_cols].astype(F32)
        r_ref[:, h * DV:(h + 1) * DV] = _group_norm_gate(intra + cross, g).astype(r_ref.dtype)


def _retention_tables(chunk, seq_len, pos0):
    log_g = jnp.log1p(-jnp.exp2(-5.0 - jnp.arange(N_HEADS, dtype=F32)))
    pos = jnp.arange(chunk, dtype=F32)
    diff = pos[:, None] - pos[None, :]
    mask = jnp.where(diff >= 0, jnp.exp(jnp.maximum(diff, 0.0)[None] * log_g[:, None, None]), 0.0)
    q_decay = jnp.exp((pos + 1.0)[None, :] * log_g[:, None])[..., None]
    k_decay = jnp.exp((chunk - 1.0 - pos)[None, :] * log_g[:, None])[..., None]
    chunk_decay = jnp.exp(chunk * log_g)[:, None, None]
    qd = jnp.broadcast_to(q_decay, (N_HEADS, chunk, LANES))
    kd = jnp.broadcast_to(k_decay, (N_HEADS, chunk, LANES))
    cd = jnp.broadcast_to(chunk_decay, (N_HEADS, 1, DV))
    positions = pos0 + jnp.arange(seq_len, dtype=jnp.int32)
    inv = ROPE_BASE ** (-jnp.linspace(0.0, 1.0, HALF, dtype=F32))
    ang = positions.astype(F32)[:, None] * inv[None, :]
    return mask, qd, kd, cd, jnp.cos(ang), jnp.sin(ang)


def _retention_prompt(proj, batch, seq_len):
    c = RET_CHUNK
    nc = seq_len // c
    mask, qd, kd, cd, cos, sin = _retention_tables(c, seq_len, 0)
    row = lambda b, j: b * nc + j
    full3 = lambda b, j: (0, 0, 0)
    return pl.pallas_call(
        _ret_chunk_kernel,
        grid=(batch, nc),
        in_specs=[
            pl.BlockSpec((c, RET_QK), lambda b, j: (row(b, j), COL_Q // RET_QK)),
            pl.BlockSpec((c, RET_QK), lambda b, j: (row(b, j), COL_K // RET_QK)),
            pl.BlockSpec((c, RET_QK), lambda b, j: (row(b, j), COL_V // RET_QK)),
            pl.BlockSpec((c, RET_QK), lambda b, j: (row(b, j), COL_V // RET_QK + 1)),
            pl.BlockSpec((c, RET_QK), lambda b, j: (row(b, j), COL_G // RET_QK)),
            pl.BlockSpec((c, RET_QK), lambda b, j: (row(b, j), COL_G // RET_QK + 1)),
            pl.BlockSpec((c, HALF), lambda b, j: (j, 0)),
            pl.BlockSpec((c, HALF), lambda b, j: (j, 0)),
            pl.BlockSpec((N_HEADS, c, c), full3),
            pl.BlockSpec((N_HEADS, c, LANES), full3),
            pl.BlockSpec((N_HEADS, c, LANES), full3),
            pl.BlockSpec((N_HEADS, 1, DV), full3),
        ],
        out_specs=[
            pl.BlockSpec((c, RET_V), lambda b, j: (row(b, j), 0)),
            pl.BlockSpec((1, N_HEADS, DK, DV), lambda b, j: (b, 0, 0, 0)),
        ],
        out_shape=[
            jax.ShapeDtypeStruct((batch * seq_len, RET_V), BF16),
            jax.ShapeDtypeStruct((batch, N_HEADS, DK, DV), F32),
        ],
        compiler_params=_params(("arbitrary", "arbitrary")),
        name="retention_chunk",
    )(proj, proj, proj, proj, proj, proj, cos, sin, mask, qd, kd, cd)


def _ret_step_kernel(q_ref, k_ref, v_ref, g_ref, cos_ref, sin_ref, gam_ref, s_ref, r_ref, so_ref):
    cos = cos_ref[...]
    sin = sin_ref[...]
    gam = gam_ref[0]
    q = _rotary(q_ref[...], cos, sin)
    k = _rotary(k_ref[...], cos, sin) * (DK ** -0.5)
    v = v_ref[...]
    intra = jnp.sum(q * k, axis=-1, keepdims=True) * v
    q_t = (q * gam[:, :DK]).T
    k_t = k.T
    cross = []
    for s in range(SAMPLE_GROUP):
        state = s_ref[s, 0]
        cross.append(jnp.sum(q_t[:, s:s + 1] * state, axis=0, keepdims=True))
        so_ref[s, 0] = gam * state + k_t[:, s:s + 1] * v[s:s + 1, :]
    o = intra + jnp.concatenate(cross, axis=0)
    r_ref[...] = _group_norm_gate(o, g_ref[...])


def _retention_sample(proj, state, pos0):
    n = proj.shape[0]
    grp = SAMPLE_GROUP
    log_g = jnp.log1p(-jnp.exp2(-5.0 - jnp.arange(N_HEADS, dtype=F32)))
    gam = jnp.broadcast_to(jnp.exp(log_g)[:, None, None], (N_HEADS, 1, DV))
    _, _, _, _, cos, sin = _retention_tables(1, 1, pos0)
    return pl.pallas_call(
        _ret_step_kernel,
        grid=(n // grp, N_HEADS),
        in_specs=[
            pl.BlockSpec((grp, DK), lambda j, h: (j, COL_Q // DK + h)),
            pl.BlockSpec((grp, DK), lambda j, h: (j, COL_K // DK + h)),
            pl.BlockSpec((grp, DV), lambda j, h: (j, COL_V // DV + h)),
            pl.BlockSpec((grp, DV), lambda j, h: (j, COL_G // DV + h)),
            pl.BlockSpec((1, HALF), lambda j, h: (0, 0)),
            pl.BlockSpec((1, HALF), lambda j, h: (0, 0)),
            pl.BlockSpec((1, 1, DV), lambda j, h: (h, 0, 0)),
            pl.BlockSpec((grp, 1, DK, DV), lambda j, h: (j, h, 0, 0)),
        ],
        out_specs=[
            pl.BlockSpec((grp, DV), lambda j, h: (j, h)),
            pl.BlockSpec((grp, 1, DK, DV), lambda j, h: (j, h, 0, 0)),
        ],
        out_shape=[
            jax.ShapeDtypeStruct((n, RET_V), F32),
            jax.ShapeDtypeStruct(state.shape, F32),
        ],
        compiler_params=_params(("arbitrary", "arbitrary")),
        name="retention_step",
    )(proj, proj, proj, proj, cos, sin, gam, state)


def _merge_kernel(per_token_hist, chained, tiles_per_seq, region_rows, *refs):
    refs = list(refs)
    cb_ref, cc_ref, ch_ref, ga_ref, gb_ref, r_ref, x_ref = refs[:7]
    del refs[:7]
    if per_token_hist:
        h0_ref, h1_ref = refs[:2]
        del refs[:2]
    cw_ref, wa_ref, wb_ref, wo_ref, n2_ref, wr_ref, br_ref, cnt_in = refs[:8]
    del refs[:8]
    if chained:
        del refs[:1]
    h_out, route_out, gate_out, u_out, cnt_out, xs_out = refs[:6]
    del refs[:6]
    if not per_token_hist:
        hist_ref = refs.pop(0)
    cnt_ref, hn_scr, dvm, dsm, sem = refs

    i = pl.program_id(0)
    tm = x_ref.shape[0]

    def wait_scatters():
        for _ in range(TOP_K):
            pltpu.make_async_copy(hn_scr, xs_out.at[pl.ds(0, tm * SUBLANES), :], sem.at[0]).wait()

    u = cc_ref[...].astype(F32) * ch_ref[...].astype(F32)
    if per_token_hist:
        u2 = h0_ref[...]
        u1 = h1_ref[...]
        u_out[...] = u
    else:
        @pl.when(i % tiles_per_seq == 0)
        def _():
            hist_ref[...] = jnp.zeros_like(hist_ref)

        hm2 = hist_ref[0:1, :]
        hm1 = hist_ref[1:2, :]
        row = lax.broadcasted_iota(I32, u.shape, 0)
        u1 = jnp.where(row == 0, hm1, pltpu.roll(u, 1, axis=0))
        u2 = jnp.where(row == 0, hm2, jnp.where(row == 1, hm1, pltpu.roll(u, 2, axis=0)))
        last2 = u[tm - (CONV_W - 1):, :]
        hist_ref[...] = last2
        u_out[0] = last2
    conv = cw_ref[0:1, :] * u2 + cw_ref[1:2, :] * u1 + cw_ref[2:3, :] * u
    z = (cb_ref[...].astype(F32) * conv).astype(BF16)
    o_a = jnp.dot(z, wa_ref[...], preferred_element_type=F32)
    o_b = jnp.dot(r_ref[...].astype(BF16), wb_ref[...], preferred_element_type=F32)
    merged = (jax.nn.sigmoid(ga_ref[...].astype(F32)) * o_a
              + jax.nn.sigmoid(gb_ref[...].astype(F32)) * o_b)
    m = jnp.dot(merged.astype(BF16), wo_ref[...], preferred_element_type=F32)
    h = x_ref[...] + m
    h_out[...] = h
    hn = (h * lax.rsqrt(jnp.mean(h * h, axis=-1, keepdims=True) + NORM_EPS)) * n2_ref[...]
    hn_hi = hn.astype(BF16)
    hn_lo = (hn - hn_hi.astype(F32)).astype(BF16)
    nt = (((1,), (1,)), ((), ()))
    part = lax.dot_general(wr_ref[...], hn_hi, nt, preferred_element_type=F32)
    logits = (part[:N_EXPERTS] + part[N_EXPERTS:] + br_ref[...]
              + lax.dot_general(wr_ref[:N_EXPERTS, :], hn_lo, nt, preferred_element_type=F32))
    expert = lax.broadcasted_iota(I32, logits.shape, 0)
    vals, idxs = [], []
    for _ in range(TOP_K):
        top = jnp.max(logits, axis=0, keepdims=True)
        arg = jnp.min(jnp.where(logits == top, expert, N_EXPERTS), axis=0, keepdims=True)
        vals.append(top)
        idxs.append(arg)
        logits = jnp.where(expert == arg, -jnp.inf, logits)
    exps = [jnp.exp(val - vals[0]) for val in vals]
    denom = exps[0] + exps[1] + exps[2] + exps[3]

    @pl.when(i == 0)
    def _():
        cnt_ref[...] = cnt_in[...]

    picks = [expert == idx for idx in idxs]
    chosen = jnp.zeros(logits.shape, F32)
    for pick in picks:
        chosen = jnp.where(pick, 1.0, chosen)
    before = (lax.broadcasted_iota(I32, (tm, tm), 0) < lax.broadcasted_iota(I32, (tm, tm), 1))
    prefix = jnp.dot(chosen.astype(BF16), jnp.where(before, 1.0, 0.0).astype(BF16),
                     preferred_element_type=F32)
    counts = cnt_ref[...]
    base = jnp.concatenate([counts] * (tm // LANES), axis=1) + prefix
    ranks = [jnp.sum(jnp.where(pick, base, 0.0), axis=0, keepdims=True).astype(I32) for pick in picks]
    counts = counts + jnp.sum(chosen, axis=1, keepdims=True)
    cnt_ref[...] = counts
    cnt_out[...] = counts

    slot_row = lax.broadcasted_iota(I32, (SUBLANES, tm), 0)
    route = jnp.zeros((SUBLANES, tm), I32)
    gates = jnp.zeros((SUBLANES, tm), F32)
    dests = jnp.zeros((SUBLANES, tm), I32)
    for kk in range(TOP_K):
        route = jnp.where(slot_row == kk, idxs[kk], route)
        route = jnp.where(slot_row == TOP_K + kk, ranks[kk], route)
        gates = jnp.where(slot_row == kk, exps[kk] / denom, gates)
        dests = jnp.where(slot_row == kk, (idxs[kk] * region_rows + ranks[kk]) * SUBLANES, dests)
    route_out[...] = route
    gate_out[...] = gates

    dvm[...] = dests
    to_smem = pltpu.make_async_copy(dvm, dsm, sem.at[1])
    to_smem.start()

    @pl.when(i > 0)
    def _():
        wait_scatters()

    for c in range(D_MODEL // LANES):
        hn_scr[pl.ds(c, tm, stride=SUBLANES), :] = hn[:, c * LANES:(c + 1) * LANES]
    to_smem.wait()

    def issue(g, carry):
        for uu in range(ISSUE_UNROLL):
            t = g * ISSUE_UNROLL + uu
            src = hn_scr.at[pl.ds(pl.multiple_of(t * SUBLANES, SUBLANES), SUBLANES), :]
            for kk in range(TOP_K):
                dst = xs_out.at[pl.ds(pl.multiple_of(dsm[kk, t], SUBLANES), SUBLANES), :]
                pltpu.make_async_copy(src, dst, sem.at[0]).start()
        return carry

    lax.fori_loop(0, tm // ISSUE_UNROLL, issue, 0)

    @pl.when(i == pl.num_programs(0) - 1)
    def _():
        wait_scatters()


def _merge(proj, r, x, hist, conv_w, wa, wb, wo, norm2, w_router, b_router, counts_in, xs_prev, region_rows,
           tm, tiles_per_seq):
    t = x.shape[0]
    per_token = hist is not None
    chained = xs_prev is not None
    col = lambda c: (lambda i: (i, c // D_MODEL))
    const2 = lambda i: (0, 0)
    tile = pl.BlockSpec((tm, D_MODEL), lambda i: (i, 0))
    in_specs = [
        pl.BlockSpec((tm, D_MODEL), col(COL_CB)),
        pl.BlockSpec((tm, D_MODEL), col(COL_CC)),
        pl.BlockSpec((tm, D_MODEL), col(COL_CH)),
        pl.BlockSpec((tm, D_MODEL), col(COL_GA)),
        pl.BlockSpec((tm, D_MODEL), col(COL_GB)),
        pl.BlockSpec((tm, RET_V), lambda i: (i, 0)),
        tile,
    ]
    args = [proj, proj, proj, proj, proj, r, x]
    if per_token:
        in_specs += [tile, tile]
        args += list(hist)
    in_specs += [
        pl.BlockSpec((CONV_W, D_MODEL), const2),
        pl.BlockSpec((D_MODEL, D_MODEL), const2),
        pl.BlockSpec((RET_V, D_MODEL), const2),
        pl.BlockSpec((D_MODEL, D_MODEL), const2),
        pl.BlockSpec((1, D_MODEL), const2),
        pl.BlockSpec((2 * N_EXPERTS, D_MODEL), const2),
        pl.BlockSpec((N_EXPERTS, tm), const2),
        pl.BlockSpec((N_EXPERTS, LANES), const2),
    ]
    args += [conv_w, wa, wb, wo, norm2, w_router, b_router, counts_in]
    aliases = {}
    if chained:
        aliases = {len(args): 5}
        in_specs.append(pl.BlockSpec(memory_space=pl.ANY))
        args.append(xs_prev)
    lanes_tile = pl.BlockSpec((SUBLANES, tm), lambda i: (0, i))
    if per_token:
        u_spec = tile
        u_shape = jax.ShapeDtypeStruct((t, D_MODEL), F32)
        scratch = []
    else:
        n_seq = t // (tm * tiles_per_seq)
        u_spec = pl.BlockSpec((1, CONV_W - 1, D_MODEL), lambda i: (i // tiles_per_seq, 0, 0))
        u_shape = jax.ShapeDtypeStruct((n_seq, CONV_W - 1, D_MODEL), F32)
        scratch = [pltpu.VMEM((CONV_W - 1, D_MODEL), F32)]
    out_specs = [tile, lanes_tile, lanes_tile, u_spec, pl.BlockSpec((N_EXPERTS, LANES), const2),
                 pl.BlockSpec(memory_space=pl.ANY)]
    out_shape = [
        jax.ShapeDtypeStruct((t, D_MODEL), F32),
        jax.ShapeDtypeStruct((SUBLANES, t), I32),
        jax.ShapeDtypeStruct((SUBLANES, t), F32),
        u_shape,
        jax.ShapeDtypeStruct((N_EXPERTS, LANES), F32),
        jax.ShapeDtypeStruct((N_EXPERTS * region_rows * SUBLANES, LANES), F32),
    ]
    scratch += [
        pltpu.VMEM((N_EXPERTS, LANES), F32),
        pltpu.VMEM((tm * SUBLANES, LANES), F32),
        pltpu.VMEM((SUBLANES, tm), I32),
        pltpu.SMEM((SUBLANES, tm), I32),
        pltpu.SemaphoreType.DMA((2,)),
    ]
    return pl.pallas_call(
        functools.partial(_merge_kernel, per_token, chained, tiles_per_seq, region_rows),
        grid=(t // tm,),
        in_specs=in_specs,
        out_specs=out_specs,
        out_shape=out_shape,
        scratch_shapes=scratch,
        input_output_aliases=aliases,
        compiler_params=_params(("arbitrary",)),
        name="merge_sample" if per_token else "merge_prompt",
    )(*args)


def _rows_from_tiles(ref, n_rows):
    return jnp.concatenate([ref[pl.ds(c, n_rows, stride=SUBLANES), :] for c in range(D_MODEL // LANES)], axis=1)


def _rows_to_tiles(ref, rows):
    for c in range(D_MODEL // LANES):
        ref[pl.ds(c, rows.shape[0], stride=SUBLANES), :] = rows[:, c * LANES:(c + 1) * LANES]


def _moe_kernel(be_ref, bx_ref, vr_ref, nused_ref, x_ref, w1_ref, b1_ref, w2_ref, b2_ref, o_ref, w1b, w2b):
    i = pl.program_id(0)
    n_used = nused_ref[0]

    @pl.when(i < n_used)
    def _():
        prev = be_ref[jnp.maximum(i - 1, 0)]

        @pl.when((i == 0) | (be_ref[i] != prev))
        def _():
            w1b[...] = w1_ref[0].astype(BF16)
            w2b[...] = w2_ref[0].astype(BF16)

        live = lax.broadcasted_iota(I32, (MOE_BM, 1), 0) < vr_ref[i]
        x = jnp.where(live, _rows_from_tiles(x_ref, MOE_BM), 0.0).astype(BF16)
        h1 = jnp.dot(x, w1b[...], preferred_element_type=F32) + b1_ref[0]
        gate = jnp.minimum(h1[:, :D_FF], SWIGLU_LIMIT)
        up = jnp.clip(h1[:, D_FF:], -SWIGLU_LIMIT, SWIGLU_LIMIT)
        glu = gate * jax.nn.sigmoid(SWIGLU_ALPHA * gate)
        act = ((up + 1.0) * glu).astype(BF16)
        _rows_to_tiles(o_ref, jnp.dot(act, w2b[...], preferred_element_type=F32) + b2_ref[0])

    @pl.when(i >= n_used)
    def _():
        o_ref[...] = jnp.zeros_like(o_ref)


def _moe_blocks(xs, block_expert, block_xs, valid_rows, n_used, w1, b1, w2, b2):
    n_blocks = block_expert.shape[0]
    grid_spec = pltpu.PrefetchScalarGridSpec(
        num_scalar_prefetch=4,
        grid=(n_blocks,),
        in_specs=[
            pl.BlockSpec((MOE_BM * SUBLANES, LANES), lambda i, be, bx, vr, nu: (bx[i], 0)),
            pl.BlockSpec((1, D_MODEL, 2 * D_FF), lambda i, be, bx, vr, nu: (be[i], 0, 0)),
            pl.BlockSpec((1, 1, 2 * D_FF), lambda i, be, bx, vr, nu: (be[i], 0, 0)),
            pl.BlockSpec((1, D_FF, D_MODEL), lambda i, be, bx, vr, nu: (be[i], 0, 0)),
            pl.BlockSpec((1, 1, D_MODEL), lambda i, be, bx, vr, nu: (be[i], 0, 0)),
        ],
        out_specs=pl.BlockSpec((MOE_BM * SUBLANES, LANES), lambda i, be, bx, vr, nu: (i, 0)),
        scratch_shapes=[
            pltpu.VMEM((D_MODEL, 2 * D_FF), BF16),
            pltpu.VMEM((D_FF, D_MODEL), BF16),
        ],
    )
    return pl.pallas_call(
        _moe_kernel,
        grid_spec=grid_spec,
        out_shape=jax.ShapeDtypeStruct((n_blocks * MOE_BM * SUBLANES, LANES), F32),
        compiler_params=_params(("arbitrary",)),
        name="moe_experts",
    )(block_expert, block_xs, valid_rows, n_used, xs, w1, b1[:, None, :], w2, b2[:, None, :])


def _block_plan(counts, n_tokens, region_rows):
    n_blocks = -(-n_tokens * TOP_K // MOE_BM) + N_EXPERTS
    blocks_e = (counts + MOE_BM - 1) // MOE_BM
    blk_end = jnp.cumsum(blocks_e)
    blk_start = blk_end - blocks_e
    n_used = blk_end[-1:]
    blk = jnp.maximum(jnp.minimum(jnp.arange(n_blocks, dtype=I32), n_used - 1), 0)
    expert = jnp.minimum(jnp.sum((blk_end[None, :] <= blk[:, None]).astype(I32), axis=1), N_EXPERTS - 1)
    within = blk - blk_start[expert]
    block_xs = expert * (region_rows // MOE_BM) + within
    valid = jnp.clip(counts[expert] - within * MOE_BM, 0, MOE_BM)
    return expert, block_xs.astype(I32), valid.astype(I32), n_used.astype(I32), (blk_start * MOE_BM).astype(I32)


def _combine_kernel(n_main, n_tokens, dest_ref, hp_ref, hs_ref, gp_ref, gs_ref, nf_ref, yb_hbm, op_ref, os_ref,
                    ybuf, sem):
    i = pl.program_id(0)
    slot = i % 2

    def issue(tile, sl):
        def body(g, carry):
            for uu in range(ISSUE_UNROLL):
                t = g * ISSUE_UNROLL + uu
                for kk in range(TOP_K):
                    row = pl.multiple_of(dest_ref[kk * n_tokens + tile * COMB_TT + t], SUBLANES)
                    pltpu.make_async_copy(
                        yb_hbm.at[pl.ds(row, SUBLANES), :],
                        ybuf.at[sl, kk, pl.ds(pl.multiple_of(t * SUBLANES, SUBLANES), SUBLANES), :],
                        sem.at[sl]).start()
            return carry

        lax.fori_loop(0, COMB_TT // ISSUE_UNROLL, body, 0)

    @pl.when(i == 0)
    def _():
        issue(0, 0)

    @pl.when(i + 1 < pl.num_programs(0))
    def _():
        issue(i + 1, 1 - slot)

    for kk in range(TOP_K):
        pltpu.make_async_copy(yb_hbm.at[pl.ds(0, COMB_TT * SUBLANES), :], ybuf.at[slot, kk], sem.at[slot]).wait()

    def finish(h_ref, gate_ref, o_ref):
        pad = jnp.zeros((COMB_TT - SUBLANES, COMB_TT), F32)
        gates = jnp.concatenate([gate_ref[...], pad], axis=0).T
        y = gates[:, 0:1] * _rows_from_tiles(ybuf.at[slot, 0], COMB_TT)
        for kk in range(1, TOP_K):
            y = y + gates[:, kk:kk + 1] * _rows_from_tiles(ybuf.at[slot, kk], COMB_TT)
        h = h_ref[...] + y
        o_ref[...] = (h * lax.rsqrt(jnp.mean(h * h, axis=-1, keepdims=True) + NORM_EPS)) * nf_ref[...]

    @pl.when(i < n_main)
    def _():
        finish(hp_ref, gp_ref, op_ref)

    @pl.when(i >= n_main)
    def _():
        finish(hs_ref, gs_ref, os_ref)


def _combine(dest, h_p, h_s, gate_p, gate_s, norm_f, yb):
    n_main = h_p.shape[0] // COMB_TT
    n_tail = h_s.shape[0] // COMB_TT
    main = lambda i, d: (jnp.minimum(i, n_main - 1), 0)
    tail = lambda i, d: (jnp.maximum(i - n_main, 0), 0)
    grid_spec = pltpu.PrefetchScalarGridSpec(
        num_scalar_prefetch=1,
        grid=(n_main + n_tail,),
        in_specs=[
            pl.BlockSpec((COMB_TT, D_MODEL), main),
            pl.BlockSpec((COMB_TT, D_MODEL), tail),
            pl.BlockSpec((SUBLANES, COMB_TT), lambda i, d: (0, jnp.minimum(i, n_main - 1))),
            pl.BlockSpec((SUBLANES, COMB_TT), lambda i, d: (0, jnp.maximum(i - n_main, 0))),
            pl.BlockSpec((1, D_MODEL), lambda i, d: (0, 0)),
            pl.BlockSpec(memory_space=pl.ANY),
        ],
        out_specs=[
            pl.BlockSpec((COMB_TT, D_MODEL), main),
            pl.BlockSpec((COMB_TT, D_MODEL), tail),
        ],
        scratch_shapes=[
            pltpu.VMEM((2, TOP_K, COMB_TT * SUBLANES, LANES), F32),
            pltpu.SemaphoreType.DMA((2,)),
        ],
    )
    return pl.pallas_call(
        functools.partial(_combine_kernel, n_main, h_p.shape[0] + h_s.shape[0]),
        grid_spec=grid_spec,
        out_shape=[jax.ShapeDtypeStruct(h_p.shape, F32), jax.ShapeDtypeStruct(h_s.shape, F32)],
        compiler_params=_params(("arbitrary",)),
        name="combine",
    )(dest, h_p, h_s, gate_p, gate_s, norm_f, yb)


def kernel(x_prompt, x_sample, state_conv, state_ret, norm1, w_in, conv_w, w_a, w_b, w_o, norm2, w_router,
           b_router, w_e1, b_e1, w_e2, b_e2, norm_f):
    batch, seq_len, _ = x_prompt.shape
    n_dec, dec_seq, _ = x_sample.shape
    depth = norm1.shape[0]
    assert dec_seq == 1 and depth == 1, "single-token decode step of a one-layer trunk"
    past_len = 16384
    t_p = batch * seq_len
    n_tokens = t_p + n_dec
    region_rows = _expert_region_rows(n_tokens)
    lyr = 0
    xp = x_prompt.reshape(t_p, D_MODEL)
    xs = x_sample.reshape(n_dec, D_MODEL)
    g1 = norm1[lyr][None, :]
    w_in_b = w_in[lyr].astype(BF16)
    wa, wb, wo = w_a[lyr].astype(BF16), w_b[lyr].astype(BF16), w_o[lyr].astype(BF16)
    g2 = norm2[lyr][None, :]
    wr_t = w_router[lyr].T
    wr_hi = wr_t.astype(BF16)
    wr_split = jnp.concatenate([wr_hi, (wr_t - wr_hi.astype(F32)).astype(BF16)], axis=0)
    br_col = b_router[lyr][:, None]

    proj_p = _inproj(xp, g1, w_in_b, 1024, BF16)
    r_p, ret_p = _retention_prompt(proj_p, batch, seq_len)
    h_p, route_p, gate_p, conv_p, counts, dispatch = _merge(
        proj_p, r_p, xp, None, conv_w[lyr], wa, wb, wo, g2, wr_split, jnp.broadcast_to(br_col, (N_EXPERTS, MERGE_TM)),
        jnp.zeros((N_EXPERTS, LANES), F32), None, region_rows, MERGE_TM, seq_len // MERGE_TM)
    proj_s = _inproj(xs, g1, w_in_b, n_dec, F32)
    r_s, ret_s = _retention_sample(proj_s, state_ret[lyr], past_len)
    hist = (state_conv[lyr][:, 0, :], state_conv[lyr][:, 1, :])
    h_s, route_s, gate_s, u_s, counts, dispatch = _merge(
        proj_s, r_s, xs, hist, conv_w[lyr], wa, wb, wo, g2, wr_split, jnp.broadcast_to(br_col, (N_EXPERTS, n_dec)),
        counts, dispatch, region_rows, n_dec, 1)
    conv_s = jnp.stack([state_conv[lyr][:, 1, :], u_s], axis=1)

    counts_i = counts[:, 0].astype(I32)
    block_expert, block_xs, valid_rows, n_used, row_start = _block_plan(counts_i, n_tokens, region_rows)
    yb = _moe_blocks(dispatch, block_expert, block_xs, valid_rows, n_used,
                     w_e1[lyr], b_e1[lyr], w_e2[lyr], b_e2[lyr])
    route = jnp.concatenate([route_p, route_s], axis=1)
    expert_ids = jnp.arange(N_EXPERTS, dtype=I32)[:, None, None]
    first_row = jnp.sum(jnp.where(route[None, :TOP_K] == expert_ids, row_start[:, None, None], 0), axis=0)
    dest = ((first_row + route[TOP_K:]) * SUBLANES).reshape(TOP_K * n_tokens)
    y_p, y_s = _combine(dest, h_p, h_s, gate_p, gate_s, norm_f[None, :], yb)

    y_prompt = y_p.reshape(batch, seq_len, D_MODEL)
    y_sample = y_s.reshape(n_dec, 1, D_MODEL)
    return (y_prompt, y_sample, conv_p[None], ret_p[None], conv_s[None], ret_s[None])
```

```python
import functools

import jax
import jax.numpy as jnp
from jax import lax
from jax.experimental import pallas as pl
from jax.experimental.pallas import tpu as pltpu

F32 = jnp.float32
BF16 = jnp.bfloat16
I32 = jnp.int32

D_MODEL = 1024
CONV_W = 3
N_HEADS = 4
DK = 256
DV = 512
HALF = DK // 2
RET_QK = N_HEADS * DK
RET_V = N_HEADS * DV
ROPE_BASE = 10000.0
N_EXPERTS = 32
TOP_K = 4
D_FF = 1024
SWIGLU_ALPHA = 1.702
SWIGLU_LIMIT = 7.0
NORM_EPS = 1e-5
N_IN_COLS = 3 * D_MODEL + 2 * RET_QK + 2 * RET_V + 2 * D_MODEL
COL_CB, COL_CC, COL_CH = 0, D_MODEL, 2 * D_MODEL
COL_Q = 3 * D_MODEL
COL_K = COL_Q + RET_QK
COL_V = COL_K + RET_QK
COL_G = COL_V + RET_V
COL_GA = COL_G + RET_V
COL_GB = COL_GA + D_MODEL

RET_CHUNK = 256
MERGE_TM = 256
INPROJ_TN = 1024
MOE_BM = 256
COMB_TT = 128
SAMPLE_GROUP = 8
LANES = 128
SUBLANES = 8
ISSUE_UNROLL = 8
VMEM_LIMIT = 56 * 1024 * 1024


def _params(sem):
    return pltpu.CompilerParams(dimension_semantics=sem, vmem_limit_bytes=VMEM_LIMIT)


def _expert_region_rows(n_tokens):
    return -(-n_tokens // MOE_BM) * MOE_BM


def _inproj_kernel(x_ref, g_ref, w_ref, o_ref, xn_ref):
    @pl.when(pl.program_id(1) == 0)
    def _():
        x = x_ref[...]
        ms = jnp.mean(x * x, axis=-1, keepdims=True)
        xn_ref[...] = ((x * lax.rsqrt(ms + NORM_EPS)) * g_ref[...]).astype(BF16)

    o_ref[...] = jnp.dot(xn_ref[...], w_ref[...], preferred_element_type=F32).astype(o_ref.dtype)


def _inproj(x, gain, w_bf16, tm, out_dtype):
    m = x.shape[0]
    n = w_bf16.shape[1]
    return pl.pallas_call(
        _inproj_kernel,
        grid=(m // tm, n // INPROJ_TN),
        in_specs=[
            pl.BlockSpec((tm, D_MODEL), lambda i, j: (i, 0)),
            pl.BlockSpec((1, D_MODEL), lambda i, j: (0, 0)),
            pl.BlockSpec((D_MODEL, INPROJ_TN), lambda i, j: (0, j)),
        ],
        out_specs=pl.BlockSpec((tm, INPROJ_TN), lambda i, j: (i, j)),
        out_shape=jax.ShapeDtypeStruct((m, n), out_dtype),
        scratch_shapes=[pltpu.VMEM((tm, D_MODEL), BF16)],
        compiler_params=_params(("arbitrary", "arbitrary")),
        name="inproj",
    )(x, gain, w_bf16)


def _rotary(x, cos, sin):
    x1 = x[:, :HALF]
    x2 = x[:, HALF:]
    return jnp.concatenate([x1 * cos - x2 * sin, x1 * sin + x2 * cos], axis=-1)


def _group_norm_gate(o, g):
    o = o * lax.rsqrt(jnp.mean(o * o, axis=-1, keepdims=True) + NORM_EPS)
    return (g * jax.nn.sigmoid(g)) * o


def _ret_chunk_kernel(q_ref, k_ref, v0_ref, v1_ref, g0_ref, g1_ref, cos_ref, sin_ref, mask_ref, qd_ref, kd_ref,
                      cd_ref, r_ref, s_ref):
    @pl.when(pl.program_id(1) == 0)
    def _():
        s_ref[...] = jnp.zeros_like(s_ref)

    cos = cos_ref[...]
    sin = sin_ref[...]
    v_refs = (v0_ref, v1_ref)
    g_refs = (g0_ref, g1_ref)
    for h in range(N_HEADS):
        half_cols = slice((h % 2) * DV, (h % 2 + 1) * DV)
        q = _rotary(q_ref[:, h * DK:(h + 1) * DK].astype(F32), cos, sin)
        k = _rotary(k_ref[:, h * DK:(h + 1) * DK].astype(F32), cos, sin) * (DK ** -0.5)
        v = v_refs[h // 2][:, half_cols]
        qd = jnp.concatenate([qd_ref[h], qd_ref[h]], axis=-1)
        kd = jnp.concatenate([kd_ref[h], kd_ref[h]], axis=-1)
        state = s_ref[0, h]
        scores = lax.dot_general(q.astype(BF16), k.astype(BF16), (((1,), (1,)), ((), ())),
                                 preferred_element_type=F32) * mask_ref[h]
        intra = jnp.dot(scores.astype(BF16), v, preferred_element_type=F32)
        cross = jnp.dot((q * qd).astype(BF16), state.astype(BF16), preferred_element_type=F32)
        kv = jnp.dot((k * kd).T.astype(BF16), v, preferred_element_type=F32)
        s_ref[0, h] = cd_ref[h] * state + kv
        g = g_refs[h // 2][:, half_cols].astype(F32)
        r_ref[:, h * DV:(h + 1) * DV] = _group_norm_gate(intra + cross, g).astype(r_ref.dtype)


def _retention_tables(chunk, seq_len, pos0):
    log_g = jnp.log1p(-jnp.exp2(-5.0 - jnp.arange(N_HEADS, dtype=F32)))
    pos = jnp.arange(chunk, dtype=F32)
    diff = pos[:, None] - pos[None, :]
    mask = jnp.where(diff >= 0, jnp.exp(jnp.maximum(diff, 0.0)[None] * log_g[:, None, None]), 0.0)
    q_decay = jnp.exp((pos + 1.0)[None, :] * log_g[:, None])[..., None]
    k_decay = jnp.exp((chunk - 1.0 - pos)[None, :] * log_g[:, None])[..., None]
    chunk_decay = jnp.exp(chunk * log_g)[:, None, None]
    qd = jnp.broadcast_to(q_decay, (N_HEADS, chunk, LANES))
    kd = jnp.broadcast_to(k_decay, (N_HEADS, chunk, LANES))
    cd = jnp.broadcast_to(chunk_decay, (N_HEADS, 1, DV))
    positions = pos0 + jnp.arange(seq_len, dtype=jnp.int32)
    inv = ROPE_BASE ** (-jnp.linspace(0.0, 1.0, HALF, dtype=F32))
    ang = positions.astype(F32)[:, None] * inv[None, :]
    return mask, qd, kd, cd, jnp.cos(ang), jnp.sin(ang)


def _retention_prompt(proj, batch, seq_len):
    c = RET_CHUNK
    nc = seq_len // c
    mask, qd, kd, cd, cos, sin = _retention_tables(c, seq_len, 0)
    row = lambda b, j: b * nc + j
    full3 = lambda b, j: (0, 0, 0)
    return pl.pallas_call(
        _ret_chunk_kernel,
        grid=(batch, nc),
        in_specs=[
            pl.BlockSpec((c, RET_QK), lambda b, j: (row(b, j), COL_Q // RET_QK)),
            pl.BlockSpec((c, RET_QK), lambda b, j: (row(b, j), COL_K // RET_QK)),
            pl.BlockSpec((c, RET_QK), lambda b, j: (row(b, j), COL_V // RET_QK)),
            pl.BlockSpec((c, RET_QK), lambda b, j: (row(b, j), COL_V // RET_QK + 1)),
            pl.BlockSpec((c, RET_QK), lambda b, j: (row(b, j), COL_G // RET_QK)),
            pl.BlockSpec((c, RET_QK), lambda b, j: (row(b, j), COL_G // RET_QK + 1)),
            pl.BlockSpec((c, HALF), lambda b, j: (j, 0)),
            pl.BlockSpec((c, HALF), lambda b, j: (j, 0)),
            pl.BlockSpec((N_HEADS, c, c), full3),
            pl.BlockSpec((N_HEADS, c, LANES), full3),
            pl.BlockSpec((N_HEADS, c, LANES), full3),
            pl.BlockSpec((N_HEADS, 1, DV), full3),
        ],
        out_specs=[
            pl.BlockSpec((c, RET_V), lambda b, j: (row(b, j), 0)),
            pl.BlockSpec((1, N_HEADS, DK, DV), lambda b, j: (b, 0, 0, 0)),
        ],
        out_shape=[
            jax.ShapeDtypeStruct((batch * seq_len, RET_V), BF16),
            jax.ShapeDtypeStruct((batch, N_HEADS, DK, DV), F32),
        ],
        compiler_params=_params(("arbitrary", "arbitrary")),
        name="retention_chunk",
    )(proj, proj, proj, proj, proj, proj, cos, sin, mask, qd, kd, cd)


def _ret_step_kernel(q_ref, k_ref, v_ref, g_ref, cos_ref, sin_ref, gam_ref, s_ref, r_ref, so_ref):
    cos = cos_ref[...]
    sin = sin_ref[...]
    gam = gam_ref[0]
    q = _rotary(q_ref[...], cos, sin)
    k = _rotary(k_ref[...], cos, sin) * (DK ** -0.5)
    v = v_ref[...]
    intra = jnp.sum(q * k, axis=-1, keepdims=True) * v
    q_t = (q * gam[:, :DK]).T
    k_t = k.T
    cross = []
    for s in range(SAMPLE_GROUP):
        state = s_ref[s, 0]
        cross.append(jnp.sum(q_t[:, s:s + 1] * state, axis=0, keepdims=True))
        so_ref[s, 0] = gam * state + k_t[:, s:s + 1] * v[s:s + 1, :]
    o = intra + jnp.concatenate(cross, axis=0)
    r_ref[...] = _group_norm_gate(o, g_ref[...])


def _retention_sample(proj, state, pos0):
    n = proj.shape[0]
    grp = SAMPLE_GROUP
    log_g = jnp.log1p(-jnp.exp2(-5.0 - jnp.arange(N_HEADS, dtype=F32)))
    gam = jnp.broadcast_to(jnp.exp(log_g)[:, None, None], (N_HEADS, 1, DV))
    _, _, _, _, cos, sin = _retention_tables(1, 1, pos0)
    return pl.pallas_call(
        _ret_step_kernel,
        grid=(n // grp, N_HEADS),
        in_specs=[
            pl.BlockSpec((grp, DK), lambda j, h: (j, COL_Q // DK + h)),
            pl.BlockSpec((grp, DK), lambda j, h: (j, COL_K // DK + h)),
            pl.BlockSpec((grp, DV), lambda j, h: (j, COL_V // DV + h)),
            pl.BlockSpec((grp, DV), lambda j, h: (j, COL_G // DV + h)),
            pl.BlockSpec((1, HALF), lambda j, h: (0, 0)),
            pl.BlockSpec((1, HALF), lambda j, h: (0, 0)),
            pl.BlockSpec((1, 1, DV), lambda j, h: (h, 0, 0)),
            pl.BlockSpec((grp, 1, DK, DV), lambda j, h: (j, h, 0, 0)),
        ],
        out_specs=[
            pl.BlockSpec((grp, DV), lambda j, h: (j, h)),
            pl.BlockSpec((grp, 1, DK, DV), lambda j, h: (j, h, 0, 0)),
        ],
        out_shape=[
            jax.ShapeDtypeStruct((n, RET_V), F32),
            jax.ShapeDtypeStruct(state.shape, F32),
        ],
        compiler_params=_params(("arbitrary", "arbitrary")),
        name="retention_step",
    )(proj, proj, proj, proj, cos, sin, gam, state)


def _merge_kernel(per_token_hist, chained, tiles_per_seq, region_rows, *refs):
    refs = list(refs)
    cb_ref, cc_ref, ch_ref, ga_ref, gb_ref, r_ref, x_ref = refs[:7]
    del refs[:7]
    if per_token_hist:
        h0_ref, h1_ref = refs[:2]
        del refs[:2]
    cw_ref, wa_ref, wb_ref, wo_ref, n2_ref, wr_ref, br_ref, cnt_in = refs[:8]
    del refs[:8]
    if chained:
        del refs[:1]
    h_out, route_out, gate_out, u_out, cnt_out, xs_out = refs[:6]
    del refs[:6]
    if not per_token_hist:
        hist_ref = refs.pop(0)
    cnt_ref, hn_scr, dvm, dsm, sem = refs

    i = pl.program_id(0)
    tm = x_ref.shape[0]

    def wait_scatters():
        for _ in range(TOP_K):
            pltpu.make_async_copy(hn_scr, xs_out.at[pl.ds(0, tm * SUBLANES), :], sem.at[0]).wait()

    u = cc_ref[...].astype(F32) * ch_ref[...].astype(F32)
    if per_token_hist:
        u2 = h0_ref[...]
        u1 = h1_ref[...]
        u_out[...] = u
    else:
        @pl.when(i % tiles_per_seq == 0)
        def _():
            hist_ref[...] = jnp.zeros_like(hist_ref)

        hm2 = hist_ref[0:1, :]
        hm1 = hist_ref[1:2, :]
        row = lax.broadcasted_iota(I32, u.shape, 0)
        u1 = jnp.where(row == 0, hm1, pltpu.roll(u, 1, axis=0))
        u2 = jnp.where(row == 0, hm2, jnp.where(row == 1, hm1, pltpu.roll(u, 2, axis=0)))
        last2 = u[tm - (CONV_W - 1):, :]
        hist_ref[...] = last2
        u_out[0] = last2
    conv = cw_ref[0:1, :] * u2 + cw_ref[1:2, :] * u1 + cw_ref[2:3, :] * u
    z = (cb_ref[...].astype(F32) * conv).astype(BF16)
    o_a = jnp.dot(z, wa_ref[...], preferred_element_type=F32)
    o_b = jnp.dot(r_ref[...].astype(BF16), wb_ref[...], preferred_element_type=F32)
    merged = (jax.nn.sigmoid(ga_ref[...].astype(F32)) * o_a
              + jax.nn.sigmoid(gb_ref[...].astype(F32)) * o_b)
    m = jnp.dot(merged.astype(BF16), wo_ref[...], preferred_element_type=F32)
    h = x_ref[...] + m
    h_out[...] = h
    hn = (h * lax.rsqrt(jnp.mean(h * h, axis=-1, keepdims=True) + NORM_EPS)) * n2_ref[...]
    hn_hi = hn.astype(BF16)
    hn_lo = (hn - hn_hi.astype(F32)).astype(BF16)
    nt = (((1,), (1,)), ((), ()))
    part = lax.dot_general(wr_ref[...], hn_hi, nt, preferred_element_type=F32)
    logits = (part[:N_EXPERTS] + part[N_EXPERTS:] + br_ref[...]
              + lax.dot_general(wr_ref[:N_EXPERTS, :], hn_lo, nt, preferred_element_type=F32))
    expert = lax.broadcasted_iota(I32, logits.shape, 0)
    vals, idxs = [], []
    for _ in range(TOP_K):
        top = jnp.max(logits, axis=0, keepdims=True)
        arg = jnp.min(jnp.where(logits == top, expert, N_EXPERTS), axis=0, keepdims=True)
        vals.append(top)
        idxs.append(arg)
        logits = jnp.where(expert == arg, -jnp.inf, logits)
    exps = [jnp.exp(val - vals[0]) for val in vals]
    denom = exps[0] + exps[1] + exps[2] + exps[3]

    @pl.when(i == 0)
    def _():
        cnt_ref[...] = cnt_in[...]

    picks = [expert == idx for idx in idxs]
    chosen = jnp.zeros(logits.shape, F32)
    for pick in picks:
        chosen = jnp.where(pick, 1.0, chosen)
    before = (lax.broadcasted_iota(I32, (tm, tm), 0) < lax.broadcasted_iota(I32, (tm, tm), 1))
    prefix = jnp.dot(chosen.astype(BF16), jnp.where(before, 1.0, 0.0).astype(BF16),
                     preferred_element_type=F32)
    counts = cnt_ref[...]
    base = jnp.concatenate([counts] * (tm // LANES), axis=1) + prefix
    ranks = [jnp.sum(jnp.where(pick, base, 0.0), axis=0, keepdims=True).astype(I32) for pick in picks]
    counts = counts + jnp.sum(chosen, axis=1, keepdims=True)
    cnt_ref[...] = counts
    cnt_out[...] = counts

    slot_row = lax.broadcasted_iota(I32, (SUBLANES, tm), 0)
    route = jnp.zeros((SUBLANES, tm), I32)
    gates = jnp.zeros((SUBLANES, tm), F32)
    dests = jnp.zeros((SUBLANES, tm), I32)
    for kk in range(TOP_K):
        route = jnp.where(slot_row == kk, idxs[kk], route)
        route = jnp.where(slot_row == TOP_K + kk, ranks[kk], route)
        gates = jnp.where(slot_row == kk, exps[kk] / denom, gates)
        dests = jnp.where(slot_row == kk, (idxs[kk] * region_rows + ranks[kk]) * SUBLANES, dests)
    route_out[...] = route
    gate_out[...] = gates

    dvm[...] = dests
    to_smem = pltpu.make_async_copy(dvm, dsm, sem.at[1])
    to_smem.start()

    @pl.when(i > 0)
    def _():
        wait_scatters()

    for c in range(D_MODEL // LANES):
        hn_scr[pl.ds(c, tm, stride=SUBLANES), :] = hn[:, c * LANES:(c + 1) * LANES]
    to_smem.wait()

    def issue(g, carry):
        for uu in range(ISSUE_UNROLL):
            t = g * ISSUE_UNROLL + uu
            src = hn_scr.at[pl.ds(pl.multiple_of(t * SUBLANES, SUBLANES), SUBLANES), :]
            for kk in range(TOP_K):
                dst = xs_out.at[pl.ds(pl.multiple_of(dsm[kk, t], SUBLANES), SUBLANES), :]
                pltpu.make_async_copy(src, dst, sem.at[0]).start()
        return carry

    lax.fori_loop(0, tm // ISSUE_UNROLL, issue, 0)

    @pl.when(i == pl.num_programs(0) - 1)
    def _():
        wait_scatters()


def _merge(proj, r, x, hist, conv_w, wa, wb, wo, norm2, w_router, b_router, counts_in, xs_prev, region_rows,
           tm, tiles_per_seq):
    t = x.shape[0]
    per_token = hist is not None
    chained = xs_prev is not None
    col = lambda c: (lambda i: (i, c // D_MODEL))
    const2 = lambda i: (0, 0)
    tile = pl.BlockSpec((tm, D_MODEL), lambda i: (i, 0))
    in_specs = [
        pl.BlockSpec((tm, D_MODEL), col(COL_CB)),
        pl.BlockSpec((tm, D_MODEL), col(COL_CC)),
        pl.BlockSpec((tm, D_MODEL), col(COL_CH)),
        pl.BlockSpec((tm, D_MODEL), col(COL_GA)),
        pl.BlockSpec((tm, D_MODEL), col(COL_GB)),
        pl.BlockSpec((tm, RET_V), lambda i: (i, 0)),
        tile,
    ]
    args = [proj, proj, proj, proj, proj, r, x]
    if per_token:
        in_specs += [tile, tile]
        args += list(hist)
    in_specs += [
        pl.BlockSpec((CONV_W, D_MODEL), const2),
        pl.BlockSpec((D_MODEL, D_MODEL), const2),
        pl.BlockSpec((RET_V, D_MODEL), const2),
        pl.BlockSpec((D_MODEL, D_MODEL), const2),
        pl.BlockSpec((1, D_MODEL), const2),
        pl.BlockSpec((2 * N_EXPERTS, D_MODEL), const2),
        pl.BlockSpec((N_EXPERTS, tm), const2),
        pl.BlockSpec((N_EXPERTS, LANES), const2),
    ]
    args += [conv_w, wa, wb, wo, norm2, w_router, b_router, counts_in]
    aliases = {}
    if chained:
        aliases = {len(args): 5}
        in_specs.append(pl.BlockSpec(memory_space=pl.ANY))
        args.append(xs_prev)
    lanes_tile = pl.BlockSpec((SUBLANES, tm), lambda i: (0, i))
    if per_token:
        u_spec = tile
        u_shape = jax.ShapeDtypeStruct((t, D_MODEL), F32)
        scratch = []
    else:
        n_seq = t // (tm * tiles_per_seq)
        u_spec = pl.BlockSpec((1, CONV_W - 1, D_MODEL), lambda i: (i // tiles_per_seq, 0, 0))
        u_shape = jax.ShapeDtypeStruct((n_seq, CONV_W - 1, D_MODEL), F32)
        scratch = [pltpu.VMEM((CONV_W - 1, D_MODEL), F32)]
    out_specs = [tile, lanes_tile, lanes_tile, u_spec, pl.BlockSpec((N_EXPERTS, LANES), const2),
                 pl.BlockSpec(memory_space=pl.ANY)]
    out_shape = [
        jax.ShapeDtypeStruct((t, D_MODEL), F32),
        jax.ShapeDtypeStruct((SUBLANES, t), I32),
        jax.ShapeDtypeStruct((SUBLANES, t), F32),
        u_shape,
        jax.ShapeDtypeStruct((N_EXPERTS, LANES), F32),
        jax.ShapeDtypeStruct((N_EXPERTS * region_rows * SUBLANES, LANES), F32),
    ]
    scratch += [
        pltpu.VMEM((N_EXPERTS, LANES), F32),
        pltpu.VMEM((tm * SUBLANES, LANES), F32),
        pltpu.VMEM((SUBLANES, tm), I32),
        pltpu.SMEM((SUBLANES, tm), I32),
        pltpu.SemaphoreType.DMA((2,)),
    ]
    return pl.pallas_call(
        functools.partial(_merge_kernel, per_token, chained, tiles_per_seq, region_rows),
        grid=(t // tm,),
        in_specs=in_specs,
        out_specs=out_specs,
        out_shape=out_shape,
        scratch_shapes=scratch,
        input_output_aliases=aliases,
        compiler_params=_params(("arbitrary",)),
        name="merge_sample" if per_token else "merge_prompt",
    )(*args)


def _rows_from_tiles(ref, n_rows):
    return jnp.concatenate([ref[pl.ds(c, n_rows, stride=SUBLANES), :] for c in range(D_MODEL // LANES)], axis=1)


def _rows_to_tiles(ref, rows):
    for c in range(D_MODEL // LANES):
        ref[pl.ds(c, rows.shape[0], stride=SUBLANES), :] = rows[:, c * LANES:(c + 1) * LANES]


def _moe_kernel(be_ref, bx_ref, vr_ref, nused_ref, next_ref, x_ref, w1_hbm, b1_ref, w2_hbm, b2_ref, o_ref,
                w1f, w2f, w1b, w2b, sem):
    i = pl.program_id(0)
    n_used = nused_ref[0]

    def weight_copies(e):
        return (pltpu.make_async_copy(w1_hbm.at[e], w1f, sem.at[0]),
                pltpu.make_async_copy(w2_hbm.at[e], w2f, sem.at[1]))

    @pl.when(i < n_used)
    def _():
        e = be_ref[i]
        prev = be_ref[jnp.maximum(i - 1, 0)]

        @pl.when(i == 0)
        def _():
            for cp in weight_copies(e):
                cp.start()

        @pl.when((i == 0) | (e != prev))
        def _():
            for cp in weight_copies(e):
                cp.wait()
            w1b[...] = w1f[...].astype(BF16)
            w2b[...] = w2f[...].astype(BF16)
            nxt = next_ref[e]

            @pl.when(nxt >= 0)
            def _():
                for cp in weight_copies(nxt):
                    cp.start()

        live = lax.broadcasted_iota(I32, (MOE_BM, 1), 0) < vr_ref[i]
        x = jnp.where(live, _rows_from_tiles(x_ref, MOE_BM), 0.0).astype(BF16)
        h1 = jnp.dot(x, w1b[...], preferred_element_type=F32) + b1_ref[0]
        gate = jnp.minimum(h1[:, :D_FF], SWIGLU_LIMIT)
        up = jnp.clip(h1[:, D_FF:], -SWIGLU_LIMIT, SWIGLU_LIMIT)
        glu = gate * jax.nn.sigmoid(SWIGLU_ALPHA * gate)
        act = ((up + 1.0) * glu).astype(BF16)
        _rows_to_tiles(o_ref, jnp.dot(act, w2b[...], preferred_element_type=F32) + b2_ref[0])

    @pl.when(i >= n_used)
    def _():
        o_ref[...] = jnp.zeros_like(o_ref)


def _moe_blocks(xs, block_expert, block_xs, valid_rows, n_used, next_expert, w1, b1, w2, b2):
    n_blocks = block_expert.shape[0]
    grid_spec = pltpu.PrefetchScalarGridSpec(
        num_scalar_prefetch=5,
        grid=(n_blocks,),
        in_specs=[
            pl.BlockSpec((MOE_BM * SUBLANES, LANES), lambda i, be, bx, vr, nu, nx: (bx[i], 0)),
            pl.BlockSpec(memory_space=pl.ANY),
            pl.BlockSpec((1, 1, 2 * D_FF), lambda i, be, bx, vr, nu, nx: (be[i], 0, 0)),
            pl.BlockSpec(memory_space=pl.ANY),
            pl.BlockSpec((1, 1, D_MODEL), lambda i, be, bx, vr, nu, nx: (be[i], 0, 0)),
        ],
        out_specs=pl.BlockSpec((MOE_BM * SUBLANES, LANES), lambda i, be, bx, vr, nu, nx: (i, 0)),
        scratch_shapes=[
            pltpu.VMEM((D_MODEL, 2 * D_FF), F32),
            pltpu.VMEM((D_FF, D_MODEL), F32),
            pltpu.VMEM((D_MODEL, 2 * D_FF), BF16),
            pltpu.VMEM((D_FF, D_MODEL), BF16),
            pltpu.SemaphoreType.DMA((2,)),
        ],
    )
    return pl.pallas_call(
        _moe_kernel,
        grid_spec=grid_spec,
        out_shape=jax.ShapeDtypeStruct((n_blocks * MOE_BM * SUBLANES, LANES), F32),
        compiler_params=_params(("arbitrary",)),
        name="moe_experts",
    )(block_expert, block_xs, valid_rows, n_used, next_expert, xs, w1, b1[:, None, :], w2, b2[:, None, :])


def _block_plan(counts, n_tokens, region_rows):
    n_blocks = -(-n_tokens * TOP_K // MOE_BM) + N_EXPERTS
    blocks_e = (counts + MOE_BM - 1) // MOE_BM
    blk_end = jnp.cumsum(blocks_e)
    blk_start = blk_end - blocks_e
    n_used = blk_end[-1:]
    blk = jnp.maximum(jnp.minimum(jnp.arange(n_blocks, dtype=I32), n_used - 1), 0)
    expert = jnp.minimum(jnp.sum((blk_end[None, :] <= blk[:, None]).astype(I32), axis=1), N_EXPERTS - 1)
    within = blk - blk_start[expert]
    block_xs = expert * (region_rows // MOE_BM) + within
    valid = jnp.clip(counts[expert] - within * MOE_BM, 0, MOE_BM)
    ids = jnp.arange(N_EXPERTS, dtype=I32)
    later = jnp.where((ids[None, :] > ids[:, None]) & (counts[None, :] > 0), ids[None, :], N_EXPERTS)
    next_expert = jnp.min(later, axis=1)
    next_expert = jnp.where(next_expert == N_EXPERTS, -1, next_expert)
    return (expert, block_xs.astype(I32), valid.astype(I32), n_used.astype(I32), next_expert.astype(I32),
            (blk_start * MOE_BM).astype(I32))


def _combine_kernel(n_main, n_tokens, dest_ref, hp_ref, hs_ref, gp_ref, gs_ref, nf_ref, yb_hbm, op_ref, os_ref,
                    ybuf, sem):
    i = pl.program_id(0)
    slot = i % 2

    def issue(tile, sl):
        def body(g, carry):
            for uu in range(ISSUE_UNROLL):
                t = g * ISSUE_UNROLL + uu
                for kk in range(TOP_K):
                    row = pl.multiple_of(dest_ref[kk * n_tokens + tile * COMB_TT + t], SUBLANES)
                    pltpu.make_async_copy(
                        yb_hbm.at[pl.ds(row, SUBLANES), :],
                        ybuf.at[sl, kk, pl.ds(pl.multiple_of(t * SUBLANES, SUBLANES), SUBLANES), :],
                        sem.at[sl]).start()
            return carry

        lax.fori_loop(0, COMB_TT // ISSUE_UNROLL, body, 0)

    @pl.when(i == 0)
    def _():
        issue(0, 0)

    @pl.when(i + 1 < pl.num_programs(0))
    def _():
        issue(i + 1, 1 - slot)

    for kk in range(TOP_K):
        pltpu.make_async_copy(yb_hbm.at[pl.ds(0, COMB_TT * SUBLANES), :], ybuf.at[slot, kk], sem.at[slot]).wait()

    def finish(h_ref, gate_ref, o_ref):
        pad = jnp.zeros((COMB_TT - SUBLANES, COMB_TT), F32)
        gates = jnp.concatenate([gate_ref[...], pad], axis=0).T
        y = gates[:, 0:1] * _rows_from_tiles(ybuf.at[slot, 0], COMB_TT)
        for kk in range(1, TOP_K):
            y = y + gates[:, kk:kk + 1] * _rows_from_tiles(ybuf.at[slot, kk], COMB_TT)
        h = h_ref[...] + y
        o_ref[...] = (h * lax.rsqrt(jnp.mean(h * h, axis=-1, keepdims=True) + NORM_EPS)) * nf_ref[...]

    @pl.when(i < n_main)
    def _():
        finish(hp_ref, gp_ref, op_ref)

    @pl.when(i >= n_main)
    def _():
        finish(hs_ref, gs_ref, os_ref)


def _combine(dest, h_p, h_s, gate_p, gate_s, norm_f, yb):
    n_main = h_p.shape[0] // COMB_TT
    n_tail = h_s.shape[0] // COMB_TT
    main = lambda i, d: (jnp.minimum(i, n_main - 1), 0)
    tail = lambda i, d: (jnp.maximum(i - n_main, 0), 0)
    grid_spec = pltpu.PrefetchScalarGridSpec(
        num_scalar_prefetch=1,
        grid=(n_main + n_tail,),
        in_specs=[
            pl.BlockSpec((COMB_TT, D_MODEL), main),
            pl.BlockSpec((COMB_TT, D_MODEL), tail),
            pl.BlockSpec((SUBLANES, COMB_TT), lambda i, d: (0, jnp.minimum(i, n_main - 1))),
            pl.BlockSpec((SUBLANES, COMB_TT), lambda i, d: (0, jnp.maximum(i - n_main, 0))),
            pl.BlockSpec((1, D_MODEL), lambda i, d: (0, 0)),
            pl.BlockSpec(memory_space=pl.ANY),
        ],
        out_specs=[
            pl.BlockSpec((COMB_TT, D_MODEL), main),
            pl.BlockSpec((COMB_TT, D_MODEL), tail),
        ],
        scratch_shapes=[
            pltpu.VMEM((2, TOP_K, COMB_TT * SUBLANES, LANES), F32),
            pltpu.SemaphoreType.DMA((2,)),
        ],
    )
    return pl.pallas_call(
        functools.partial(_combine_kernel, n_main, h_p.shape[0] + h_s.shape[0]),
        grid_spec=grid_spec,
        out_shape=[jax.ShapeDtypeStruct(h_p.shape, F32), jax.ShapeDtypeStruct(h_s.shape, F32)],
        compiler_params=_params(("arbitrary",)),
        name="combine",
    )(dest, h_p, h_s, gate_p, gate_s, norm_f, yb)


def kernel(x_prompt, x_sample, state_conv, state_ret, norm1, w_in, conv_w, w_a, w_b, w_o, norm2, w_router,
           b_router, w_e1, b_e1, w_e2, b_e2, norm_f):
    batch, seq_len, _ = x_prompt.shape
    n_dec, dec_seq, _ = x_sample.shape
    depth = norm1.shape[0]
    assert dec_seq == 1 and depth == 1, "single-token decode step of a one-layer trunk"
    past_len = 16384
    t_p = batch * seq_len
    n_tokens = t_p + n_dec
    region_rows = _expert_region_rows(n_tokens)
    lyr = 0
    xp = x_prompt.reshape(t_p, D_MODEL)
    xs = x_sample.reshape(n_dec, D_MODEL)
    g1 = norm1[lyr][None, :]
    w_in_b = w_in[lyr].astype(BF16)
    wa, wb, wo = w_a[lyr].astype(BF16), w_b[lyr].astype(BF16), w_o[lyr].astype(BF16)
    g2 = norm2[lyr][None, :]
    wr_t = w_router[lyr].T
    wr_hi = wr_t.astype(BF16)
    wr_split = jnp.concatenate([wr_hi, (wr_t - wr_hi.astype(F32)).astype(BF16)], axis=0)
    br_col = b_router[lyr][:, None]

    proj_p = _inproj(xp, g1, w_in_b, 1024, BF16)
    r_p, ret_p = _retention_prompt(proj_p, batch, seq_len)
    h_p, route_p, gate_p, conv_p, counts, dispatch = _merge(
        proj_p, r_p, xp, None, conv_w[lyr], wa, wb, wo, g2, wr_split, jnp.broadcast_to(br_col, (N_EXPERTS, MERGE_TM)),
        jnp.zeros((N_EXPERTS, LANES), F32), None, region_rows, MERGE_TM, seq_len // MERGE_TM)
    proj_s = _inproj(xs, g1, w_in_b, n_dec, F32)
    r_s, ret_s = _retention_sample(proj_s, state_ret[lyr], past_len)
    hist = (state_conv[lyr][:, 0, :], state_conv[lyr][:, 1, :])
    h_s, route_s, gate_s, u_s, counts, dispatch = _merge(
        proj_s, r_s, xs, hist, conv_w[lyr], wa, wb, wo, g2, wr_split, jnp.broadcast_to(br_col, (N_EXPERTS, n_dec)),
        counts, dispatch, region_rows, n_dec, 1)
    conv_s = jnp.stack([state_conv[lyr][:, 1, :], u_s], axis=1)

    counts_i = counts[:, 0].astype(I32)
    block_expert, block_xs, valid_rows, n_used, next_expert, row_start = _block_plan(
        counts_i, n_tokens, region_rows)
    yb = _moe_blocks(dispatch, block_expert, block_xs, valid_rows, n_used, next_expert,
                     w_e1[lyr], b_e1[lyr], w_e2[lyr], b_e2[lyr])
    route = jnp.concatenate([route_p, route_s], axis=1)
    expert_ids = jnp.arange(N_EXPERTS, dtype=I32)[:, None, None]
    first_row = jnp.sum(jnp.where(route[None, :TOP_K] == expert_ids, row_start[:, None, None], 0), axis=0)
    dest = ((first_row + route[TOP_K:]) * SUBLANES).reshape(TOP_K * n_tokens)
    y_p, y_s = _combine(dest, h_p, h_s, gate_p, gate_s, norm_f[None, :], yb)

    y_prompt = y_p.reshape(batch, seq_len, D_MODEL)
    y_sample = y_s.reshape(n_dec, 1, D_MODEL)
    return (y_prompt, y_sample, conv_p[None], ret_p[None], conv_s[None], ret_s[None])
```

```python
import functools

import jax
import jax.numpy as jnp
from jax import lax
from jax.experimental import pallas as pl
from jax.experimental.pallas import tpu as pltpu

F32 = jnp.float32
BF16 = jnp.bfloat16
I32 = jnp.int32

D_MODEL = 1024
CONV_W = 3
N_HEADS = 4
DK = 256
DV = 512
HALF = DK // 2
RET_QK = N_HEADS * DK
RET_V = N_HEADS * DV
ROPE_BASE = 10000.0
N_EXPERTS = 32
TOP_K = 4
D_FF = 1024
SWIGLU_ALPHA = 1.702
SWIGLU_LIMIT = 7.0
NORM_EPS = 1e-5
N_IN_COLS = 3 * D_MODEL + 2 * RET_QK + 2 * RET_V + 2 * D_MODEL
COL_CB, COL_CC, COL_CH = 0, D_MODEL, 2 * D_MODEL
COL_Q = 3 * D_MODEL
COL_K = COL_Q + RET_QK
COL_V = COL_K + RET_QK
COL_G = COL_V + RET_V
COL_GA = COL_G + RET_V
COL_GB = COL_GA + D_MODEL

RET_CHUNK = 256
MERGE_TM = 256
INPROJ_TN = 1024
MOE_BM = 256
COMB_TT = 128
SAMPLE_GROUP = 8
LANES = 128
SUBLANES = 8
ISSUE_UNROLL = 8
VMEM_LIMIT = 56 * 1024 * 1024


def _params(sem):
    return pltpu.CompilerParams(dimension_semantics=sem, vmem_limit_bytes=VMEM_LIMIT)


def _expert_region_rows(n_tokens):
    return -(-n_tokens // MOE_BM) * MOE_BM


def _inproj_kernel(x_ref, g_ref, w_ref, o_ref, xn_ref):
    @pl.when(pl.program_id(1) == 0)
    def _():
        x = x_ref[...]
        ms = jnp.mean(x * x, axis=-1, keepdims=True)
        xn_ref[...] = ((x * lax.rsqrt(ms + NORM_EPS)) * g_ref[...]).astype(BF16)

    o_ref[...] = jnp.dot(xn_ref[...], w_ref[...], preferred_element_type=F32).astype(o_ref.dtype)


def _inproj(x, gain, w_bf16, tm, out_dtype):
    m = x.shape[0]
    n = w_bf16.shape[1]
    return pl.pallas_call(
        _inproj_kernel,
        grid=(m // tm, n // INPROJ_TN),
        in_specs=[
            pl.BlockSpec((tm, D_MODEL), lambda i, j: (i, 0)),
            pl.BlockSpec((1, D_MODEL), lambda i, j: (0, 0)),
            pl.BlockSpec((D_MODEL, INPROJ_TN), lambda i, j: (0, j)),
        ],
        out_specs=pl.BlockSpec((tm, INPROJ_TN), lambda i, j: (i, j)),
        out_shape=jax.ShapeDtypeStruct((m, n), out_dtype),
        scratch_shapes=[pltpu.VMEM((tm, D_MODEL), BF16)],
        compiler_params=_params(("arbitrary", "arbitrary")),
        name="inproj",
    )(x, gain, w_bf16)


def _rotary(x, cos, sin):
    x1 = x[:, :HALF]
    x2 = x[:, HALF:]
    return jnp.concatenate([x1 * cos - x2 * sin, x1 * sin + x2 * cos], axis=-1)


def _group_norm_gate(o, g):
    o = o * lax.rsqrt(jnp.mean(o * o, axis=-1, keepdims=True) + NORM_EPS)
    return (g * jax.nn.sigmoid(g)) * o


def _ret_chunk_kernel(q_ref, k_ref, v0_ref, v1_ref, g0_ref, g1_ref, cos_ref, sin_ref, mask_ref, qd_ref, kd_ref,
                      cd_ref, r_ref, s_ref):
    @pl.when(pl.program_id(1) == 0)
    def _():
        s_ref[...] = jnp.zeros_like(s_ref)

    cos = cos_ref[...]
    sin = sin_ref[...]
    v_refs = (v0_ref, v1_ref)
    g_refs = (g0_ref, g1_ref)
    for h in range(N_HEADS):
        half_cols = slice((h % 2) * DV, (h % 2 + 1) * DV)
        q = _rotary(q_ref[:, h * DK:(h + 1) * DK].astype(F32), cos, sin)
        k = _rotary(k_ref[:, h * DK:(h + 1) * DK].astype(F32), cos, sin) * (DK ** -0.5)
        v = v_refs[h // 2][:, half_cols]
        qd = jnp.concatenate([qd_ref[h], qd_ref[h]], axis=-1)
        kd = jnp.concatenate([kd_ref[h], kd_ref[h]], axis=-1)
        state = s_ref[0, h]
        scores = lax.dot_general(q.astype(BF16), k.astype(BF16), (((1,), (1,)), ((), ())),
                                 preferred_element_type=F32) * mask_ref[h]
        intra = jnp.dot(scores.astype(BF16), v, preferred_element_type=F32)
        cross = jnp.dot((q * qd).astype(BF16), state.astype(BF16), preferred_element_type=F32)
        kv = jnp.dot((k * kd).T.astype(BF16), v, preferred_element_type=F32)
        s_ref[0, h] = cd_ref[h] * state + kv
        g = g_refs[h // 2][:, half_cols].astype(F32)
        r_ref[:, h * DV:(h + 1) * DV] = _group_norm_gate(intra + cross, g).astype(r_ref.dtype)


def _retention_tables(chunk, seq_len, pos0):
    log_g = jnp.log1p(-jnp.exp2(-5.0 - jnp.arange(N_HEADS, dtype=F32)))
    pos = jnp.arange(chunk, dtype=F32)
    diff = pos[:, None] - pos[None, :]
    mask = jnp.where(diff >= 0, jnp.exp(jnp.maximum(diff, 0.0)[None] * log_g[:, None, None]), 0.0)
    q_decay = jnp.exp((pos + 1.0)[None, :] * log_g[:, None])[..., None]
    k_decay = jnp.exp((chunk - 1.0 - pos)[None, :] * log_g[:, None])[..., None]
    chunk_decay = jnp.exp(chunk * log_g)[:, None, None]
    qd = jnp.broadcast_to(q_decay, (N_HEADS, chunk, LANES))
    kd = jnp.broadcast_to(k_decay, (N_HEADS, chunk, LANES))
    cd = jnp.broadcast_to(chunk_decay, (N_HEADS, 1, DV))
    positions = pos0 + jnp.arange(seq_len, dtype=jnp.int32)
    inv = ROPE_BASE ** (-jnp.linspace(0.0, 1.0, HALF, dtype=F32))
    ang = positions.astype(F32)[:, None] * inv[None, :]
    return mask, qd, kd, cd, jnp.cos(ang), jnp.sin(ang)


def _retention_prompt(proj, batch, seq_len):
    c = RET_CHUNK
    nc = seq_len // c
    mask, qd, kd, cd, cos, sin = _retention_tables(c, seq_len, 0)
    row = lambda b, j: b * nc + j
    full3 = lambda b, j: (0, 0, 0)
    return pl.pallas_call(
        _ret_chunk_kernel,
        grid=(batch, nc),
        in_specs=[
            pl.BlockSpec((c, RET_QK), lambda b, j: (row(b, j), COL_Q // RET_QK)),
            pl.BlockSpec((c, RET_QK), lambda b, j: (row(b, j), COL_K // RET_QK)),
            pl.BlockSpec((c, RET_QK), lambda b, j: (row(b, j), COL_V // RET_QK)),
            pl.BlockSpec((c, RET_QK), lambda b, j: (row(b, j), COL_V // RET_QK + 1)),
            pl.BlockSpec((c, RET_QK), lambda b, j: (row(b, j), COL_G // RET_QK)),
            pl.BlockSpec((c, RET_QK), lambda b, j: (row(b, j), COL_G // RET_QK + 1)),
            pl.BlockSpec((c, HALF), lambda b, j: (j, 0)),
            pl.BlockSpec((c, HALF), lambda b, j: (j, 0)),
            pl.BlockSpec((N_HEADS, c, c), full3),
            pl.BlockSpec((N_HEADS, c, LANES), full3),
            pl.BlockSpec((N_HEADS, c, LANES), full3),
            pl.BlockSpec((N_HEADS, 1, DV), full3),
        ],
        out_specs=[
            pl.BlockSpec((c, RET_V), lambda b, j: (row(b, j), 0)),
            pl.BlockSpec((1, N_HEADS, DK, DV), lambda b, j: (b, 0, 0, 0)),
        ],
        out_shape=[
            jax.ShapeDtypeStruct((batch * seq_len, RET_V), BF16),
            jax.ShapeDtypeStruct((batch, N_HEADS, DK, DV), F32),
        ],
        compiler_params=_params(("arbitrary", "arbitrary")),
        name="retention_chunk",
    )(proj, proj, proj, proj, proj, proj, cos, sin, mask, qd, kd, cd)


def _ret_step_kernel(q_ref, k_ref, v_ref, g_ref, cos_ref, sin_ref, gam_ref, s_ref, r_ref, so_ref):
    cos = cos_ref[...]
    sin = sin_ref[...]
    gam = gam_ref[0]
    q = _rotary(q_ref[...], cos, sin)
    k = _rotary(k_ref[...], cos, sin) * (DK ** -0.5)
    v = v_ref[...]
    intra = jnp.sum(q * k, axis=-1, keepdims=True) * v
    q_t = (q * gam[:, :DK]).T
    k_t = k.T
    cross = []
    for s in range(SAMPLE_GROUP):
        state = s_ref[s, 0]
        cross.append(jnp.sum(q_t[:, s:s + 1] * state, axis=0, keepdims=True))
        so_ref[s, 0] = gam * state + k_t[:, s:s + 1] * v[s:s + 1, :]
    o = intra + jnp.concatenate(cross, axis=0)
    r_ref[...] = _group_norm_gate(o, g_ref[...])


def _retention_sample(proj, state, pos0):
    n = proj.shape[0]
    grp = SAMPLE_GROUP
    log_g = jnp.log1p(-jnp.exp2(-5.0 - jnp.arange(N_HEADS, dtype=F32)))
    gam = jnp.broadcast_to(jnp.exp(log_g)[:, None, None], (N_HEADS, 1, DV))
    _, _, _, _, cos, sin = _retention_tables(1, 1, pos0)
    return pl.pallas_call(
        _ret_step_kernel,
        grid=(n // grp, N_HEADS),
        in_specs=[
            pl.BlockSpec((grp, DK), lambda j, h: (j, COL_Q // DK + h)),
            pl.BlockSpec((grp, DK), lambda j, h: (j, COL_K // DK + h)),
            pl.BlockSpec((grp, DV), lambda j, h: (j, COL_V // DV + h)),
            pl.BlockSpec((grp, DV), lambda j, h: (j, COL_G // DV + h)),
            pl.BlockSpec((1, HALF), lambda j, h: (0, 0)),
            pl.BlockSpec((1, HALF), lambda j, h: (0, 0)),
            pl.BlockSpec((1, 1, DV), lambda j, h: (h, 0, 0)),
            pl.BlockSpec((grp, 1, DK, DV), lambda j, h: (j, h, 0, 0)),
        ],
        out_specs=[
            pl.BlockSpec((grp, DV), lambda j, h: (j, h)),
            pl.BlockSpec((grp, 1, DK, DV), lambda j, h: (j, h, 0, 0)),
        ],
        out_shape=[
            jax.ShapeDtypeStruct((n, RET_V), F32),
            jax.ShapeDtypeStruct(state.shape, F32),
        ],
        compiler_params=_params(("arbitrary", "arbitrary")),
        name="retention_step",
    )(proj, proj, proj, proj, cos, sin, gam, state)


def _merge_kernel(per_token_hist, chained, tiles_per_seq, region_rows, *refs):
    refs = list(refs)
    cb_ref, cc_ref, ch_ref, ga_ref, gb_ref, r_ref, x_ref = refs[:7]
    del refs[:7]
    if per_token_hist:
        h0_ref, h1_ref = refs[:2]
        del refs[:2]
    cw_ref, wa_ref, wb_ref, wo_ref, n2_ref, wr_ref, br_ref, cnt_in = refs[:8]
    del refs[:8]
    if chained:
        del refs[:1]
    h_out, route_out, gate_out, u_out, cnt_out, xs_out = refs[:6]
    del refs[:6]
    if not per_token_hist:
        hist_ref = refs.pop(0)
    cnt_ref, hn_st, dvm, dsm, sem = refs

    i = pl.program_id(0)
    last = pl.num_programs(0) - 1
    tm = x_ref.shape[0]
    cur = i % 2
    prev = 1 - cur

    def wait_scatters(p):
        for _ in range(TOP_K):
            pltpu.make_async_copy(hn_st.at[p], xs_out.at[pl.ds(0, tm * SUBLANES), :], sem.at[p]).wait()

    def dests_to_smem(p):
        return pltpu.make_async_copy(dvm, dsm.at[p], sem.at[2])

    def row_scatter(p, t, kk):
        rows = t * SUBLANES if isinstance(t, int) else pl.multiple_of(t * SUBLANES, SUBLANES)
        dst = xs_out.at[pl.ds(pl.multiple_of(dsm[p, kk, t], SUBLANES), SUBLANES), :]
        return pltpu.make_async_copy(hn_st.at[p, pl.ds(rows, SUBLANES), :], dst, sem.at[p])

    @pl.when(i == 0)
    def _():
        cnt_ref[...] = cnt_in[...]
        hn_st[1] = jnp.zeros((tm * SUBLANES, LANES), F32)
        spare = lax.broadcasted_iota(I32, (SUBLANES, tm), 0) * tm + lax.broadcasted_iota(I32, (SUBLANES, tm), 1)
        dvm[...] = (N_EXPERTS * region_rows + spare) * SUBLANES
        dests_to_smem(1).start()
        dests_to_smem(1).wait()

    if not per_token_hist:
        @pl.when(i % tiles_per_seq == 0)
        def _():
            hist_ref[...] = jnp.zeros_like(hist_ref)

    @pl.when(i > 0)
    def _():
        dests_to_smem(prev).wait()

    for t in range(tm):
        for kk in range(TOP_K):
            row_scatter(prev, t, kk).start()

    u = cc_ref[...].astype(F32) * ch_ref[...].astype(F32)
    if per_token_hist:
        u2 = h0_ref[...]
        u1 = h1_ref[...]
        u_out[...] = u
    else:
        hm2 = hist_ref[0:1, :]
        hm1 = hist_ref[1:2, :]
        row = lax.broadcasted_iota(I32, u.shape, 0)
        u1 = jnp.where(row == 0, hm1, pltpu.roll(u, 1, axis=0))
        u2 = jnp.where(row == 0, hm2, jnp.where(row == 1, hm1, pltpu.roll(u, 2, axis=0)))
        last2 = u[tm - (CONV_W - 1):, :]
        hist_ref[...] = last2
        u_out[0] = last2
    conv = cw_ref[0:1, :] * u2 + cw_ref[1:2, :] * u1 + cw_ref[2:3, :] * u
    z = (cb_ref[...].astype(F32) * conv).astype(BF16)
    o_a = jnp.dot(z, wa_ref[...], preferred_element_type=F32)
    o_b = jnp.dot(r_ref[...].astype(BF16), wb_ref[...], preferred_element_type=F32)
    merged = (jax.nn.sigmoid(ga_ref[...].astype(F32)) * o_a
              + jax.nn.sigmoid(gb_ref[...].astype(F32)) * o_b)
    m = jnp.dot(merged.astype(BF16), wo_ref[...], preferred_element_type=F32)
    h = x_ref[...] + m
    h_out[...] = h
    hn = (h * lax.rsqrt(jnp.mean(h * h, axis=-1, keepdims=True) + NORM_EPS)) * n2_ref[...]
    hn_hi = hn.astype(BF16)
    hn_lo = (hn - hn_hi.astype(F32)).astype(BF16)
    nt = (((1,), (1,)), ((), ()))
    part = lax.dot_general(wr_ref[...], hn_hi, nt, preferred_element_type=F32)
    logits = (part[:N_EXPERTS] + part[N_EXPERTS:] + br_ref[...]
              + lax.dot_general(wr_ref[:N_EXPERTS, :], hn_lo, nt, preferred_element_type=F32))
    expert = lax.broadcasted_iota(I32, logits.shape, 0)
    vals, idxs = [], []
    for _ in range(TOP_K):
        top = jnp.max(logits, axis=0, keepdims=True)
        arg = jnp.min(jnp.where(logits == top, expert, N_EXPERTS), axis=0, keepdims=True)
        vals.append(top)
        idxs.append(arg)
        logits = jnp.where(expert == arg, -jnp.inf, logits)
    exps = [jnp.exp(val - vals[0]) for val in vals]
    denom = exps[0] + exps[1] + exps[2] + exps[3]

    picks = [expert == idx for idx in idxs]
    chosen = jnp.zeros(logits.shape, F32)
    for pick in picks:
        chosen = jnp.where(pick, 1.0, chosen)
    before = (lax.broadcasted_iota(I32, (tm, tm), 0) < lax.broadcasted_iota(I32, (tm, tm), 1))
    prefix = jnp.dot(chosen.astype(BF16), jnp.where(before, 1.0, 0.0).astype(BF16),
                     preferred_element_type=F32)
    counts = cnt_ref[...]
    base = jnp.concatenate([counts] * (tm // LANES), axis=1) + prefix
    ranks = [jnp.sum(jnp.where(pick, base, 0.0), axis=0, keepdims=True).astype(I32) for pick in picks]
    counts = counts + jnp.sum(chosen, axis=1, keepdims=True)
    cnt_ref[...] = counts
    cnt_out[...] = counts

    slot_row = lax.broadcasted_iota(I32, (SUBLANES, tm), 0)
    route = jnp.zeros((SUBLANES, tm), I32)
    gates = jnp.zeros((SUBLANES, tm), F32)
    dests = jnp.zeros((SUBLANES, tm), I32)
    for kk in range(TOP_K):
        route = jnp.where(slot_row == kk, idxs[kk], route)
        route = jnp.where(slot_row == TOP_K + kk, ranks[kk], route)
        gates = jnp.where(slot_row == kk, exps[kk] / denom, gates)
        dests = jnp.where(slot_row == kk, (idxs[kk] * region_rows + ranks[kk]) * SUBLANES, dests)
    route_out[...] = route
    gate_out[...] = gates

    @pl.when(i > 0)
    def _():
        wait_scatters(cur)

    for c in range(D_MODEL // LANES):
        hn_st[cur, pl.ds(c, tm, stride=SUBLANES), :] = hn[:, c * LANES:(c + 1) * LANES]
    dvm[...] = dests
    dests_to_smem(cur).start()

    @pl.when(i == last)
    def _():
        dests_to_smem(cur).wait()
        wait_scatters(prev)

        def issue(g, carry):
            for uu in range(ISSUE_UNROLL):
                for kk in range(TOP_K):
                    row_scatter(cur, g * ISSUE_UNROLL + uu, kk).start()
            return carry

        lax.fori_loop(0, tm // ISSUE_UNROLL, issue, 0)
        wait_scatters(cur)


def _merge(proj, r, x, hist, conv_w, wa, wb, wo, norm2, w_router, b_router, counts_in, xs_prev, region_rows,
           tm, tiles_per_seq):
    t = x.shape[0]
    per_token = hist is not None
    chained = xs_prev is not None
    col = lambda c: (lambda i: (i, c // D_MODEL))
    const2 = lambda i: (0, 0)
    tile = pl.BlockSpec((tm, D_MODEL), lambda i: (i, 0))
    in_specs = [
        pl.BlockSpec((tm, D_MODEL), col(COL_CB)),
        pl.BlockSpec((tm, D_MODEL), col(COL_CC)),
        pl.BlockSpec((tm, D_MODEL), col(COL_CH)),
        pl.BlockSpec((tm, D_MODEL), col(COL_GA)),
        pl.BlockSpec((tm, D_MODEL), col(COL_GB)),
        pl.BlockSpec((tm, RET_V), lambda i: (i, 0)),
        tile,
    ]
    args = [proj, proj, proj, proj, proj, r, x]
    if per_token:
        in_specs += [tile, tile]
        args += list(hist)
    in_specs += [
        pl.BlockSpec((CONV_W, D_MODEL), const2),
        pl.BlockSpec((D_MODEL, D_MODEL), const2),
        pl.BlockSpec((RET_V, D_MODEL), const2),
        pl.BlockSpec((D_MODEL, D_MODEL), const2),
        pl.BlockSpec((1, D_MODEL), const2),
        pl.BlockSpec((2 * N_EXPERTS, D_MODEL), const2),
        pl.BlockSpec((N_EXPERTS, tm), const2),
        pl.BlockSpec((N_EXPERTS, LANES), const2),
    ]
    args += [conv_w, wa, wb, wo, norm2, w_router, b_router, counts_in]
    aliases = {}
    if chained:
        aliases = {len(args): 5}
        in_specs.append(pl.BlockSpec(memory_space=pl.ANY))
        args.append(xs_prev)
    lanes_tile = pl.BlockSpec((SUBLANES, tm), lambda i: (0, i))
    if per_token:
        u_spec = tile
        u_shape = jax.ShapeDtypeStruct((t, D_MODEL), F32)
        scratch = []
    else:
        n_seq = t // (tm * tiles_per_seq)
        u_spec = pl.BlockSpec((1, CONV_W - 1, D_MODEL), lambda i: (i // tiles_per_seq, 0, 0))
        u_shape = jax.ShapeDtypeStruct((n_seq, CONV_W - 1, D_MODEL), F32)
        scratch = [pltpu.VMEM((CONV_W - 1, D_MODEL), F32)]
    out_specs = [tile, lanes_tile, lanes_tile, u_spec, pl.BlockSpec((N_EXPERTS, LANES), const2),
                 pl.BlockSpec(memory_space=pl.ANY)]
    out_shape = [
        jax.ShapeDtypeStruct((t, D_MODEL), F32),
        jax.ShapeDtypeStruct((SUBLANES, t), I32),
        jax.ShapeDtypeStruct((SUBLANES, t), F32),
        u_shape,
        jax.ShapeDtypeStruct((N_EXPERTS, LANES), F32),
        jax.ShapeDtypeStruct(((N_EXPERTS * region_rows + TOP_K * MERGE_TM) * SUBLANES, LANES), F32),
    ]
    scratch += [
        pltpu.VMEM((N_EXPERTS, LANES), F32),
        pltpu.VMEM((2, tm * SUBLANES, LANES), F32),
        pltpu.VMEM((SUBLANES, tm), I32),
        pltpu.SMEM((2, SUBLANES, tm), I32),
        pltpu.SemaphoreType.DMA((3,)),
    ]
    return pl.pallas_call(
        functools.partial(_merge_kernel, per_token, chained, tiles_per_seq, region_rows),
        grid=(t // tm,),
        in_specs=in_specs,
        out_specs=out_specs,
        out_shape=out_shape,
        scratch_shapes=scratch,
        input_output_aliases=aliases,
        compiler_params=_params(("arbitrary",)),
        name="merge_sample" if per_token else "merge_prompt",
    )(*args)


def _rows_from_tiles(ref, n_rows):
    return jnp.concatenate([ref[pl.ds(c, n_rows, stride=SUBLANES), :] for c in range(D_MODEL // LANES)], axis=1)


def _rows_to_tiles(ref, rows):
    for c in range(D_MODEL // LANES):
        ref[pl.ds(c, rows.shape[0], stride=SUBLANES), :] = rows[:, c * LANES:(c + 1) * LANES]


def _moe_kernel(be_ref, bx_ref, vr_ref, nused_ref, next_ref, x_ref, w1_hbm, b1_ref, w2_hbm, b2_ref, o_ref,
                w1f, w2f, w1b, w2b, sem):
    i = pl.program_id(0)
    n_used = nused_ref[0]

    def weight_copies(e):
        return (pltpu.make_async_copy(w1_hbm.at[e], w1f, sem.at[0]),
                pltpu.make_async_copy(w2_hbm.at[e], w2f, sem.at[1]))

    @pl.when(i < n_used)
    def _():
        e = be_ref[i]
        prev = be_ref[jnp.maximum(i - 1, 0)]

        @pl.when(i == 0)
        def _():
            for cp in weight_copies(e):
                cp.start()

        @pl.when((i == 0) | (e != prev))
        def _():
            for cp in weight_copies(e):
                cp.wait()
            w1b[...] = w1f[...].astype(BF16)
            w2b[...] = w2f[...].astype(BF16)
            nxt = next_ref[e]

            @pl.when(nxt >= 0)
            def _():
                for cp in weight_copies(nxt):
                    cp.start()

        live = lax.broadcasted_iota(I32, (MOE_BM, 1), 0) < vr_ref[i]
        x = jnp.where(live, _rows_from_tiles(x_ref, MOE_BM), 0.0).astype(BF16)
        h1 = jnp.dot(x, w1b[...], preferred_element_type=F32) + b1_ref[0]
        gate = jnp.minimum(h1[:, :D_FF], SWIGLU_LIMIT)
        up = jnp.clip(h1[:, D_FF:], -SWIGLU_LIMIT, SWIGLU_LIMIT)
        glu = gate * jax.nn.sigmoid(SWIGLU_ALPHA * gate)
        act = ((up + 1.0) * glu).astype(BF16)
        _rows_to_tiles(o_ref, jnp.dot(act, w2b[...], preferred_element_type=F32) + b2_ref[0])

    @pl.when(i >= n_used)
    def _():
        o_ref[...] = jnp.zeros_like(o_ref)


def _moe_blocks(xs, block_expert, block_xs, valid_rows, n_used, next_expert, w1, b1, w2, b2):
    n_blocks = block_expert.shape[0]
    grid_spec = pltpu.PrefetchScalarGridSpec(
        num_scalar_prefetch=5,
        grid=(n_blocks,),
        in_specs=[
            pl.BlockSpec((MOE_BM * SUBLANES, LANES), lambda i, be, bx, vr, nu, nx: (bx[i], 0)),
            pl.BlockSpec(memory_space=pl.ANY),
            pl.BlockSpec((1, 1, 2 * D_FF), lambda i, be, bx, vr, nu, nx: (be[i], 0, 0)),
            pl.BlockSpec(memory_space=pl.ANY),
            pl.BlockSpec((1, 1, D_MODEL), lambda i, be, bx, vr, nu, nx: (be[i], 0, 0)),
        ],
        out_specs=pl.BlockSpec((MOE_BM * SUBLANES, LANES), lambda i, be, bx, vr, nu, nx: (i, 0)),
        scratch_shapes=[
            pltpu.VMEM((D_MODEL, 2 * D_FF), F32),
            pltpu.VMEM((D_FF, D_MODEL), F32),
            pltpu.VMEM((D_MODEL, 2 * D_FF), BF16),
            pltpu.VMEM((D_FF, D_MODEL), BF16),
            pltpu.SemaphoreType.DMA((2,)),
        ],
    )
    return pl.pallas_call(
        _moe_kernel,
        grid_spec=grid_spec,
        out_shape=jax.ShapeDtypeStruct((n_blocks * MOE_BM * SUBLANES, LANES), F32),
        compiler_params=_params(("arbitrary",)),
        name="moe_experts",
    )(block_expert, block_xs, valid_rows, n_used, next_expert, xs, w1, b1[:, None, :], w2, b2[:, None, :])


def _block_plan(counts, n_tokens, region_rows):
    n_blocks = -(-n_tokens * TOP_K // MOE_BM) + N_EXPERTS
    blocks_e = (counts + MOE_BM - 1) // MOE_BM
    blk_end = jnp.cumsum(blocks_e)
    blk_start = blk_end - blocks_e
    n_used = blk_end[-1:]
    blk = jnp.maximum(jnp.minimum(jnp.arange(n_blocks, dtype=I32), n_used - 1), 0)
    expert = jnp.minimum(jnp.sum((blk_end[None, :] <= blk[:, None]).astype(I32), axis=1), N_EXPERTS - 1)
    within = blk - blk_start[expert]
    block_xs = expert * (region_rows // MOE_BM) + within
    valid = jnp.clip(counts[expert] - within * MOE_BM, 0, MOE_BM)
    ids = jnp.arange(N_EXPERTS, dtype=I32)
    later = jnp.where((ids[None, :] > ids[:, None]) & (counts[None, :] > 0), ids[None, :], N_EXPERTS)
    next_expert = jnp.min(later, axis=1)
    next_expert = jnp.where(next_expert == N_EXPERTS, -1, next_expert)
    return (expert, block_xs.astype(I32), valid.astype(I32), n_used.astype(I32), next_expert.astype(I32),
            (blk_start * MOE_BM).astype(I32))


def _combine_kernel(n_main, n_tokens, dest_ref, hp_ref, hs_ref, gp_ref, gs_ref, nf_ref, yb_hbm, op_ref, os_ref,
                    ybuf0, ybuf1, sem):
    i = pl.program_id(0)
    last = pl.num_programs(0) - 1
    bufs = (ybuf0, ybuf1)

    def row_copy(tile, t, kk, sl):
        row = pl.multiple_of(dest_ref[kk * n_tokens + tile * COMB_TT + t], SUBLANES)
        return pltpu.make_async_copy(yb_hbm.at[pl.ds(row, SUBLANES), :],
                                     bufs[sl].at[kk, pl.ds(t * SUBLANES, SUBLANES), :], sem.at[sl])

    def wait_tile(sl):
        for kk in range(TOP_K):
            pltpu.make_async_copy(yb_hbm.at[pl.ds(0, COMB_TT * SUBLANES), :], bufs[sl].at[kk], sem.at[sl]).wait()

    @pl.when(i == 0)
    def _():
        for t in range(COMB_TT):
            for kk in range(TOP_K):
                row_copy(0, t, kk, 0).start()

    def step(sl, h_ref, gate_ref, o_ref):
        wait_tile(sl)
        nxt = jnp.minimum(i + 1, last)
        for t in range(COMB_TT):
            for kk in range(TOP_K):
                row_copy(nxt, t, kk, 1 - sl).start()
        pad = jnp.zeros((COMB_TT - SUBLANES, COMB_TT), F32)
        gates = jnp.concatenate([gate_ref[...], pad], axis=0).T
        y = gates[:, 0:1] * _rows_from_tiles(bufs[sl].at[0], COMB_TT)
        for kk in range(1, TOP_K):
            y = y + gates[:, kk:kk + 1] * _rows_from_tiles(bufs[sl].at[kk], COMB_TT)
        h = h_ref[...] + y
        o_ref[...] = (h * lax.rsqrt(jnp.mean(h * h, axis=-1, keepdims=True) + NORM_EPS)) * nf_ref[...]

    for sl in range(2):
        @pl.when((i % 2 == sl) & (i < n_main))
        def _():
            step(sl, hp_ref, gp_ref, op_ref)

        @pl.when((i % 2 == sl) & (i >= n_main))
        def _():
            step(sl, hs_ref, gs_ref, os_ref)

        @pl.when((i == last) & (i % 2 == sl))
        def _():
            wait_tile(1 - sl)


def _combine(dest, h_p, h_s, gate_p, gate_s, norm_f, yb):
    n_main = h_p.shape[0] // COMB_TT
    n_tail = h_s.shape[0] // COMB_TT
    main = lambda i, d: (jnp.minimum(i, n_main - 1), 0)
    tail = lambda i, d: (jnp.maximum(i - n_main, 0), 0)
    grid_spec = pltpu.PrefetchScalarGridSpec(
        num_scalar_prefetch=1,
        grid=(n_main + n_tail,),
        in_specs=[
            pl.BlockSpec((COMB_TT, D_MODEL), main),
            pl.BlockSpec((COMB_TT, D_MODEL), tail),
            pl.BlockSpec((SUBLANES, COMB_TT), lambda i, d: (0, jnp.minimum(i, n_main - 1))),
            pl.BlockSpec((SUBLANES, COMB_TT), lambda i, d: (0, jnp.maximum(i - n_main, 0))),
            pl.BlockSpec((1, D_MODEL), lambda i, d: (0, 0)),
            pl.BlockSpec(memory_space=pl.ANY),
        ],
        out_specs=[
            pl.BlockSpec((COMB_TT, D_MODEL), main),
            pl.BlockSpec((COMB_TT, D_MODEL), tail),
        ],
        scratch_shapes=[
            pltpu.VMEM((TOP_K, COMB_TT * SUBLANES, LANES), F32),
            pltpu.VMEM((TOP_K, COMB_TT * SUBLANES, LANES), F32),
            pltpu.SemaphoreType.DMA((2,)),
        ],
    )
    return pl.pallas_call(
        functools.partial(_combine_kernel, n_main, h_p.shape[0] + h_s.shape[0]),
        grid_spec=grid_spec,
        out_shape=[jax.ShapeDtypeStruct(h_p.shape, F32), jax.ShapeDtypeStruct(h_s.shape, F32)],
        compiler_params=_params(("arbitrary",)),
        name="combine",
    )(dest, h_p, h_s, gate_p, gate_s, norm_f, yb)


def kernel(x_prompt, x_sample, state_conv, state_ret, norm1, w_in, conv_w, w_a, w_b, w_o, norm2, w_router,
           b_router, w_e1, b_e1, w_e2, b_e2, norm_f):
    batch, seq_len, _ = x_prompt.shape
    n_dec, dec_seq, _ = x_sample.shape
    depth = norm1.shape[0]
    assert dec_seq == 1 and depth == 1, "single-token decode step of a one-layer trunk"
    past_len = 16384
    t_p = batch * seq_len
    n_tokens = t_p + n_dec
    region_rows = _expert_region_rows(n_tokens)
    lyr = 0
    xp = x_prompt.reshape(t_p, D_MODEL)
    xs = x_sample.reshape(n_dec, D_MODEL)
    g1 = norm1[lyr][None, :]
    w_in_b = w_in[lyr].astype(BF16)
    wa, wb, wo = w_a[lyr].astype(BF16), w_b[lyr].astype(BF16), w_o[lyr].astype(BF16)
    g2 = norm2[lyr][None, :]
    wr_t = w_router[lyr].T
    wr_hi = wr_t.astype(BF16)
    wr_split = jnp.concatenate([wr_hi, (wr_t - wr_hi.astype(F32)).astype(BF16)], axis=0)
    br_col = b_router[lyr][:, None]

    proj_p = _inproj(xp, g1, w_in_b, 1024, BF16)
    r_p, ret_p = _retention_prompt(proj_p, batch, seq_len)
    h_p, route_p, gate_p, conv_p, counts, dispatch = _merge(
        proj_p, r_p, xp, None, conv_w[lyr], wa, wb, wo, g2, wr_split, jnp.broadcast_to(br_col, (N_EXPERTS, MERGE_TM)),
        jnp.zeros((N_EXPERTS, LANES), F32), None, region_rows, MERGE_TM, seq_len // MERGE_TM)
    proj_s = _inproj(xs, g1, w_in_b, n_dec, F32)
    r_s, ret_s = _retention_sample(proj_s, state_ret[lyr], past_len)
    hist = (state_conv[lyr][:, 0, :], state_conv[lyr][:, 1, :])
    h_s, route_s, gate_s, u_s, counts, dispatch = _merge(
        proj_s, r_s, xs, hist, conv_w[lyr], wa, wb, wo, g2, wr_split, jnp.broadcast_to(br_col, (N_EXPERTS, n_dec)),
        counts, dispatch, region_rows, n_dec, 1)
    conv_s = jnp.stack([state_conv[lyr][:, 1, :], u_s], axis=1)

    counts_i = counts[:, 0].astype(I32)
    block_expert, block_xs, valid_rows, n_used, next_expert, row_start = _block_plan(
        counts_i, n_tokens, region_rows)
    yb = _moe_blocks(dispatch, block_expert, block_xs, valid_rows, n_used, next_expert,
                     w_e1[lyr], b_e1[lyr], w_e2[lyr], b_e2[lyr])
    route = jnp.concatenate([route_p, route_s], axis=1)
    expert_ids = jnp.arange(N_EXPERTS, dtype=I32)[:, None, None]
    first_row = jnp.sum(jnp.where(route[None, :TOP_K] == expert_ids, row_start[:, None, None], 0), axis=0)
    dest = ((first_row + route[TOP_K:]) * SUBLANES).reshape(TOP_K * n_tokens)
    y_p, y_s = _combine(dest, h_p, h_s, gate_p, gate_s, norm_f[None, :], yb)

    y_prompt = y_p.reshape(batch, seq_len, D_MODEL)
    y_sample = y_s.reshape(n_dec, 1, D_MODEL)
    return (y_prompt, y_sample, conv_p[None], ret_p[None], conv_s[None], ret_s[None])
```

```python
import functools

import jax
import jax.numpy as jnp
from jax import lax
from jax.experimental import pallas as pl
from jax.experimental.pallas import tpu as pltpu

F32 = jnp.float32
BF16 = jnp.bfloat16
I32 = jnp.int32

D_MODEL = 1024
CONV_W = 3
N_HEADS = 4
DK = 256
DV = 512
HALF = DK // 2
RET_QK = N_HEADS * DK
RET_V = N_HEADS * DV
ROPE_BASE = 10000.0
N_EXPERTS = 32
TOP_K = 4
D_FF = 1024
SWIGLU_ALPHA = 1.702
SWIGLU_LIMIT = 7.0
NORM_EPS = 1e-5
N_IN_COLS = 3 * D_MODEL + 2 * RET_QK + 2 * RET_V + 2 * D_MODEL
COL_CB, COL_CC, COL_CH = 0, D_MODEL, 2 * D_MODEL
COL_Q = 3 * D_MODEL
COL_K = COL_Q + RET_QK
COL_V = COL_K + RET_QK
COL_G = COL_V + RET_V
COL_GA = COL_G + RET_V
COL_GB = COL_GA + D_MODEL

RET_CHUNK = 256
MERGE_TM = 256
INPROJ_TN = 1024
MOE_BM = 256
COMB_TT = 128
COMB_RING = 3
SAMPLE_GROUP = 8
LANES = 128
SUBLANES = 8
ISSUE_UNROLL = 8
VMEM_LIMIT = 56 * 1024 * 1024


def _params(sem):
    return pltpu.CompilerParams(dimension_semantics=sem, vmem_limit_bytes=VMEM_LIMIT)


def _expert_region_rows(n_tokens):
    return -(-n_tokens // MOE_BM) * MOE_BM


def _inproj_kernel(x_ref, g_ref, w_ref, o_ref, xn_ref):
    @pl.when(pl.program_id(1) == 0)
    def _():
        x = x_ref[...]
        ms = jnp.mean(x * x, axis=-1, keepdims=True)
        xn_ref[...] = ((x * lax.rsqrt(ms + NORM_EPS)) * g_ref[...]).astype(BF16)

    o_ref[...] = jnp.dot(xn_ref[...], w_ref[...], preferred_element_type=F32).astype(o_ref.dtype)


def _inproj(x, gain, w_bf16, tm, out_dtype):
    m = x.shape[0]
    n = w_bf16.shape[1]
    return pl.pallas_call(
        _inproj_kernel,
        grid=(m // tm, n // INPROJ_TN),
        in_specs=[
            pl.BlockSpec((tm, D_MODEL), lambda i, j: (i, 0)),
            pl.BlockSpec((1, D_MODEL), lambda i, j: (0, 0)),
            pl.BlockSpec((D_MODEL, INPROJ_TN), lambda i, j: (0, j)),
        ],
        out_specs=pl.BlockSpec((tm, INPROJ_TN), lambda i, j: (i, j)),
        out_shape=jax.ShapeDtypeStruct((m, n), out_dtype),
        scratch_shapes=[pltpu.VMEM((tm, D_MODEL), BF16)],
        compiler_params=_params(("arbitrary", "arbitrary")),
        name="inproj",
    )(x, gain, w_bf16)


def _rotary(x, cos, sin):
    x1 = x[:, :HALF]
    x2 = x[:, HALF:]
    return jnp.concatenate([x1 * cos - x2 * sin, x1 * sin + x2 * cos], axis=-1)


def _group_norm_gate(o, g):
    o = o * lax.rsqrt(jnp.mean(o * o, axis=-1, keepdims=True) + NORM_EPS)
    return (g * jax.nn.sigmoid(g)) * o


def _ret_chunk_kernel(q_ref, k_ref, v0_ref, v1_ref, g0_ref, g1_ref, cos_ref, sin_ref, mask_ref, qd_ref, kd_ref,
                      cd_ref, r_ref, s_ref):
    @pl.when(pl.program_id(1) == 0)
    def _():
        s_ref[...] = jnp.zeros_like(s_ref)

    cos = cos_ref[...]
    sin = sin_ref[...]
    v_refs = (v0_ref, v1_ref)
    g_refs = (g0_ref, g1_ref)
    for h in range(N_HEADS):
        half_cols = slice((h % 2) * DV, (h % 2 + 1) * DV)
        q = _rotary(q_ref[:, h * DK:(h + 1) * DK].astype(F32), cos, sin)
        k = _rotary(k_ref[:, h * DK:(h + 1) * DK].astype(F32), cos, sin) * (DK ** -0.5)
        v = v_refs[h // 2][:, half_cols]
        qd = jnp.concatenate([qd_ref[h], qd_ref[h]], axis=-1)
        kd = jnp.concatenate([kd_ref[h], kd_ref[h]], axis=-1)
        state = s_ref[0, h]
        scores = lax.dot_general(q.astype(BF16), k.astype(BF16), (((1,), (1,)), ((), ())),
                                 preferred_element_type=F32) * mask_ref[h]
        intra = jnp.dot(scores.astype(BF16), v, preferred_element_type=F32)
        cross = jnp.dot((q * qd).astype(BF16), state.astype(BF16), preferred_element_type=F32)
        kv = jnp.dot((k * kd).T.astype(BF16), v, preferred_element_type=F32)
        s_ref[0, h] = cd_ref[h] * state + kv
        g = g_refs[h // 2][:, half_cols].astype(F32)
        r_ref[:, h * DV:(h + 1) * DV] = _group_norm_gate(intra + cross, g).astype(r_ref.dtype)


def _retention_tables(chunk, seq_len, pos0):
    log_g = jnp.log1p(-jnp.exp2(-5.0 - jnp.arange(N_HEADS, dtype=F32)))
    pos = jnp.arange(chunk, dtype=F32)
    diff = pos[:, None] - pos[None, :]
    mask = jnp.where(diff >= 0, jnp.exp(jnp.maximum(diff, 0.0)[None] * log_g[:, None, None]), 0.0)
    q_decay = jnp.exp((pos + 1.0)[None, :] * log_g[:, None])[..., None]
    k_decay = jnp.exp((chunk - 1.0 - pos)[None, :] * log_g[:, None])[..., None]
    chunk_decay = jnp.exp(chunk * log_g)[:, None, None]
    qd = jnp.broadcast_to(q_decay, (N_HEADS, chunk, LANES))
    kd = jnp.broadcast_to(k_decay, (N_HEADS, chunk, LANES))
    cd = jnp.broadcast_to(chunk_decay, (N_HEADS, 1, DV))
    positions = pos0 + jnp.arange(seq_len, dtype=jnp.int32)
    inv = ROPE_BASE ** (-jnp.linspace(0.0, 1.0, HALF, dtype=F32))
    ang = positions.astype(F32)[:, None] * inv[None, :]
    return mask, qd, kd, cd, jnp.cos(ang), jnp.sin(ang)


def _retention_prompt(proj, batch, seq_len):
    c = RET_CHUNK
    nc = seq_len // c
    mask, qd, kd, cd, cos, sin = _retention_tables(c, seq_len, 0)
    row = lambda b, j: b * nc + j
    full3 = lambda b, j: (0, 0, 0)
    return pl.pallas_call(
        _ret_chunk_kernel,
        grid=(batch, nc),
        in_specs=[
            pl.BlockSpec((c, RET_QK), lambda b, j: (row(b, j), COL_Q // RET_QK)),
            pl.BlockSpec((c, RET_QK), lambda b, j: (row(b, j), COL_K // RET_QK)),
            pl.BlockSpec((c, RET_QK), lambda b, j: (row(b, j), COL_V // RET_QK)),
            pl.BlockSpec((c, RET_QK), lambda b, j: (row(b, j), COL_V // RET_QK + 1)),
            pl.BlockSpec((c, RET_QK), lambda b, j: (row(b, j), COL_G // RET_QK)),
            pl.BlockSpec((c, RET_QK), lambda b, j: (row(b, j), COL_G // RET_QK + 1)),
            pl.BlockSpec((c, HALF), lambda b, j: (j, 0)),
            pl.BlockSpec((c, HALF), lambda b, j: (j, 0)),
            pl.BlockSpec((N_HEADS, c, c), full3),
            pl.BlockSpec((N_HEADS, c, LANES), full3),
            pl.BlockSpec((N_HEADS, c, LANES), full3),
            pl.BlockSpec((N_HEADS, 1, DV), full3),
        ],
        out_specs=[
            pl.BlockSpec((c, RET_V), lambda b, j: (row(b, j), 0)),
            pl.BlockSpec((1, N_HEADS, DK, DV), lambda b, j: (b, 0, 0, 0)),
        ],
        out_shape=[
            jax.ShapeDtypeStruct((batch * seq_len, RET_V), BF16),
            jax.ShapeDtypeStruct((batch, N_HEADS, DK, DV), F32),
        ],
        compiler_params=_params(("arbitrary", "arbitrary")),
        name="retention_chunk",
    )(proj, proj, proj, proj, proj, proj, cos, sin, mask, qd, kd, cd)


def _ret_step_kernel(q_ref, k_ref, v_ref, g_ref, cos_ref, sin_ref, gam_ref, s_ref, r_ref, so_ref):
    cos = cos_ref[...]
    sin = sin_ref[...]
    gam = gam_ref[0]
    q = _rotary(q_ref[...], cos, sin)
    k = _rotary(k_ref[...], cos, sin) * (DK ** -0.5)
    v = v_ref[...]
    intra = jnp.sum(q * k, axis=-1, keepdims=True) * v
    q_t = (q * gam[:, :DK]).T
    k_t = k.T
    cross = []
    for s in range(SAMPLE_GROUP):
        state = s_ref[s, 0]
        cross.append(jnp.sum(q_t[:, s:s + 1] * state, axis=0, keepdims=True))
        so_ref[s, 0] = gam * state + k_t[:, s:s + 1] * v[s:s + 1, :]
    o = intra + jnp.concatenate(cross, axis=0)
    r_ref[...] = _group_norm_gate(o, g_ref[...])


def _retention_sample(proj, state, pos0):
    n = proj.shape[0]
    grp = SAMPLE_GROUP
    log_g = jnp.log1p(-jnp.exp2(-5.0 - jnp.arange(N_HEADS, dtype=F32)))
    gam = jnp.broadcast_to(jnp.exp(log_g)[:, None, None], (N_HEADS, 1, DV))
    _, _, _, _, cos, sin = _retention_tables(1, 1, pos0)
    return pl.pallas_call(
        _ret_step_kernel,
        grid=(n // grp, N_HEADS),
        in_specs=[
            pl.BlockSpec((grp, DK), lambda j, h: (j, COL_Q // DK + h)),
            pl.BlockSpec((grp, DK), lambda j, h: (j, COL_K // DK + h)),
            pl.BlockSpec((grp, DV), lambda j, h: (j, COL_V // DV + h)),
            pl.BlockSpec((grp, DV), lambda j, h: (j, COL_G // DV + h)),
            pl.BlockSpec((1, HALF), lambda j, h: (0, 0)),
            pl.BlockSpec((1, HALF), lambda j, h: (0, 0)),
            pl.BlockSpec((1, 1, DV), lambda j, h: (h, 0, 0)),
            pl.BlockSpec((grp, 1, DK, DV), lambda j, h: (j, h, 0, 0)),
        ],
        out_specs=[
            pl.BlockSpec((grp, DV), lambda j, h: (j, h)),
            pl.BlockSpec((grp, 1, DK, DV), lambda j, h: (j, h, 0, 0)),
        ],
        out_shape=[
            jax.ShapeDtypeStruct((n, RET_V), F32),
            jax.ShapeDtypeStruct(state.shape, F32),
        ],
        compiler_params=_params(("arbitrary", "arbitrary")),
        name="retention_step",
    )(proj, proj, proj, proj, cos, sin, gam, state)


def _merge_kernel(per_token_hist, chained, tiles_per_seq, region_rows, *refs):
    refs = list(refs)
    cb_ref, cc_ref, ch_ref, ga_ref, gb_ref, r_ref, x_ref = refs[:7]
    del refs[:7]
    if per_token_hist:
        h0_ref, h1_ref = refs[:2]
        del refs[:2]
    cw_ref, wa_ref, wb_ref, wo_ref, n2_ref, wr_ref, br_ref, cnt_in = refs[:8]
    del refs[:8]
    if chained:
        del refs[:1]
    h_out, route_out, gate_out, u_out, cnt_out, xs_out = refs[:6]
    del refs[:6]
    if not per_token_hist:
        hist_ref = refs.pop(0)
    cnt_ref, hn_st, dvm, dsm, sem = refs

    i = pl.program_id(0)
    last = pl.num_programs(0) - 1
    tm = x_ref.shape[0]
    cur = i % 2
    prev = 1 - cur

    def wait_scatters(p):
        for _ in range(TOP_K):
            pltpu.make_async_copy(hn_st.at[p], xs_out.at[pl.ds(0, tm * SUBLANES), :], sem.at[p]).wait()

    def dests_to_smem(p):
        return pltpu.make_async_copy(dvm, dsm.at[p], sem.at[2])

    def row_scatter(p, t, kk):
        rows = t * SUBLANES if isinstance(t, int) else pl.multiple_of(t * SUBLANES, SUBLANES)
        dst = xs_out.at[pl.ds(pl.multiple_of(dsm[p, kk, t], SUBLANES), SUBLANES), :]
        return pltpu.make_async_copy(hn_st.at[p, pl.ds(rows, SUBLANES), :], dst, sem.at[p])

    @pl.when(i == 0)
    def _():
        cnt_ref[...] = cnt_in[...]
        hn_st[1] = jnp.zeros((tm * SUBLANES, LANES), F32)
        spare = lax.broadcasted_iota(I32, (SUBLANES, tm), 0) * tm + lax.broadcasted_iota(I32, (SUBLANES, tm), 1)
        dvm[...] = (N_EXPERTS * region_rows + spare) * SUBLANES
        dests_to_smem(1).start()
        dests_to_smem(1).wait()

    if not per_token_hist:
        @pl.when(i % tiles_per_seq == 0)
        def _():
            hist_ref[...] = jnp.zeros_like(hist_ref)

    @pl.when(i > 0)
    def _():
        dests_to_smem(prev).wait()

    for t in range(tm):
        for kk in range(TOP_K):
            row_scatter(prev, t, kk).start()

    u = cc_ref[...].astype(F32) * ch_ref[...].astype(F32)
    if per_token_hist:
        u2 = h0_ref[...]
        u1 = h1_ref[...]
        u_out[...] = u
    else:
        hm2 = hist_ref[0:1, :]
        hm1 = hist_ref[1:2, :]
        row = lax.broadcasted_iota(I32, u.shape, 0)
        u1 = jnp.where(row == 0, hm1, pltpu.roll(u, 1, axis=0))
        u2 = jnp.where(row == 0, hm2, jnp.where(row == 1, hm1, pltpu.roll(u, 2, axis=0)))
        last2 = u[tm - (CONV_W - 1):, :]
        hist_ref[...] = last2
        u_out[0] = last2
    conv = cw_ref[0:1, :] * u2 + cw_ref[1:2, :] * u1 + cw_ref[2:3, :] * u
    z = (cb_ref[...].astype(F32) * conv).astype(BF16)
    o_a = jnp.dot(z, wa_ref[...], preferred_element_type=F32)
    o_b = jnp.dot(r_ref[...].astype(BF16), wb_ref[...], preferred_element_type=F32)
    merged = (jax.nn.sigmoid(ga_ref[...].astype(F32)) * o_a
              + jax.nn.sigmoid(gb_ref[...].astype(F32)) * o_b)
    m = jnp.dot(merged.astype(BF16), wo_ref[...], preferred_element_type=F32)
    h = x_ref[...] + m
    h_out[...] = h
    hn = (h * lax.rsqrt(jnp.mean(h * h, axis=-1, keepdims=True) + NORM_EPS)) * n2_ref[...]
    hn_hi = hn.astype(BF16)
    hn_lo = (hn - hn_hi.astype(F32)).astype(BF16)
    nt = (((1,), (1,)), ((), ()))
    part = lax.dot_general(wr_ref[...], hn_hi, nt, preferred_element_type=F32)
    logits = (part[:N_EXPERTS] + part[N_EXPERTS:] + br_ref[...]
              + lax.dot_general(wr_ref[:N_EXPERTS, :], hn_lo, nt, preferred_element_type=F32))
    expert = lax.broadcasted_iota(I32, logits.shape, 0)
    vals, idxs = [], []
    for _ in range(TOP_K):
        top = jnp.max(logits, axis=0, keepdims=True)
        arg = jnp.min(jnp.where(logits == top, expert, N_EXPERTS), axis=0, keepdims=True)
        vals.append(top)
        idxs.append(arg)
        logits = jnp.where(expert == arg, -jnp.inf, logits)
    exps = [jnp.exp(val - vals[0]) for val in vals]
    denom = exps[0] + exps[1] + exps[2] + exps[3]

    picks = [expert == idx for idx in idxs]
    chosen = jnp.zeros(logits.shape, F32)
    for pick in picks:
        chosen = jnp.where(pick, 1.0, chosen)
    before = (lax.broadcasted_iota(I32, (tm, tm), 0) < lax.broadcasted_iota(I32, (tm, tm), 1))
    prefix = jnp.dot(chosen.astype(BF16), jnp.where(before, 1.0, 0.0).astype(BF16),
                     preferred_element_type=F32)
    counts = cnt_ref[...]
    base = jnp.concatenate([counts] * (tm // LANES), axis=1) + prefix
    ranks = [jnp.sum(jnp.where(pick, base, 0.0), axis=0, keepdims=True).astype(I32) for pick in picks]
    counts = counts + jnp.sum(chosen, axis=1, keepdims=True)
    cnt_ref[...] = counts
    cnt_out[...] = counts

    slot_row = lax.broadcasted_iota(I32, (SUBLANES, tm), 0)
    route = jnp.zeros((SUBLANES, tm), I32)
    gates = jnp.zeros((SUBLANES, tm), F32)
    dests = jnp.zeros((SUBLANES, tm), I32)
    for kk in range(TOP_K):
        route = jnp.where(slot_row == kk, idxs[kk], route)
        route = jnp.where(slot_row == TOP_K + kk, ranks[kk], route)
        gates = jnp.where(slot_row == kk, exps[kk] / denom, gates)
        dests = jnp.where(slot_row == kk, (idxs[kk] * region_rows + ranks[kk]) * SUBLANES, dests)
    route_out[...] = route
    gate_out[...] = gates

    @pl.when(i > 0)
    def _():
        wait_scatters(cur)

    for c in range(D_MODEL // LANES):
        hn_st[cur, pl.ds(c, tm, stride=SUBLANES), :] = hn[:, c * LANES:(c + 1) * LANES]
    dvm[...] = dests
    dests_to_smem(cur).start()

    @pl.when(i == last)
    def _():
        dests_to_smem(cur).wait()
        wait_scatters(prev)

        def issue(g, carry):
            for uu in range(ISSUE_UNROLL):
                for kk in range(TOP_K):
                    row_scatter(cur, g * ISSUE_UNROLL + uu, kk).start()
            return carry

        lax.fori_loop(0, tm // ISSUE_UNROLL, issue, 0)
        wait_scatters(cur)


def _merge(proj, r, x, hist, conv_w, wa, wb, wo, norm2, w_router, b_router, counts_in, xs_prev, region_rows,
           tm, tiles_per_seq):
    t = x.shape[0]
    per_token = hist is not None
    chained = xs_prev is not None
    col = lambda c: (lambda i: (i, c // D_MODEL))
    const2 = lambda i: (0, 0)
    tile = pl.BlockSpec((tm, D_MODEL), lambda i: (i, 0))
    in_specs = [
        pl.BlockSpec((tm, D_MODEL), col(COL_CB)),
        pl.BlockSpec((tm, D_MODEL), col(COL_CC)),
        pl.BlockSpec((tm, D_MODEL), col(COL_CH)),
        pl.BlockSpec((tm, D_MODEL), col(COL_GA)),
        pl.BlockSpec((tm, D_MODEL), col(COL_GB)),
        pl.BlockSpec((tm, RET_V), lambda i: (i, 0)),
        tile,
    ]
    args = [proj, proj, proj, proj, proj, r, x]
    if per_token:
        in_specs += [tile, tile]
        args += list(hist)
    in_specs += [
        pl.BlockSpec((CONV_W, D_MODEL), const2),
        pl.BlockSpec((D_MODEL, D_MODEL), const2),
        pl.BlockSpec((RET_V, D_MODEL), const2),
        pl.BlockSpec((D_MODEL, D_MODEL), const2),
        pl.BlockSpec((1, D_MODEL), const2),
        pl.BlockSpec((2 * N_EXPERTS, D_MODEL), const2),
        pl.BlockSpec((N_EXPERTS, tm), const2),
        pl.BlockSpec((N_EXPERTS, LANES), const2),
    ]
    args += [conv_w, wa, wb, wo, norm2, w_router, b_router, counts_in]
    aliases = {}
    if chained:
        aliases = {len(args): 5}
        in_specs.append(pl.BlockSpec(memory_space=pl.ANY))
        args.append(xs_prev)
    lanes_tile = pl.BlockSpec((SUBLANES, tm), lambda i: (0, i))
    if per_token:
        u_spec = tile
        u_shape = jax.ShapeDtypeStruct((t, D_MODEL), F32)
        scratch = []
    else:
        n_seq = t // (tm * tiles_per_seq)
        u_spec = pl.BlockSpec((1, CONV_W - 1, D_MODEL), lambda i: (i // tiles_per_seq, 0, 0))
        u_shape = jax.ShapeDtypeStruct((n_seq, CONV_W - 1, D_MODEL), F32)
        scratch = [pltpu.VMEM((CONV_W - 1, D_MODEL), F32)]
    out_specs = [tile, lanes_tile, lanes_tile, u_spec, pl.BlockSpec((N_EXPERTS, LANES), const2),
                 pl.BlockSpec(memory_space=pl.ANY)]
    out_shape = [
        jax.ShapeDtypeStruct((t, D_MODEL), F32),
        jax.ShapeDtypeStruct((SUBLANES, t), I32),
        jax.ShapeDtypeStruct((SUBLANES, t), F32),
        u_shape,
        jax.ShapeDtypeStruct((N_EXPERTS, LANES), F32),
        jax.ShapeDtypeStruct(((N_EXPERTS * region_rows + TOP_K * MERGE_TM) * SUBLANES, LANES), F32),
    ]
    scratch += [
        pltpu.VMEM((N_EXPERTS, LANES), F32),
        pltpu.VMEM((2, tm * SUBLANES, LANES), F32),
        pltpu.VMEM((SUBLANES, tm), I32),
        pltpu.SMEM((2, SUBLANES, tm), I32),
        pltpu.SemaphoreType.DMA((3,)),
    ]
    return pl.pallas_call(
        functools.partial(_merge_kernel, per_token, chained, tiles_per_seq, region_rows),
        grid=(t // tm,),
        in_specs=in_specs,
        out_specs=out_specs,
        out_shape=out_shape,
        scratch_shapes=scratch,
        input_output_aliases=aliases,
        compiler_params=_params(("arbitrary",)),
        name="merge_sample" if per_token else "merge_prompt",
    )(*args)


def _rows_from_tiles(ref, n_rows):
    return jnp.concatenate([ref[pl.ds(c, n_rows, stride=SUBLANES), :] for c in range(D_MODEL // LANES)], axis=1)


def _rows_to_tiles(ref, rows):
    for c in range(D_MODEL // LANES):
        ref[pl.ds(c, rows.shape[0], stride=SUBLANES), :] = rows[:, c * LANES:(c + 1) * LANES]


def _moe_kernel(be_ref, bx_ref, vr_ref, nused_ref, next_ref, x_ref, w1_hbm, b1_ref, w2_hbm, b2_ref, o_ref,
                w1f, w2f, w1b, w2b, sem):
    i = pl.program_id(0)
    n_used = nused_ref[0]

    def weight_copies(e):
        return (pltpu.make_async_copy(w1_hbm.at[e], w1f, sem.at[0]),
                pltpu.make_async_copy(w2_hbm.at[e], w2f, sem.at[1]))

    @pl.when(i < n_used)
    def _():
        e = be_ref[i]
        prev = be_ref[jnp.maximum(i - 1, 0)]

        @pl.when(i == 0)
        def _():
            for cp in weight_copies(e):
                cp.start()

        @pl.when((i == 0) | (e != prev))
        def _():
            for cp in weight_copies(e):
                cp.wait()
            w1b[...] = w1f[...].astype(BF16)
            w2b[...] = w2f[...].astype(BF16)
            nxt = next_ref[e]

            @pl.when(nxt >= 0)
            def _():
                for cp in weight_copies(nxt):
                    cp.start()

        live = lax.broadcasted_iota(I32, (MOE_BM, 1), 0) < vr_ref[i]
        x = jnp.where(live, _rows_from_tiles(x_ref, MOE_BM), 0.0).astype(BF16)
        h1 = jnp.dot(x, w1b[...], preferred_element_type=F32) + b1_ref[0]
        gate = jnp.minimum(h1[:, :D_FF], SWIGLU_LIMIT)
        up = jnp.clip(h1[:, D_FF:], -SWIGLU_LIMIT, SWIGLU_LIMIT)
        glu = gate * jax.nn.sigmoid(SWIGLU_ALPHA * gate)
        act = ((up + 1.0) * glu).astype(BF16)
        _rows_to_tiles(o_ref, jnp.dot(act, w2b[...], preferred_element_type=F32) + b2_ref[0])

    @pl.when(i >= n_used)
    def _():
        o_ref[...] = jnp.zeros_like(o_ref)


def _moe_blocks(xs, block_expert, block_xs, valid_rows, n_used, next_expert, w1, b1, w2, b2):
    n_blocks = block_expert.shape[0]
    grid_spec = pltpu.PrefetchScalarGridSpec(
        num_scalar_prefetch=5,
        grid=(n_blocks,),
        in_specs=[
            pl.BlockSpec((MOE_BM * SUBLANES, LANES), lambda i, be, bx, vr, nu, nx: (bx[i], 0)),
            pl.BlockSpec(memory_space=pl.ANY),
            pl.BlockSpec((1, 1, 2 * D_FF), lambda i, be, bx, vr, nu, nx: (be[i], 0, 0)),
            pl.BlockSpec(memory_space=pl.ANY),
            pl.BlockSpec((1, 1, D_MODEL), lambda i, be, bx, vr, nu, nx: (be[i], 0, 0)),
        ],
        out_specs=pl.BlockSpec((MOE_BM * SUBLANES, LANES), lambda i, be, bx, vr, nu, nx: (i, 0)),
        scratch_shapes=[
            pltpu.VMEM((D_MODEL, 2 * D_FF), F32),
            pltpu.VMEM((D_FF, D_MODEL), F32),
            pltpu.VMEM((D_MODEL, 2 * D_FF), BF16),
            pltpu.VMEM((D_FF, D_MODEL), BF16),
            pltpu.SemaphoreType.DMA((2,)),
        ],
    )
    return pl.pallas_call(
        _moe_kernel,
        grid_spec=grid_spec,
        out_shape=jax.ShapeDtypeStruct((n_blocks * MOE_BM * SUBLANES, LANES), F32),
        compiler_params=_params(("arbitrary",)),
        name="moe_experts",
    )(block_expert, block_xs, valid_rows, n_used, next_expert, xs, w1, b1[:, None, :], w2, b2[:, None, :])


def _block_plan(counts, n_tokens, region_rows):
    n_blocks = -(-n_tokens * TOP_K // MOE_BM) + N_EXPERTS
    blocks_e = (counts + MOE_BM - 1) // MOE_BM
    blk_end = jnp.cumsum(blocks_e)
    blk_start = blk_end - blocks_e
    n_used = blk_end[-1:]
    blk = jnp.maximum(jnp.minimum(jnp.arange(n_blocks, dtype=I32), n_used - 1), 0)
    expert = jnp.minimum(jnp.sum((blk_end[None, :] <= blk[:, None]).astype(I32), axis=1), N_EXPERTS - 1)
    within = blk - blk_start[expert]
    block_xs = expert * (region_rows // MOE_BM) + within
    valid = jnp.clip(counts[expert] - within * MOE_BM, 0, MOE_BM)
    ids = jnp.arange(N_EXPERTS, dtype=I32)
    later = jnp.where((ids[None, :] > ids[:, None]) & (counts[None, :] > 0), ids[None, :], N_EXPERTS)
    next_expert = jnp.min(later, axis=1)
    next_expert = jnp.where(next_expert == N_EXPERTS, -1, next_expert)
    return (expert, block_xs.astype(I32), valid.astype(I32), n_used.astype(I32), next_expert.astype(I32),
            (blk_start * MOE_BM).astype(I32))


def _combine_kernel(n_main, n_tokens, dest_ref, hp_ref, hs_ref, gp_ref, gs_ref, nf_ref, yb_hbm, op_ref, os_ref,
                    *scratch):
    ybufs, sem = scratch[:-1], scratch[-1]
    i = pl.program_id(0)
    last = pl.num_programs(0) - 1
    ring = len(ybufs)
    ahead = ring - 1

    def row_copy(tile, sl, t, kk):
        row = pl.multiple_of(dest_ref[kk * n_tokens + tile * COMB_TT + t], SUBLANES)
        rows = t * SUBLANES if isinstance(t, int) else pl.multiple_of(t * SUBLANES, SUBLANES)
        return pltpu.make_async_copy(yb_hbm.at[pl.ds(row, SUBLANES), :],
                                     ybufs[sl].at[kk, pl.ds(rows, SUBLANES), :], sem.at[sl])

    def wait_tile(sl):
        for kk in range(TOP_K):
            pltpu.make_async_copy(yb_hbm.at[pl.ds(0, COMB_TT * SUBLANES), :], ybufs[sl].at[kk], sem.at[sl]).wait()

    @pl.when(i == 0)
    def _():
        def issue(g, carry):
            for uu in range(ISSUE_UNROLL):
                for kk in range(TOP_K):
                    for tile in range(ahead):
                        row_copy(jnp.minimum(tile, last), tile, g * ISSUE_UNROLL + uu, kk).start()
            return carry

        lax.fori_loop(0, COMB_TT // ISSUE_UNROLL, issue, 0)

    def step(cur, h_ref, gate_ref, o_ref):
        wait_tile(cur)
        later = jnp.minimum(i + ahead, last)
        for t in range(COMB_TT):
            for kk in range(TOP_K):
                row_copy(later, (cur + ahead) % ring, t, kk).start()
        pad = jnp.zeros((COMB_TT - SUBLANES, COMB_TT), F32)
        gates = jnp.concatenate([gate_ref[...], pad], axis=0).T
        y = gates[:, 0:1] * _rows_from_tiles(ybufs[cur].at[0], COMB_TT)
        for kk in range(1, TOP_K):
            y = y + gates[:, kk:kk + 1] * _rows_from_tiles(ybufs[cur].at[kk], COMB_TT)
        h = h_ref[...] + y
        o_ref[...] = (h * lax.rsqrt(jnp.mean(h * h, axis=-1, keepdims=True) + NORM_EPS)) * nf_ref[...]

    for cur in range(ring):
        @pl.when((i % ring == cur) & (i < n_main))
        def _(cur=cur):
            step(cur, hp_ref, gp_ref, op_ref)

        @pl.when((i % ring == cur) & (i >= n_main))
        def _(cur=cur):
            step(cur, hs_ref, gs_ref, os_ref)

    @pl.when(i == last)
    def _():
        for cur in range(ring):
            @pl.when(i % ring == cur)
            def _(cur=cur):
                for extra in range(1, ring):
                    wait_tile((cur + extra) % ring)


def _combine(dest, h_p, h_s, gate_p, gate_s, norm_f, yb):
    n_main = h_p.shape[0] // COMB_TT
    n_tail = h_s.shape[0] // COMB_TT
    main = lambda i, d: (jnp.minimum(i, n_main - 1), 0)
    tail = lambda i, d: (jnp.maximum(i - n_main, 0), 0)
    grid_spec = pltpu.PrefetchScalarGridSpec(
        num_scalar_prefetch=1,
        grid=(n_main + n_tail,),
        in_specs=[
            pl.BlockSpec((COMB_TT, D_MODEL), main),
            pl.BlockSpec((COMB_TT, D_MODEL), tail),
            pl.BlockSpec((SUBLANES, COMB_TT), lambda i, d: (0, jnp.minimum(i, n_main - 1))),
            pl.BlockSpec((SUBLANES, COMB_TT), lambda i, d: (0, jnp.maximum(i - n_main, 0))),
            pl.BlockSpec((1, D_MODEL), lambda i, d: (0, 0)),
            pl.BlockSpec(memory_space=pl.ANY),
        ],
        out_specs=[
            pl.BlockSpec((COMB_TT, D_MODEL), main),
            pl.BlockSpec((COMB_TT, D_MODEL), tail),
        ],
        scratch_shapes=[
            *[pltpu.VMEM((TOP_K, COMB_TT * SUBLANES, LANES), F32) for _ in range(COMB_RING)],
            pltpu.SemaphoreType.DMA((COMB_RING,)),
        ],
    )
    return pl.pallas_call(
        functools.partial(_combine_kernel, n_main, h_p.shape[0] + h_s.shape[0]),
        grid_spec=grid_spec,
        out_shape=[jax.ShapeDtypeStruct(h_p.shape, F32), jax.ShapeDtypeStruct(h_s.shape, F32)],
        compiler_params=_params(("arbitrary",)),
        name="combine",
    )(dest, h_p, h_s, gate_p, gate_s, norm_f, yb)


def kernel(x_prompt, x_sample, state_conv, state_ret, norm1, w_in, conv_w, w_a, w_b, w_o, norm2, w_router,
           b_router, w_e1, b_e1, w_e2, b_e2, norm_f):
    batch, seq_len, _ = x_prompt.shape
    n_dec, dec_seq, _ = x_sample.shape
    depth = norm1.shape[0]
    assert dec_seq == 1 and depth == 1, "single-token decode step of a one-layer trunk"
    past_len = 16384
    t_p = batch * seq_len
    n_tokens = t_p + n_dec
    region_rows = _expert_region_rows(n_tokens)
    lyr = 0
    xp = x_prompt.reshape(t_p, D_MODEL)
    xs = x_sample.reshape(n_dec, D_MODEL)
    g1 = norm1[lyr][None, :]
    w_in_b = w_in[lyr].astype(BF16)
    wa, wb, wo = w_a[lyr].astype(BF16), w_b[lyr].astype(BF16), w_o[lyr].astype(BF16)
    g2 = norm2[lyr][None, :]
    wr_t = w_router[lyr].T
    wr_hi = wr_t.astype(BF16)
    wr_split = jnp.concatenate([wr_hi, (wr_t - wr_hi.astype(F32)).astype(BF16)], axis=0)
    br_col = b_router[lyr][:, None]

    proj_p = _inproj(xp, g1, w_in_b, 1024, BF16)
    r_p, ret_p = _retention_prompt(proj_p, batch, seq_len)
    h_p, route_p, gate_p, conv_p, counts, dispatch = _merge(
        proj_p, r_p, xp, None, conv_w[lyr], wa, wb, wo, g2, wr_split, jnp.broadcast_to(br_col, (N_EXPERTS, MERGE_TM)),
        jnp.zeros((N_EXPERTS, LANES), F32), None, region_rows, MERGE_TM, seq_len // MERGE_TM)
    proj_s = _inproj(xs, g1, w_in_b, n_dec, F32)
    r_s, ret_s = _retention_sample(proj_s, state_ret[lyr], past_len)
    hist = (state_conv[lyr][:, 0, :], state_conv[lyr][:, 1, :])
    h_s, route_s, gate_s, u_s, counts, dispatch = _merge(
        proj_s, r_s, xs, hist, conv_w[lyr], wa, wb, wo, g2, wr_split, jnp.broadcast_to(br_col, (N_EXPERTS, n_dec)),
        counts, dispatch, region_rows, n_dec, 1)
    conv_s = jnp.stack([state_conv[lyr][:, 1, :], u_s], axis=1)

    counts_i = counts[:, 0].astype(I32)
    block_expert, block_xs, valid_rows, n_used, next_expert, row_start = _block_plan(
        counts_i, n_tokens, region_rows)
    yb = _moe_blocks(dispatch, block_expert, block_xs, valid_rows, n_used, next_expert,
                     w_e1[lyr], b_e1[lyr], w_e2[lyr], b_e2[lyr])
    route = jnp.concatenate([route_p, route_s], axis=1)
    expert_ids = jnp.arange(N_EXPERTS, dtype=I32)[:, None, None]
    first_row = jnp.sum(jnp.where(route[None, :TOP_K] == expert_ids, row_start[:, None, None], 0), axis=0)
    dest = ((first_row + route[TOP_K:]) * SUBLANES).reshape(TOP_K * n_tokens)
    y_p, y_s = _combine(dest, h_p, h_s, gate_p, gate_s, norm_f[None, :], yb)

    y_prompt = y_p.reshape(batch, seq_len, D_MODEL)
    y_sample = y_s.reshape(n_dec, 1, D_MODEL)
    return (y_prompt, y_sample, conv_p[None], ret_p[None], conv_s[None], ret_s[None])
```

```python
import functools

import jax
import jax.numpy as jnp
from jax import lax
from jax.experimental import pallas as pl
from jax.experimental.pallas import tpu as pltpu

F32 = jnp.float32
BF16 = jnp.bfloat16
I32 = jnp.int32

D_MODEL = 1024
CONV_W = 3
N_HEADS = 4
DK = 256
DV = 512
HALF = DK // 2
RET_QK = N_HEADS * DK
RET_V = N_HEADS * DV
ROPE_BASE = 10000.0
N_EXPERTS = 32
TOP_K = 4
D_FF = 1024
SWIGLU_ALPHA = 1.702
SWIGLU_LIMIT = 7.0
NORM_EPS = 1e-5
N_IN_COLS = 3 * D_MODEL + 2 * RET_QK + 2 * RET_V + 2 * D_MODEL
COL_CB, COL_CC, COL_CH = 0, D_MODEL, 2 * D_MODEL
COL_Q = 3 * D_MODEL
COL_K = COL_Q + RET_QK
COL_V = COL_K + RET_QK
COL_G = COL_V + RET_V
COL_GA = COL_G + RET_V
COL_GB = COL_GA + D_MODEL

RET_CHUNK = 256
MERGE_TM = 256
INPROJ_TM = 2048
INPROJ_TN = 1024
MOE_BM = 256
MOE_STEP_BLOCKS = 2
COMB_TT = 128
COMB_RING = 3
SAMPLE_GROUP = 8
LANES = 128
SUBLANES = 8
ISSUE_UNROLL = 8
VMEM_LIMIT = 56 * 1024 * 1024


def _params(sem):
    return pltpu.CompilerParams(dimension_semantics=sem, vmem_limit_bytes=VMEM_LIMIT)


def _expert_region_rows(n_tokens):
    return -(-n_tokens // MOE_BM) * MOE_BM


def _inproj_kernel(x_ref, g_ref, w_ref, o_ref, xn_ref):
    @pl.when(pl.program_id(1) == 0)
    def _():
        x = x_ref[...]
        ms = jnp.mean(x * x, axis=-1, keepdims=True)
        xn_ref[...] = ((x * lax.rsqrt(ms + NORM_EPS)) * g_ref[...]).astype(BF16)

    o_ref[...] = jnp.dot(xn_ref[...], w_ref[...], preferred_element_type=F32).astype(o_ref.dtype)


def _inproj(x, gain, w_bf16, tm, out_dtype):
    m = x.shape[0]
    n = w_bf16.shape[1]
    return pl.pallas_call(
        _inproj_kernel,
        grid=(m // tm, n // INPROJ_TN),
        in_specs=[
            pl.BlockSpec((tm, D_MODEL), lambda i, j: (i, 0)),
            pl.BlockSpec((1, D_MODEL), lambda i, j: (0, 0)),
            pl.BlockSpec((D_MODEL, INPROJ_TN), lambda i, j: (0, j)),
        ],
        out_specs=pl.BlockSpec((tm, INPROJ_TN), lambda i, j: (i, j)),
        out_shape=jax.ShapeDtypeStruct((m, n), out_dtype),
        scratch_shapes=[pltpu.VMEM((tm, D_MODEL), BF16)],
        compiler_params=_params(("arbitrary", "arbitrary")),
        name="inproj",
    )(x, gain, w_bf16)


def _rotary(x, cos, sin):
    x1 = x[:, :HALF]
    x2 = x[:, HALF:]
    return jnp.concatenate([x1 * cos - x2 * sin, x1 * sin + x2 * cos], axis=-1)


def _group_norm_gate(o, g):
    o = o * lax.rsqrt(jnp.mean(o * o, axis=-1, keepdims=True) + NORM_EPS)
    return (g * jax.nn.sigmoid(g)) * o


def _ret_chunk_kernel(q_ref, k_ref, v0_ref, v1_ref, g0_ref, g1_ref, cos_ref, sin_ref, mask_ref, qd_ref, kd_ref,
                      cd_ref, r_ref, s_ref):
    @pl.when(pl.program_id(1) == 0)
    def _():
        s_ref[...] = jnp.zeros_like(s_ref)

    cos = cos_ref[...]
    sin = sin_ref[...]
    v_refs = (v0_ref, v1_ref)
    g_refs = (g0_ref, g1_ref)
    for h in range(N_HEADS):
        half_cols = slice((h % 2) * DV, (h % 2 + 1) * DV)
        q = _rotary(q_ref[:, h * DK:(h + 1) * DK].astype(F32), cos, sin)
        k = _rotary(k_ref[:, h * DK:(h + 1) * DK].astype(F32), cos, sin) * (DK ** -0.5)
        v = v_refs[h // 2][:, half_cols]
        qd = jnp.concatenate([qd_ref[h], qd_ref[h]], axis=-1)
        kd = jnp.concatenate([kd_ref[h], kd_ref[h]], axis=-1)
        state = s_ref[0, h]
        scores = lax.dot_general(q.astype(BF16), k.astype(BF16), (((1,), (1,)), ((), ())),
                                 preferred_element_type=F32) * mask_ref[h]
        intra = jnp.dot(scores.astype(BF16), v, preferred_element_type=F32)
        cross = jnp.dot((q * qd).astype(BF16), state.astype(BF16), preferred_element_type=F32)
        kv = jnp.dot((k * kd).T.astype(BF16), v, preferred_element_type=F32)
        s_ref[0, h] = cd_ref[h] * state + kv
        g = g_refs[h // 2][:, half_cols].astype(F32)
        r_ref[:, h * DV:(h + 1) * DV] = _group_norm_gate(intra + cross, g).astype(r_ref.dtype)


def _retention_tables(chunk, seq_len, pos0):
    log_g = jnp.log1p(-jnp.exp2(-5.0 - jnp.arange(N_HEADS, dtype=F32)))
    pos = jnp.arange(chunk, dtype=F32)
    diff = pos[:, None] - pos[None, :]
    mask = jnp.where(diff >= 0, jnp.exp(jnp.maximum(diff, 0.0)[None] * log_g[:, None, None]), 0.0)
    q_decay = jnp.exp((pos + 1.0)[None, :] * log_g[:, None])[..., None]
    k_decay = jnp.exp((chunk - 1.0 - pos)[None, :] * log_g[:, None])[..., None]
    chunk_decay = jnp.exp(chunk * log_g)[:, None, None]
    qd = jnp.broadcast_to(q_decay, (N_HEADS, chunk, LANES))
    kd = jnp.broadcast_to(k_decay, (N_HEADS, chunk, LANES))
    cd = jnp.broadcast_to(chunk_decay, (N_HEADS, 1, DV))
    positions = pos0 + jnp.arange(seq_len, dtype=jnp.int32)
    inv = ROPE_BASE ** (-jnp.linspace(0.0, 1.0, HALF, dtype=F32))
    ang = positions.astype(F32)[:, None] * inv[None, :]
    return mask, qd, kd, cd, jnp.cos(ang), jnp.sin(ang)


def _retention_prompt(proj, batch, seq_len):
    c = RET_CHUNK
    nc = seq_len // c
    mask, qd, kd, cd, cos, sin = _retention_tables(c, seq_len, 0)
    row = lambda b, j: b * nc + j
    full3 = lambda b, j: (0, 0, 0)
    return pl.pallas_call(
        _ret_chunk_kernel,
        grid=(batch, nc),
        in_specs=[
            pl.BlockSpec((c, RET_QK), lambda b, j: (row(b, j), COL_Q // RET_QK)),
            pl.BlockSpec((c, RET_QK), lambda b, j: (row(b, j), COL_K // RET_QK)),
            pl.BlockSpec((c, RET_QK), lambda b, j: (row(b, j), COL_V // RET_QK)),
            pl.BlockSpec((c, RET_QK), lambda b, j: (row(b, j), COL_V // RET_QK + 1)),
            pl.BlockSpec((c, RET_QK), lambda b, j: (row(b, j), COL_G // RET_QK)),
            pl.BlockSpec((c, RET_QK), lambda b, j: (row(b, j), COL_G // RET_QK + 1)),
            pl.BlockSpec((c, HALF), lambda b, j: (j, 0)),
            pl.BlockSpec((c, HALF), lambda b, j: (j, 0)),
            pl.BlockSpec((N_HEADS, c, c), full3),
            pl.BlockSpec((N_HEADS, c, LANES), full3),
            pl.BlockSpec((N_HEADS, c, LANES), full3),
            pl.BlockSpec((N_HEADS, 1, DV), full3),
        ],
        out_specs=[
            pl.BlockSpec((c, RET_V), lambda b, j: (row(b, j), 0)),
            pl.BlockSpec((1, N_HEADS, DK, DV), lambda b, j: (b, 0, 0, 0)),
        ],
        out_shape=[
            jax.ShapeDtypeStruct((batch * seq_len, RET_V), BF16),
            jax.ShapeDtypeStruct((batch, N_HEADS, DK, DV), F32),
        ],
        compiler_params=_params(("arbitrary", "arbitrary")),
        name="retention_chunk",
    )(proj, proj, proj, proj, proj, proj, cos, sin, mask, qd, kd, cd)


def _ret_step_kernel(q_ref, k_ref, v_ref, g_ref, cos_ref, sin_ref, gam_ref, s_ref, r_ref, so_ref):
    cos = cos_ref[...]
    sin = sin_ref[...]
    gam = gam_ref[0]
    q = _rotary(q_ref[...], cos, sin)
    k = _rotary(k_ref[...], cos, sin) * (DK ** -0.5)
    v = v_ref[...]
    intra = jnp.sum(q * k, axis=-1, keepdims=True) * v
    q_t = (q * gam[:, :DK]).T
    k_t = k.T
    cross = []
    for s in range(SAMPLE_GROUP):
        state = s_ref[s, 0]
        cross.append(jnp.sum(q_t[:, s:s + 1] * state, axis=0, keepdims=True))
        so_ref[s, 0] = gam * state + k_t[:, s:s + 1] * v[s:s + 1, :]
    o = intra + jnp.concatenate(cross, axis=0)
    r_ref[...] = _group_norm_gate(o, g_ref[...])


def _retention_sample(proj, state, pos0):
    n = proj.shape[0]
    grp = SAMPLE_GROUP
    log_g = jnp.log1p(-jnp.exp2(-5.0 - jnp.arange(N_HEADS, dtype=F32)))
    gam = jnp.broadcast_to(jnp.exp(log_g)[:, None, None], (N_HEADS, 1, DV))
    _, _, _, _, cos, sin = _retention_tables(1, 1, pos0)
    return pl.pallas_call(
        _ret_step_kernel,
        grid=(n // grp, N_HEADS),
        in_specs=[
            pl.BlockSpec((grp, DK), lambda j, h: (j, COL_Q // DK + h)),
            pl.BlockSpec((grp, DK), lambda j, h: (j, COL_K // DK + h)),
            pl.BlockSpec((grp, DV), lambda j, h: (j, COL_V // DV + h)),
            pl.BlockSpec((grp, DV), lambda j, h: (j, COL_G // DV + h)),
            pl.BlockSpec((1, HALF), lambda j, h: (0, 0)),
            pl.BlockSpec((1, HALF), lambda j, h: (0, 0)),
            pl.BlockSpec((1, 1, DV), lambda j, h: (h, 0, 0)),
            pl.BlockSpec((grp, 1, DK, DV), lambda j, h: (j, h, 0, 0)),
        ],
        out_specs=[
            pl.BlockSpec((grp, DV), lambda j, h: (j, h)),
            pl.BlockSpec((grp, 1, DK, DV), lambda j, h: (j, h, 0, 0)),
        ],
        out_shape=[
            jax.ShapeDtypeStruct((n, RET_V), F32),
            jax.ShapeDtypeStruct(state.shape, F32),
        ],
        compiler_params=_params(("arbitrary", "arbitrary")),
        name="retention_step",
    )(proj, proj, proj, proj, cos, sin, gam, state)


def _merge_kernel(per_token_hist, chained, tiles_per_seq, region_rows, *refs):
    refs = list(refs)
    cb_ref, cc_ref, ch_ref, ga_ref, gb_ref, r_ref, x_ref = refs[:7]
    del refs[:7]
    if per_token_hist:
        h0_ref, h1_ref = refs[:2]
        del refs[:2]
    cw_ref, wa_ref, wb_ref, wo_ref, n2_ref, wr_ref, br_ref, cnt_in = refs[:8]
    del refs[:8]
    if chained:
        del refs[:1]
    h_out, route_out, gate_out, u_out, cnt_out, xs_out = refs[:6]
    del refs[:6]
    if not per_token_hist:
        hist_ref = refs.pop(0)
    cnt_ref, hn_st, dvm, dsm, sem = refs

    i = pl.program_id(0)
    last = pl.num_programs(0) - 1
    tm = x_ref.shape[0]
    cur = i % 2
    prev = 1 - cur

    def wait_scatters(p):
        for _ in range(TOP_K):
            pltpu.make_async_copy(hn_st.at[p], xs_out.at[pl.ds(0, tm * SUBLANES), :], sem.at[p]).wait()

    def dests_to_smem(p):
        return pltpu.make_async_copy(dvm, dsm.at[p], sem.at[2])

    def row_scatter(p, t, kk):
        rows = t * SUBLANES if isinstance(t, int) else pl.multiple_of(t * SUBLANES, SUBLANES)
        dst = xs_out.at[pl.ds(pl.multiple_of(dsm[p, kk, t], SUBLANES), SUBLANES), :]
        return pltpu.make_async_copy(hn_st.at[p, pl.ds(rows, SUBLANES), :], dst, sem.at[p])

    @pl.when(i == 0)
    def _():
        cnt_ref[...] = cnt_in[...]
        hn_st[1] = jnp.zeros((tm * SUBLANES, LANES), F32)
        spare = lax.broadcasted_iota(I32, (SUBLANES, tm), 0) * tm + lax.broadcasted_iota(I32, (SUBLANES, tm), 1)
        dvm[...] = (N_EXPERTS * region_rows + spare) * SUBLANES
        dests_to_smem(1).start()
        dests_to_smem(1).wait()

    if not per_token_hist:
        @pl.when(i % tiles_per_seq == 0)
        def _():
            hist_ref[...] = jnp.zeros_like(hist_ref)

    @pl.when(i > 0)
    def _():
        dests_to_smem(prev).wait()

    for t in range(tm):
        for kk in range(TOP_K):
            row_scatter(prev, t, kk).start()

    u = cc_ref[...].astype(F32) * ch_ref[...].astype(F32)
    if per_token_hist:
        u2 = h0_ref[...]
        u1 = h1_ref[...]
        u_out[...] = u
    else:
        hm2 = hist_ref[0:1, :]
        hm1 = hist_ref[1:2, :]
        row = lax.broadcasted_iota(I32, u.shape, 0)
        u1 = jnp.where(row == 0, hm1, pltpu.roll(u, 1, axis=0))
        u2 = jnp.where(row == 0, hm2, jnp.where(row == 1, hm1, pltpu.roll(u, 2, axis=0)))
        last2 = u[tm - (CONV_W - 1):, :]
        hist_ref[...] = last2
        u_out[0] = last2
    conv = cw_ref[0:1, :] * u2 + cw_ref[1:2, :] * u1 + cw_ref[2:3, :] * u
    z = (cb_ref[...].astype(F32) * conv).astype(BF16)
    o_a = jnp.dot(z, wa_ref[...], preferred_element_type=F32)
    o_b = jnp.dot(r_ref[...].astype(BF16), wb_ref[...], preferred_element_type=F32)
    merged = (jax.nn.sigmoid(ga_ref[...].astype(F32)) * o_a
              + jax.nn.sigmoid(gb_ref[...].astype(F32)) * o_b)
    m = jnp.dot(merged.astype(BF16), wo_ref[...], preferred_element_type=F32)
    h = x_ref[...] + m
    h_out[...] = h
    hn = (h * lax.rsqrt(jnp.mean(h * h, axis=-1, keepdims=True) + NORM_EPS)) * n2_ref[...]
    hn_hi = hn.astype(BF16)
    hn_lo = (hn - hn_hi.astype(F32)).astype(BF16)
    nt = (((1,), (1,)), ((), ()))
    part = lax.dot_general(wr_ref[...], hn_hi, nt, preferred_element_type=F32)
    logits = (part[:N_EXPERTS] + part[N_EXPERTS:] + br_ref[...]
              + lax.dot_general(wr_ref[:N_EXPERTS, :], hn_lo, nt, preferred_element_type=F32))
    expert = lax.broadcasted_iota(I32, logits.shape, 0)
    vals, idxs = [], []
    for _ in range(TOP_K):
        top = jnp.max(logits, axis=0, keepdims=True)
        arg = jnp.min(jnp.where(logits == top, expert, N_EXPERTS), axis=0, keepdims=True)
        vals.append(top)
        idxs.append(arg)
        logits = jnp.where(expert == arg, -jnp.inf, logits)
    exps = [jnp.exp(val - vals[0]) for val in vals]
    denom = exps[0] + exps[1] + exps[2] + exps[3]

    picks = [expert == idx for idx in idxs]
    chosen = jnp.zeros(logits.shape, F32)
    for pick in picks:
        chosen = jnp.where(pick, 1.0, chosen)
    before = (lax.broadcasted_iota(I32, (tm, tm), 0) < lax.broadcasted_iota(I32, (tm, tm), 1))
    prefix = jnp.dot(chosen.astype(BF16), jnp.where(before, 1.0, 0.0).astype(BF16),
                     preferred_element_type=F32)
    counts = cnt_ref[...]
    base = jnp.concatenate([counts] * (tm // LANES), axis=1) + prefix
    ranks = [jnp.sum(jnp.where(pick, base, 0.0), axis=0, keepdims=True).astype(I32) for pick in picks]
    counts = counts + jnp.sum(chosen, axis=1, keepdims=True)
    cnt_ref[...] = counts
    cnt_out[...] = counts

    slot_row = lax.broadcasted_iota(I32, (SUBLANES, tm), 0)
    route = jnp.zeros((SUBLANES, tm), I32)
    gates = jnp.zeros((SUBLANES, tm), F32)
    dests = jnp.zeros((SUBLANES, tm), I32)
    for kk in range(TOP_K):
        route = jnp.where(slot_row == kk, idxs[kk], route)
        route = jnp.where(slot_row == TOP_K + kk, ranks[kk], route)
        gates = jnp.where(slot_row == kk, exps[kk] / denom, gates)
        dests = jnp.where(slot_row == kk, (idxs[kk] * region_rows + ranks[kk]) * SUBLANES, dests)
    route_out[...] = route
    gate_out[...] = gates

    @pl.when(i > 0)
    def _():
        wait_scatters(cur)

    for c in range(D_MODEL // LANES):
        hn_st[cur, pl.ds(c, tm, stride=SUBLANES), :] = hn[:, c * LANES:(c + 1) * LANES]
    dvm[...] = dests
    dests_to_smem(cur).start()

    @pl.when(i == last)
    def _():
        dests_to_smem(cur).wait()
        wait_scatters(prev)

        def issue(g, carry):
            for uu in range(ISSUE_UNROLL):
                for kk in range(TOP_K):
                    row_scatter(cur, g * ISSUE_UNROLL + uu, kk).start()
            return carry

        lax.fori_loop(0, tm // ISSUE_UNROLL, issue, 0)
        wait_scatters(cur)


def _merge(proj, r, x, hist, conv_w, wa, wb, wo, norm2, w_router, b_router, counts_in, xs_prev, region_rows,
           tm, tiles_per_seq):
    t = x.shape[0]
    per_token = hist is not None
    chained = xs_prev is not None
    col = lambda c: (lambda i: (i, c // D_MODEL))
    const2 = lambda i: (0, 0)
    tile = pl.BlockSpec((tm, D_MODEL), lambda i: (i, 0))
    in_specs = [
        pl.BlockSpec((tm, D_MODEL), col(COL_CB)),
        pl.BlockSpec((tm, D_MODEL), col(COL_CC)),
        pl.BlockSpec((tm, D_MODEL), col(COL_CH)),
        pl.BlockSpec((tm, D_MODEL), col(COL_GA)),
        pl.BlockSpec((tm, D_MODEL), col(COL_GB)),
        pl.BlockSpec((tm, RET_V), lambda i: (i, 0)),
        tile,
    ]
    args = [proj, proj, proj, proj, proj, r, x]
    if per_token:
        in_specs += [tile, tile]
        args += list(hist)
    in_specs += [
        pl.BlockSpec((CONV_W, D_MODEL), const2),
        pl.BlockSpec((D_MODEL, D_MODEL), const2),
        pl.BlockSpec((RET_V, D_MODEL), const2),
        pl.BlockSpec((D_MODEL, D_MODEL), const2),
        pl.BlockSpec((1, D_MODEL), const2),
        pl.BlockSpec((2 * N_EXPERTS, D_MODEL), const2),
        pl.BlockSpec((N_EXPERTS, tm), const2),
        pl.BlockSpec((N_EXPERTS, LANES), const2),
    ]
    args += [conv_w, wa, wb, wo, norm2, w_router, b_router, counts_in]
    aliases = {}
    if chained:
        aliases = {len(args): 5}
        in_specs.append(pl.BlockSpec(memory_space=pl.ANY))
        args.append(xs_prev)
    lanes_tile = pl.BlockSpec((SUBLANES, tm), lambda i: (0, i))
    if per_token:
        u_spec = tile
        u_shape = jax.ShapeDtypeStruct((t, D_MODEL), F32)
        scratch = []
    else:
        n_seq = t // (tm * tiles_per_seq)
        u_spec = pl.BlockSpec((1, CONV_W - 1, D_MODEL), lambda i: (i // tiles_per_seq, 0, 0))
        u_shape = jax.ShapeDtypeStruct((n_seq, CONV_W - 1, D_MODEL), F32)
        scratch = [pltpu.VMEM((CONV_W - 1, D_MODEL), F32)]
    out_specs = [tile, lanes_tile, lanes_tile, u_spec, pl.BlockSpec((N_EXPERTS, LANES), const2),
                 pl.BlockSpec(memory_space=pl.ANY)]
    out_shape = [
        jax.ShapeDtypeStruct((t, D_MODEL), F32),
        jax.ShapeDtypeStruct((SUBLANES, t), I32),
        jax.ShapeDtypeStruct((SUBLANES, t), F32),
        u_shape,
        jax.ShapeDtypeStruct((N_EXPERTS, LANES), F32),
        jax.ShapeDtypeStruct(((N_EXPERTS * region_rows + TOP_K * MERGE_TM) * SUBLANES, LANES), F32),
    ]
    scratch += [
        pltpu.VMEM((N_EXPERTS, LANES), F32),
        pltpu.VMEM((2, tm * SUBLANES, LANES), F32),
        pltpu.VMEM((SUBLANES, tm), I32),
        pltpu.SMEM((2, SUBLANES, tm), I32),
        pltpu.SemaphoreType.DMA((3,)),
    ]
    return pl.pallas_call(
        functools.partial(_merge_kernel, per_token, chained, tiles_per_seq, region_rows),
        grid=(t // tm,),
        in_specs=in_specs,
        out_specs=out_specs,
        out_shape=out_shape,
        scratch_shapes=scratch,
        input_output_aliases=aliases,
        compiler_params=_params(("arbitrary",)),
        name="merge_sample" if per_token else "merge_prompt",
    )(*args)


def _rows_from_tiles(ref, n_rows):
    return jnp.concatenate([ref[pl.ds(c, n_rows, stride=SUBLANES), :] for c in range(D_MODEL // LANES)], axis=1)


def _rows_to_tiles(ref, rows):
    for c in range(D_MODEL // LANES):
        ref[pl.ds(c, rows.shape[0], stride=SUBLANES), :] = rows[:, c * LANES:(c + 1) * LANES]


def _moe_kernel(be_ref, bx_ref, vr_ref, nused_ref, next_ref, *refs):
    x_refs = refs[:MOE_STEP_BLOCKS]
    w1_hbm, b1_ref, w2_hbm, b2_ref, o_ref, w1f, w2f, w1b, w2b, sem = refs[MOE_STEP_BLOCKS:]
    n_used = nused_ref[0]

    def weight_copies(e):
        return (pltpu.make_async_copy(w1_hbm.at[e], w1f, sem.at[0]),
                pltpu.make_async_copy(w2_hbm.at[e], w2f, sem.at[1]))

    def block(i, x_ref, o_view):
        @pl.when(i < n_used)
        def _():
            e = be_ref[i]
            prev = be_ref[jnp.maximum(i - 1, 0)]

            @pl.when(i == 0)
            def _():
                for cp in weight_copies(e):
                    cp.start()

            @pl.when((i == 0) | (e != prev))
            def _():
                for cp in weight_copies(e):
                    cp.wait()
                w1b[...] = w1f[...].astype(BF16)
                w2b[...] = w2f[...].astype(BF16)
                nxt = next_ref[e]

                @pl.when(nxt >= 0)
                def _():
                    for cp in weight_copies(nxt):
                        cp.start()

            live = lax.broadcasted_iota(I32, (MOE_BM, 1), 0) < vr_ref[i]
            x = jnp.where(live, _rows_from_tiles(x_ref, MOE_BM), 0.0).astype(BF16)
            h1 = jnp.dot(x, w1b[...], preferred_element_type=F32) + b1_ref[e]
            gate = jnp.minimum(h1[:, :D_FF], SWIGLU_LIMIT)
            up = jnp.clip(h1[:, D_FF:], -SWIGLU_LIMIT, SWIGLU_LIMIT)
            glu = gate * jax.nn.sigmoid(SWIGLU_ALPHA * gate)
            act = ((up + 1.0) * glu).astype(BF16)
            _rows_to_tiles(o_view, jnp.dot(act, w2b[...], preferred_element_type=F32) + b2_ref[e])

        @pl.when(i >= n_used)
        def _():
            o_view[...] = jnp.zeros(o_view.shape, F32)

    for sub in range(MOE_STEP_BLOCKS):
        rows = MOE_BM * SUBLANES
        block(pl.program_id(0) * MOE_STEP_BLOCKS + sub, x_refs[sub], o_ref.at[pl.ds(sub * rows, rows), :])


def _moe_blocks(xs, block_expert, block_xs, valid_rows, n_used, next_expert, w1, b1, w2, b2):
    n_blocks = block_expert.shape[0]
    rows = MOE_BM * SUBLANES
    x_spec = lambda sub: pl.BlockSpec(
        (rows, LANES), lambda i, be, bx, vr, nu, nx: (bx[i * MOE_STEP_BLOCKS + sub], 0))
    whole3 = lambda i, be, bx, vr, nu, nx: (0, 0, 0)
    grid_spec = pltpu.PrefetchScalarGridSpec(
        num_scalar_prefetch=5,
        grid=(n_blocks // MOE_STEP_BLOCKS,),
        in_specs=[x_spec(sub) for sub in range(MOE_STEP_BLOCKS)] + [
            pl.BlockSpec(memory_space=pl.ANY),
            pl.BlockSpec((N_EXPERTS, 1, 2 * D_FF), whole3),
            pl.BlockSpec(memory_space=pl.ANY),
            pl.BlockSpec((N_EXPERTS, 1, D_MODEL), whole3),
        ],
        out_specs=pl.BlockSpec((MOE_STEP_BLOCKS * rows, LANES), lambda i, be, bx, vr, nu, nx: (i, 0)),
        scratch_shapes=[
            pltpu.VMEM((D_MODEL, 2 * D_FF), F32),
            pltpu.VMEM((D_FF, D_MODEL), F32),
            pltpu.VMEM((D_MODEL, 2 * D_FF), BF16),
            pltpu.VMEM((D_FF, D_MODEL), BF16),
            pltpu.SemaphoreType.DMA((2,)),
        ],
    )
    return pl.pallas_call(
        _moe_kernel,
        grid_spec=grid_spec,
        out_shape=jax.ShapeDtypeStruct((n_blocks * rows, LANES), F32),
        compiler_params=_params(("arbitrary",)),
        name="moe_experts",
    )(block_expert, block_xs, valid_rows, n_used, next_expert, *([xs] * MOE_STEP_BLOCKS),
      w1, b1[:, None, :], w2, b2[:, None, :])


def _block_plan(counts, n_tokens, region_rows):
    n_blocks = -(-n_tokens * TOP_K // MOE_BM) + N_EXPERTS
    n_blocks = -(-n_blocks // MOE_STEP_BLOCKS) * MOE_STEP_BLOCKS
    blocks_e = (counts + MOE_BM - 1) // MOE_BM
    blk_end = jnp.cumsum(blocks_e)
    blk_start = blk_end - blocks_e
    n_used = blk_end[-1:]
    blk = jnp.maximum(jnp.minimum(jnp.arange(n_blocks, dtype=I32), n_used - 1), 0)
    expert = jnp.minimum(jnp.sum((blk_end[None, :] <= blk[:, None]).astype(I32), axis=1), N_EXPERTS - 1)
    within = blk - blk_start[expert]
    block_xs = expert * (region_rows // MOE_BM) + within
    valid = jnp.clip(counts[expert] - within * MOE_BM, 0, MOE_BM)
    ids = jnp.arange(N_EXPERTS, dtype=I32)
    later = jnp.where((ids[None, :] > ids[:, None]) & (counts[None, :] > 0), ids[None, :], N_EXPERTS)
    next_expert = jnp.min(later, axis=1)
    next_expert = jnp.where(next_expert == N_EXPERTS, -1, next_expert)
    return (expert, block_xs.astype(I32), valid.astype(I32), n_used.astype(I32), next_expert.astype(I32),
            (blk_start * MOE_BM).astype(I32))


def _combine_kernel(n_main, n_tokens, dest_ref, hp_ref, hs_ref, gp_ref, gs_ref, nf_ref, yb_hbm, op_ref, os_ref,
                    *scratch):
    ybufs, sem = scratch[:-1], scratch[-1]
    i = pl.program_id(0)
    last = pl.num_programs(0) - 1
    ring = len(ybufs)
    ahead = ring - 1

    def row_copy(tile, sl, t, kk):
        row = pl.multiple_of(dest_ref[kk * n_tokens + tile * COMB_TT + t], SUBLANES)
        rows = t * SUBLANES if isinstance(t, int) else pl.multiple_of(t * SUBLANES, SUBLANES)
        return pltpu.make_async_copy(yb_hbm.at[pl.ds(row, SUBLANES), :],
                                     ybufs[sl].at[kk, pl.ds(rows, SUBLANES), :], sem.at[sl])

    def wait_tile(sl):
        for kk in range(TOP_K):
            pltpu.make_async_copy(yb_hbm.at[pl.ds(0, COMB_TT * SUBLANES), :], ybufs[sl].at[kk], sem.at[sl]).wait()

    @pl.when(i == 0)
    def _():
        def issue(g, carry):
            for uu in range(ISSUE_UNROLL):
                for kk in range(TOP_K):
                    for tile in range(ahead):
                        row_copy(jnp.minimum(tile, last), tile, g * ISSUE_UNROLL + uu, kk).start()
            return carry

        lax.fori_loop(0, COMB_TT // ISSUE_UNROLL, issue, 0)

    def step(cur, h_ref, gate_ref, o_ref):
        wait_tile(cur)
        later = jnp.minimum(i + ahead, last)
        for t in range(COMB_TT):
            for kk in range(TOP_K):
                row_copy(later, (cur + ahead) % ring, t, kk).start()
        pad = jnp.zeros((COMB_TT - SUBLANES, COMB_TT), F32)
        gates = jnp.concatenate([gate_ref[...], pad], axis=0).T
        y = gates[:, 0:1] * _rows_from_tiles(ybufs[cur].at[0], COMB_TT)
        for kk in range(1, TOP_K):
            y = y + gates[:, kk:kk + 1] * _rows_from_tiles(ybufs[cur].at[kk], COMB_TT)
        h = h_ref[...] + y
        o_ref[...] = (h * lax.rsqrt(jnp.mean(h * h, axis=-1, keepdims=True) + NORM_EPS)) * nf_ref[...]

    for cur in range(ring):
        @pl.when((i % ring == cur) & (i < n_main))
        def _(cur=cur):
            step(cur, hp_ref, gp_ref, op_ref)

        @pl.when((i % ring == cur) & (i >= n_main))
        def _(cur=cur):
            step(cur, hs_ref, gs_ref, os_ref)

    @pl.when(i == last)
    def _():
        for cur in range(ring):
            @pl.when(i % ring == cur)
            def _(cur=cur):
                for extra in range(1, ring):
                    wait_tile((cur + extra) % ring)


def _combine(dest, h_p, h_s, gate_p, gate_s, norm_f, yb):
    n_main = h_p.shape[0] // COMB_TT
    n_tail = h_s.shape[0] // COMB_TT
    main = lambda i, d: (jnp.minimum(i, n_main - 1), 0)
    tail = lambda i, d: (jnp.maximum(i - n_main, 0), 0)
    grid_spec = pltpu.PrefetchScalarGridSpec(
        num_scalar_prefetch=1,
        grid=(n_main + n_tail,),
        in_specs=[
            pl.BlockSpec((COMB_TT, D_MODEL), main),
            pl.BlockSpec((COMB_TT, D_MODEL), tail),
            pl.BlockSpec((SUBLANES, COMB_TT), lambda i, d: (0, jnp.minimum(i, n_main - 1))),
            pl.BlockSpec((SUBLANES, COMB_TT), lambda i, d: (0, jnp.maximum(i - n_main, 0))),
            pl.BlockSpec((1, D_MODEL), lambda i, d: (0, 0)),
            pl.BlockSpec(memory_space=pl.ANY),
        ],
        out_specs=[
            pl.BlockSpec((COMB_TT, D_MODEL), main),
            pl.BlockSpec((COMB_TT, D_MODEL), tail),
        ],
        scratch_shapes=[
            *[pltpu.VMEM((TOP_K, COMB_TT * SUBLANES, LANES), F32) for _ in range(COMB_RING)],
            pltpu.SemaphoreType.DMA((COMB_RING,)),
        ],
    )
    return pl.pallas_call(
        functools.partial(_combine_kernel, n_main, h_p.shape[0] + h_s.shape[0]),
        grid_spec=grid_spec,
        out_shape=[jax.ShapeDtypeStruct(h_p.shape, F32), jax.ShapeDtypeStruct(h_s.shape, F32)],
        compiler_params=_params(("arbitrary",)),
        name="combine",
    )(dest, h_p, h_s, gate_p, gate_s, norm_f, yb)


def kernel(x_prompt, x_sample, state_conv, state_ret, norm1, w_in, conv_w, w_a, w_b, w_o, norm2, w_router,
           b_router, w_e1, b_e1, w_e2, b_e2, norm_f):
    batch, seq_len, _ = x_prompt.shape
    n_dec, dec_seq, _ = x_sample.shape
    depth = norm1.shape[0]
    assert dec_seq == 1 and depth == 1, "single-token decode step of a one-layer trunk"
    past_len = 16384
    t_p = batch * seq_len
    n_tokens = t_p + n_dec
    region_rows = _expert_region_rows(n_tokens)
    lyr = 0
    xp = x_prompt.reshape(t_p, D_MODEL)
    xs = x_sample.reshape(n_dec, D_MODEL)
    g1 = norm1[lyr][None, :]
    w_in_b = w_in[lyr].astype(BF16)
    wa, wb, wo = w_a[lyr].astype(BF16), w_b[lyr].astype(BF16), w_o[lyr].astype(BF16)
    g2 = norm2[lyr][None, :]
    wr_t = w_router[lyr].T
    wr_hi = wr_t.astype(BF16)
    wr_split = jnp.concatenate([wr_hi, (wr_t - wr_hi.astype(F32)).astype(BF16)], axis=0)
    br_col = b_router[lyr][:, None]

    proj_p = _inproj(xp, g1, w_in_b, INPROJ_TM, BF16)
    r_p, ret_p = _retention_prompt(proj_p, batch, seq_len)
    h_p, route_p, gate_p, conv_p, counts, dispatch = _merge(
        proj_p, r_p, xp, None, conv_w[lyr], wa, wb, wo, g2, wr_split, jnp.broadcast_to(br_col, (N_EXPERTS, MERGE_TM)),
        jnp.zeros((N_EXPERTS, LANES), F32), None, region_rows, MERGE_TM, seq_len // MERGE_TM)
    proj_s = _inproj(xs, g1, w_in_b, n_dec, F32)
    r_s, ret_s = _retention_sample(proj_s, state_ret[lyr], past_len)
    hist = (state_conv[lyr][:, 0, :], state_conv[lyr][:, 1, :])
    h_s, route_s, gate_s, u_s, counts, dispatch = _merge(
        proj_s, r_s, xs, hist, conv_w[lyr], wa, wb, wo, g2, wr_split, jnp.broadcast_to(br_col, (N_EXPERTS, n_dec)),
        counts, dispatch, region_rows, n_dec, 1)
    conv_s = jnp.stack([state_conv[lyr][:, 1, :], u_s], axis=1)

    counts_i = counts[:, 0].astype(I32)
    block_expert, block_xs, valid_rows, n_used, next_expert, row_start = _block_plan(
        counts_i, n_tokens, region_rows)
    yb = _moe_blocks(dispatch, block_expert, block_xs, valid_rows, n_used, next_expert,
                     w_e1[lyr], b_e1[lyr], w_e2[lyr], b_e2[lyr])
    route = jnp.concatenate([route_p, route_s], axis=1)
    expert_ids = jnp.arange(N_EXPERTS, dtype=I32)[:, None, None]
    first_row = jnp.sum(jnp.where(route[None, :TOP_K] == expert_ids, row_start[:, None, None], 0), axis=0)
    dest = ((first_row + route[TOP_K:]) * SUBLANES).reshape(TOP_K * n_tokens)
    y_p, y_s = _combine(dest, h_p, h_s, gate_p, gate_s, norm_f[None, :], yb)

    y_prompt = y_p.reshape(batch, seq_len, D_MODEL)
    y_sample = y_s.reshape(n_dec, 1, D_MODEL)
    return (y_prompt, y_sample, conv_p[None], ret_p[None], conv_s[None], ret_s[None])
```

```python
import functools

import jax
import jax.numpy as jnp
from jax import lax
from jax.experimental import pallas as pl
from jax.experimental.pallas import tpu as pltpu

F32 = jnp.float32
BF16 = jnp.bfloat16
I32 = jnp.int32

D_MODEL = 1024
CONV_W = 3
N_HEADS = 4
DK = 256
DV = 512
HALF = DK // 2
RET_QK = N_HEADS * DK
RET_V = N_HEADS * DV
ROPE_BASE = 10000.0
N_EXPERTS = 32
TOP_K = 4
D_FF = 1024
SWIGLU_ALPHA = 1.702
SWIGLU_LIMIT = 7.0
NORM_EPS = 1e-5
N_IN_COLS = 3 * D_MODEL + 2 * RET_QK + 2 * RET_V + 2 * D_MODEL
COL_CB, COL_CC, COL_CH = 0, D_MODEL, 2 * D_MODEL
COL_Q = 3 * D_MODEL
COL_K = COL_Q + RET_QK
COL_V = COL_K + RET_QK
COL_G = COL_V + RET_V
COL_GA = COL_G + RET_V
COL_GB = COL_GA + D_MODEL

RET_CHUNK = 256
MERGE_TM = 256
INPROJ_TM = 2048
INPROJ_TN = 1024
MOE_BM = 256
MOE_STEP_BLOCKS = 4
COMB_TT = 128
COMB_RING = 3
SAMPLE_GROUP = 16
LANES = 128
SUBLANES = 8
ISSUE_UNROLL = 8
VMEM_LIMIT = 56 * 1024 * 1024


def _params(sem):
    return pltpu.CompilerParams(dimension_semantics=sem, vmem_limit_bytes=VMEM_LIMIT)


def _expert_region_rows(n_tokens):
    return -(-n_tokens // MOE_BM) * MOE_BM


def _inproj_kernel(x_ref, g_ref, w_ref, o_ref, xn_ref):
    @pl.when(pl.program_id(1) == 0)
    def _():
        x = x_ref[...]
        ms = jnp.mean(x * x, axis=-1, keepdims=True)
        xn_ref[...] = ((x * lax.rsqrt(ms + NORM_EPS)) * g_ref[...]).astype(BF16)

    o_ref[...] = jnp.dot(xn_ref[...], w_ref[...], preferred_element_type=F32).astype(o_ref.dtype)


def _inproj(x, gain, w_bf16, tm, out_dtype):
    m = x.shape[0]
    n = w_bf16.shape[1]
    return pl.pallas_call(
        _inproj_kernel,
        grid=(m // tm, n // INPROJ_TN),
        in_specs=[
            pl.BlockSpec((tm, D_MODEL), lambda i, j: (i, 0)),
            pl.BlockSpec((1, D_MODEL), lambda i, j: (0, 0)),
            pl.BlockSpec((D_MODEL, INPROJ_TN), lambda i, j: (0, j)),
        ],
        out_specs=pl.BlockSpec((tm, INPROJ_TN), lambda i, j: (i, j)),
        out_shape=jax.ShapeDtypeStruct((m, n), out_dtype),
        scratch_shapes=[pltpu.VMEM((tm, D_MODEL), BF16)],
        compiler_params=_params(("arbitrary", "arbitrary")),
        name="inproj",
    )(x, gain, w_bf16)


def _rotary(x, cos, sin):
    x1 = x[:, :HALF]
    x2 = x[:, HALF:]
    return jnp.concatenate([x1 * cos - x2 * sin, x1 * sin + x2 * cos], axis=-1)


def _group_norm_gate(o, g):
    o = o * lax.rsqrt(jnp.mean(o * o, axis=-1, keepdims=True) + NORM_EPS)
    return (g * jax.nn.sigmoid(g)) * o


def _ret_chunk_kernel(q_ref, k_ref, v0_ref, v1_ref, g0_ref, g1_ref, cos_ref, sin_ref, mask_ref, qd_ref, kd_ref,
                      cd_ref, r_ref, s_ref):
    @pl.when(pl.program_id(1) == 0)
    def _():
        s_ref[...] = jnp.zeros_like(s_ref)

    cos = cos_ref[...]
    sin = sin_ref[...]
    v_refs = (v0_ref, v1_ref)
    g_refs = (g0_ref, g1_ref)
    for h in range(N_HEADS):
        half_cols = slice((h % 2) * DV, (h % 2 + 1) * DV)
        q = _rotary(q_ref[:, h * DK:(h + 1) * DK].astype(F32), cos, sin)
        k = _rotary(k_ref[:, h * DK:(h + 1) * DK].astype(F32), cos, sin) * (DK ** -0.5)
        v = v_refs[h // 2][:, half_cols]
        qd = jnp.concatenate([qd_ref[h], qd_ref[h]], axis=-1)
        kd = jnp.concatenate([kd_ref[h], kd_ref[h]], axis=-1)
        state = s_ref[0, h]
        scores = lax.dot_general(q.astype(BF16), k.astype(BF16), (((1,), (1,)), ((), ())),
                                 preferred_element_type=F32) * mask_ref[h]
        intra = jnp.dot(scores.astype(BF16), v, preferred_element_type=F32)
        cross = jnp.dot((q * qd).astype(BF16), state.astype(BF16), preferred_element_type=F32)
        kv = jnp.dot((k * kd).T.astype(BF16), v, preferred_element_type=F32)
        s_ref[0, h] = cd_ref[h] * state + kv
        g = g_refs[h // 2][:, half_cols].astype(F32)
        r_ref[:, h * DV:(h + 1) * DV] = _group_norm_gate(intra + cross, g).astype(r_ref.dtype)


def _retention_tables(chunk, seq_len, pos0):
    log_g = jnp.log1p(-jnp.exp2(-5.0 - jnp.arange(N_HEADS, dtype=F32)))
    pos = jnp.arange(chunk, dtype=F32)
    diff = pos[:, None] - pos[None, :]
    mask = jnp.where(diff >= 0, jnp.exp(jnp.maximum(diff, 0.0)[None] * log_g[:, None, None]), 0.0)
    q_decay = jnp.exp((pos + 1.0)[None, :] * log_g[:, None])[..., None]
    k_decay = jnp.exp((chunk - 1.0 - pos)[None, :] * log_g[:, None])[..., None]
    chunk_decay = jnp.exp(chunk * log_g)[:, None, None]
    qd = jnp.broadcast_to(q_decay, (N_HEADS, chunk, LANES))
    kd = jnp.broadcast_to(k_decay, (N_HEADS, chunk, LANES))
    cd = jnp.broadcast_to(chunk_decay, (N_HEADS, 1, DV))
    positions = pos0 + jnp.arange(seq_len, dtype=jnp.int32)
    inv = ROPE_BASE ** (-jnp.linspace(0.0, 1.0, HALF, dtype=F32))
    ang = positions.astype(F32)[:, None] * inv[None, :]
    return mask, qd, kd, cd, jnp.cos(ang), jnp.sin(ang)


def _retention_prompt(proj, batch, seq_len):
    c = RET_CHUNK
    nc = seq_len // c
    mask, qd, kd, cd, cos, sin = _retention_tables(c, seq_len, 0)
    row = lambda b, j: b * nc + j
    full3 = lambda b, j: (0, 0, 0)
    return pl.pallas_call(
        _ret_chunk_kernel,
        grid=(batch, nc),
        in_specs=[
            pl.BlockSpec((c, RET_QK), lambda b, j: (row(b, j), COL_Q // RET_QK)),
            pl.BlockSpec((c, RET_QK), lambda b, j: (row(b, j), COL_K // RET_QK)),
            pl.BlockSpec((c, RET_QK), lambda b, j: (row(b, j), COL_V // RET_QK)),
            pl.BlockSpec((c, RET_QK), lambda b, j: (row(b, j), COL_V // RET_QK + 1)),
            pl.BlockSpec((c, RET_QK), lambda b, j: (row(b, j), COL_G // RET_QK)),
            pl.BlockSpec((c, RET_QK), lambda b, j: (row(b, j), COL_G // RET_QK + 1)),
            pl.BlockSpec((c, HALF), lambda b, j: (j, 0)),
            pl.BlockSpec((c, HALF), lambda b, j: (j, 0)),
            pl.BlockSpec((N_HEADS, c, c), full3),
            pl.BlockSpec((N_HEADS, c, LANES), full3),
            pl.BlockSpec((N_HEADS, c, LANES), full3),
            pl.BlockSpec((N_HEADS, 1, DV), full3),
        ],
        out_specs=[
            pl.BlockSpec((c, RET_V), lambda b, j: (row(b, j), 0)),
            pl.BlockSpec((1, N_HEADS, DK, DV), lambda b, j: (b, 0, 0, 0)),
        ],
        out_shape=[
            jax.ShapeDtypeStruct((batch * seq_len, RET_V), BF16),
            jax.ShapeDtypeStruct((batch, N_HEADS, DK, DV), F32),
        ],
        compiler_params=_params(("arbitrary", "arbitrary")),
        name="retention_chunk",
    )(proj, proj, proj, proj, proj, proj, cos, sin, mask, qd, kd, cd)


def _ret_step_kernel(q_ref, k_ref, v_ref, g_ref, cos_ref, sin_ref, gam_ref, s_ref, r_ref, so_ref):
    cos = cos_ref[...]
    sin = sin_ref[...]
    gam = gam_ref[0]
    q = _rotary(q_ref[...], cos, sin)
    k = _rotary(k_ref[...], cos, sin) * (DK ** -0.5)
    v = v_ref[...]
    intra = jnp.sum(q * k, axis=-1, keepdims=True) * v
    q_t = (q * gam[:, :DK]).T
    k_t = k.T
    cross = []
    for s in range(SAMPLE_GROUP):
        state = s_ref[s, 0]
        cross.append(jnp.sum(q_t[:, s:s + 1] * state, axis=0, keepdims=True))
        so_ref[s, 0] = gam * state + k_t[:, s:s + 1] * v[s:s + 1, :]
    o = intra + jnp.concatenate(cross, axis=0)
    r_ref[...] = _group_norm_gate(o, g_ref[...])


def _retention_sample(proj, state, pos0):
    n = proj.shape[0]
    grp = SAMPLE_GROUP
    log_g = jnp.log1p(-jnp.exp2(-5.0 - jnp.arange(N_HEADS, dtype=F32)))
    gam = jnp.broadcast_to(jnp.exp(log_g)[:, None, None], (N_HEADS, 1, DV))
    _, _, _, _, cos, sin = _retention_tables(1, 1, pos0)
    return pl.pallas_call(
        _ret_step_kernel,
        grid=(n // grp, N_HEADS),
        in_specs=[
            pl.BlockSpec((grp, DK), lambda j, h: (j, COL_Q // DK + h)),
            pl.BlockSpec((grp, DK), lambda j, h: (j, COL_K // DK + h)),
            pl.BlockSpec((grp, DV), lambda j, h: (j, COL_V // DV + h)),
            pl.BlockSpec((grp, DV), lambda j, h: (j, COL_G // DV + h)),
            pl.BlockSpec((1, HALF), lambda j, h: (0, 0)),
            pl.BlockSpec((1, HALF), lambda j, h: (0, 0)),
            pl.BlockSpec((1, 1, DV), lambda j, h: (h, 0, 0)),
            pl.BlockSpec((grp, 1, DK, DV), lambda j, h: (j, h, 0, 0)),
        ],
        out_specs=[
            pl.BlockSpec((grp, DV), lambda j, h: (j, h)),
            pl.BlockSpec((grp, 1, DK, DV), lambda j, h: (j, h, 0, 0)),
        ],
        out_shape=[
            jax.ShapeDtypeStruct((n, RET_V), F32),
            jax.ShapeDtypeStruct(state.shape, F32),
        ],
        compiler_params=_params(("arbitrary", "arbitrary")),
        name="retention_step",
    )(proj, proj, proj, proj, cos, sin, gam, state)


def _merge_kernel(per_token_hist, chained, tiles_per_seq, region_rows, *refs):
    refs = list(refs)
    cb_ref, cc_ref, ch_ref, ga_ref, gb_ref, r_ref, x_ref = refs[:7]
    del refs[:7]
    if per_token_hist:
        h0_ref, h1_ref = refs[:2]
        del refs[:2]
    cw_ref, wa_ref, wb_ref, wo_ref, n2_ref, wr_ref, br_ref, cnt_in = refs[:8]
    del refs[:8]
    if chained:
        del refs[:1]
    h_out, route_out, gate_out, u_out, cnt_out, xs_out = refs[:6]
    del refs[:6]
    if not per_token_hist:
        hist_ref = refs.pop(0)
    cnt_ref, hn_st, dvm, dsm, sem = refs

    i = pl.program_id(0)
    last = pl.num_programs(0) - 1
    tm = x_ref.shape[0]
    cur = i % 2
    prev = 1 - cur

    def wait_scatters(p):
        for _ in range(TOP_K):
            pltpu.make_async_copy(hn_st.at[p], xs_out.at[pl.ds(0, tm * SUBLANES), :], sem.at[p]).wait()

    def dests_to_smem(p):
        return pltpu.make_async_copy(dvm, dsm.at[p], sem.at[2])

    def row_scatter(p, t, kk):
        rows = t * SUBLANES if isinstance(t, int) else pl.multiple_of(t * SUBLANES, SUBLANES)
        dst = xs_out.at[pl.ds(pl.multiple_of(dsm[p, kk, t], SUBLANES), SUBLANES), :]
        return pltpu.make_async_copy(hn_st.at[p, pl.ds(rows, SUBLANES), :], dst, sem.at[p])

    @pl.when(i == 0)
    def _():
        cnt_ref[...] = cnt_in[...]
        hn_st[1] = jnp.zeros((tm * SUBLANES, LANES), F32)
        spare = lax.broadcasted_iota(I32, (SUBLANES, tm), 0) * tm + lax.broadcasted_iota(I32, (SUBLANES, tm), 1)
        dvm[...] = (N_EXPERTS * region_rows + spare) * SUBLANES
        dests_to_smem(1).start()
        dests_to_smem(1).wait()

    if not per_token_hist:
        @pl.when(i % tiles_per_seq == 0)
        def _():
            hist_ref[...] = jnp.zeros_like(hist_ref)

    @pl.when(i > 0)
    def _():
        dests_to_smem(prev).wait()

    for t in range(tm):
        for kk in range(TOP_K):
            row_scatter(prev, t, kk).start()

    u = cc_ref[...].astype(F32) * ch_ref[...].astype(F32)
    if per_token_hist:
        u2 = h0_ref[...]
        u1 = h1_ref[...]
        u_out[...] = u
    else:
        hm2 = hist_ref[0:1, :]
        hm1 = hist_ref[1:2, :]
        row = lax.broadcasted_iota(I32, u.shape, 0)
        u1 = jnp.where(row == 0, hm1, pltpu.roll(u, 1, axis=0))
        u2 = jnp.where(row == 0, hm2, jnp.where(row == 1, hm1, pltpu.roll(u, 2, axis=0)))
        last2 = u[tm - (CONV_W - 1):, :]
        hist_ref[...] = last2
        u_out[0] = last2
    conv = cw_ref[0:1, :] * u2 + cw_ref[1:2, :] * u1 + cw_ref[2:3, :] * u
    z = (cb_ref[...].astype(F32) * conv).astype(BF16)
    o_a = jnp.dot(z, wa_ref[...], preferred_element_type=F32)
    o_b = jnp.dot(r_ref[...].astype(BF16), wb_ref[...], preferred_element_type=F32)
    merged = (jax.nn.sigmoid(ga_ref[...].astype(F32)) * o_a
              + jax.nn.sigmoid(gb_ref[...].astype(F32)) * o_b)
    m = jnp.dot(merged.astype(BF16), wo_ref[...], preferred_element_type=F32)
    h = x_ref[...] + m
    h_out[...] = h
    hn = (h * lax.rsqrt(jnp.mean(h * h, axis=-1, keepdims=True) + NORM_EPS)) * n2_ref[...]
    hn_hi = hn.astype(BF16)
    hn_lo = (hn - hn_hi.astype(F32)).astype(BF16)
    nt = (((1,), (1,)), ((), ()))
    part = lax.dot_general(wr_ref[...], hn_hi, nt, preferred_element_type=F32)
    logits = (part[:N_EXPERTS] + part[N_EXPERTS:] + br_ref[...]
              + lax.dot_general(wr_ref[:N_EXPERTS, :], hn_lo, nt, preferred_element_type=F32))
    expert = lax.broadcasted_iota(I32, logits.shape, 0)
    vals, idxs = [], []
    for _ in range(TOP_K):
        top = jnp.max(logits, axis=0, keepdims=True)
        arg = jnp.min(jnp.where(logits == top, expert, N_EXPERTS), axis=0, keepdims=True)
        vals.append(top)
        idxs.append(arg)
        logits = jnp.where(expert == arg, -jnp.inf, logits)
    exps = [jnp.exp(val - vals[0]) for val in vals]
    denom = exps[0] + exps[1] + exps[2] + exps[3]

    picks = [expert == idx for idx in idxs]
    chosen = jnp.zeros(logits.shape, F32)
    for pick in picks:
        chosen = jnp.where(pick, 1.0, chosen)
    before = (lax.broadcasted_iota(I32, (tm, tm), 0) < lax.broadcasted_iota(I32, (tm, tm), 1))
    prefix = jnp.dot(chosen.astype(BF16), jnp.where(before, 1.0, 0.0).astype(BF16),
                     preferred_element_type=F32)
    counts = cnt_ref[...]
    base = jnp.concatenate([counts] * (tm // LANES), axis=1) + prefix
    ranks = [jnp.sum(jnp.where(pick, base, 0.0), axis=0, keepdims=True).astype(I32) for pick in picks]
    counts = counts + jnp.sum(chosen, axis=1, keepdims=True)
    cnt_ref[...] = counts
    cnt_out[...] = counts

    slot_row = lax.broadcasted_iota(I32, (SUBLANES, tm), 0)
    route = jnp.zeros((SUBLANES, tm), I32)
    gates = jnp.zeros((SUBLANES, tm), F32)
    dests = jnp.zeros((SUBLANES, tm), I32)
    for kk in range(TOP_K):
        route = jnp.where(slot_row == kk, idxs[kk], route)
        route = jnp.where(slot_row == TOP_K + kk, ranks[kk], route)
        gates = jnp.where(slot_row == kk, exps[kk] / denom, gates)
        dests = jnp.where(slot_row == kk, (idxs[kk] * region_rows + ranks[kk]) * SUBLANES, dests)
    route_out[...] = route
    gate_out[...] = gates

    @pl.when(i > 0)
    def _():
        wait_scatters(cur)

    for c in range(D_MODEL // LANES):
        hn_st[cur, pl.ds(c, tm, stride=SUBLANES), :] = hn[:, c * LANES:(c + 1) * LANES]
    dvm[...] = dests
    dests_to_smem(cur).start()

    @pl.when(i == last)
    def _():
        dests_to_smem(cur).wait()
        wait_scatters(prev)

        def issue(g, carry):
            for uu in range(ISSUE_UNROLL):
                for kk in range(TOP_K):
                    row_scatter(cur, g * ISSUE_UNROLL + uu, kk).start()
            return carry

        lax.fori_loop(0, tm // ISSUE_UNROLL, issue, 0)
        wait_scatters(cur)


def _merge(proj, r, x, hist, conv_w, wa, wb, wo, norm2, w_router, b_router, counts_in, xs_prev, region_rows,
           tm, tiles_per_seq):
    t = x.shape[0]
    per_token = hist is not None
    chained = xs_prev is not None
    col = lambda c: (lambda i: (i, c // D_MODEL))
    const2 = lambda i: (0, 0)
    tile = pl.BlockSpec((tm, D_MODEL), lambda i: (i, 0))
    in_specs = [
        pl.BlockSpec((tm, D_MODEL), col(COL_CB)),
        pl.BlockSpec((tm, D_MODEL), col(COL_CC)),
        pl.BlockSpec((tm, D_MODEL), col(COL_CH)),
        pl.BlockSpec((tm, D_MODEL), col(COL_GA)),
        pl.BlockSpec((tm, D_MODEL), col(COL_GB)),
        pl.BlockSpec((tm, RET_V), lambda i: (i, 0)),
        tile,
    ]
    args = [proj, proj, proj, proj, proj, r, x]
    if per_token:
        in_specs += [tile, tile]
        args += list(hist)
    in_specs += [
        pl.BlockSpec((CONV_W, D_MODEL), const2),
        pl.BlockSpec((D_MODEL, D_MODEL), const2),
        pl.BlockSpec((RET_V, D_MODEL), const2),
        pl.BlockSpec((D_MODEL, D_MODEL), const2),
        pl.BlockSpec((1, D_MODEL), const2),
        pl.BlockSpec((2 * N_EXPERTS, D_MODEL), const2),
        pl.BlockSpec((N_EXPERTS, tm), const2),
        pl.BlockSpec((N_EXPERTS, LANES), const2),
    ]
    args += [conv_w, wa, wb, wo, norm2, w_router, b_router, counts_in]
    aliases = {}
    if chained:
        aliases = {len(args): 5}
        in_specs.append(pl.BlockSpec(memory_space=pl.ANY))
        args.append(xs_prev)
    lanes_tile = pl.BlockSpec((SUBLANES, tm), lambda i: (0, i))
    if per_token:
        u_spec = tile
        u_shape = jax.ShapeDtypeStruct((t, D_MODEL), F32)
        scratch = []
    else:
        n_seq = t // (tm * tiles_per_seq)
        u_spec = pl.BlockSpec((1, CONV_W - 1, D_MODEL), lambda i: (i // tiles_per_seq, 0, 0))
        u_shape = jax.ShapeDtypeStruct((n_seq, CONV_W - 1, D_MODEL), F32)
        scratch = [pltpu.VMEM((CONV_W - 1, D_MODEL), F32)]
    out_specs = [tile, lanes_tile, lanes_tile, u_spec, pl.BlockSpec((N_EXPERTS, LANES), const2),
                 pl.BlockSpec(memory_space=pl.ANY)]
    out_shape = [
        jax.ShapeDtypeStruct((t, D_MODEL), F32),
        jax.ShapeDtypeStruct((SUBLANES, t), I32),
        jax.ShapeDtypeStruct((SUBLANES, t), F32),
        u_shape,
        jax.ShapeDtypeStruct((N_EXPERTS, LANES), F32),
        jax.ShapeDtypeStruct(((N_EXPERTS * region_rows + TOP_K * MERGE_TM) * SUBLANES, LANES), F32),
    ]
    scratch += [
        pltpu.VMEM((N_EXPERTS, LANES), F32),
        pltpu.VMEM((2, tm * SUBLANES, LANES), F32),
        pltpu.VMEM((SUBLANES, tm), I32),
        pltpu.SMEM((2, SUBLANES, tm), I32),
        pltpu.SemaphoreType.DMA((3,)),
    ]
    return pl.pallas_call(
        functools.partial(_merge_kernel, per_token, chained, tiles_per_seq, region_rows),
        grid=(t // tm,),
        in_specs=in_specs,
        out_specs=out_specs,
        out_shape=out_shape,
        scratch_shapes=scratch,
        input_output_aliases=aliases,
        compiler_params=_params(("arbitrary",)),
        name="merge_sample" if per_token else "merge_prompt",
    )(*args)


def _rows_from_tiles(ref, n_rows):
    return jnp.concatenate([ref[pl.ds(c, n_rows, stride=SUBLANES), :] for c in range(D_MODEL // LANES)], axis=1)


def _rows_to_tiles(ref, rows):
    for c in range(D_MODEL // LANES):
        ref[pl.ds(c, rows.shape[0], stride=SUBLANES), :] = rows[:, c * LANES:(c + 1) * LANES]


def _moe_kernel(be_ref, bx_ref, vr_ref, nused_ref, next_ref, *refs):
    x_refs = refs[:MOE_STEP_BLOCKS]
    w1_hbm, b1_ref, w2_hbm, b2_ref, o_ref, w1f, w2f, w1b, w2b, sem = refs[MOE_STEP_BLOCKS:]
    n_used = nused_ref[0]

    def weight_copies(e):
        return (pltpu.make_async_copy(w1_hbm.at[e], w1f, sem.at[0]),
                pltpu.make_async_copy(w2_hbm.at[e], w2f, sem.at[1]))

    def block(i, x_ref, o_view):
        @pl.when(i < n_used)
        def _():
            e = be_ref[i]
            prev = be_ref[jnp.maximum(i - 1, 0)]

            @pl.when(i == 0)
            def _():
                for cp in weight_copies(e):
                    cp.start()

            @pl.when((i == 0) | (e != prev))
            def _():
                for cp in weight_copies(e):
                    cp.wait()
                w1b[...] = w1f[...].astype(BF16)
                w2b[...] = w2f[...].astype(BF16)
                nxt = next_ref[e]

                @pl.when(nxt >= 0)
                def _():
                    for cp in weight_copies(nxt):
                        cp.start()

            live = lax.broadcasted_iota(I32, (MOE_BM, 1), 0) < vr_ref[i]
            x = jnp.where(live, _rows_from_tiles(x_ref, MOE_BM), 0.0).astype(BF16)
            h1 = jnp.dot(x, w1b[...], preferred_element_type=F32) + b1_ref[e]
            gate = jnp.minimum(h1[:, :D_FF], SWIGLU_LIMIT)
            up = jnp.clip(h1[:, D_FF:], -SWIGLU_LIMIT, SWIGLU_LIMIT)
            glu = gate * jax.nn.sigmoid(SWIGLU_ALPHA * gate)
            act = ((up + 1.0) * glu).astype(BF16)
            _rows_to_tiles(o_view, jnp.dot(act, w2b[...], preferred_element_type=F32) + b2_ref[e])

        @pl.when(i >= n_used)
        def _():
            o_view[...] = jnp.zeros(o_view.shape, F32)

    for sub in range(MOE_STEP_BLOCKS):
        rows = MOE_BM * SUBLANES
        block(pl.program_id(0) * MOE_STEP_BLOCKS + sub, x_refs[sub], o_ref.at[pl.ds(sub * rows, rows), :])


def _moe_blocks(xs, block_expert, block_xs, valid_rows, n_used, next_expert, w1, b1, w2, b2):
    n_blocks = block_expert.shape[0]
    rows = MOE_BM * SUBLANES
    x_spec = lambda sub: pl.BlockSpec(
        (rows, LANES), lambda i, be, bx, vr, nu, nx: (bx[i * MOE_STEP_BLOCKS + sub], 0))
    whole3 = lambda i, be, bx, vr, nu, nx: (0, 0, 0)
    grid_spec = pltpu.PrefetchScalarGridSpec(
        num_scalar_prefetch=5,
        grid=(n_blocks // MOE_STEP_BLOCKS,),
        in_specs=[x_spec(sub) for sub in range(MOE_STEP_BLOCKS)] + [
            pl.BlockSpec(memory_space=pl.ANY),
            pl.BlockSpec((N_EXPERTS, 1, 2 * D_FF), whole3),
            pl.BlockSpec(memory_space=pl.ANY),
            pl.BlockSpec((N_EXPERTS, 1, D_MODEL), whole3),
        ],
        out_specs=pl.BlockSpec((MOE_STEP_BLOCKS * rows, LANES), lambda i, be, bx, vr, nu, nx: (i, 0)),
        scratch_shapes=[
            pltpu.VMEM((D_MODEL, 2 * D_FF), F32),
            pltpu.VMEM((D_FF, D_MODEL), F32),
            pltpu.VMEM((D_MODEL, 2 * D_FF), BF16),
            pltpu.VMEM((D_FF, D_MODEL), BF16),
            pltpu.SemaphoreType.DMA((2,)),
        ],
    )
    return pl.pallas_call(
        _moe_kernel,
        grid_spec=grid_spec,
        out_shape=jax.ShapeDtypeStruct((n_blocks * rows, LANES), F32),
        compiler_params=_params(("arbitrary",)),
        name="moe_experts",
    )(block_expert, block_xs, valid_rows, n_used, next_expert, *([xs] * MOE_STEP_BLOCKS),
      w1, b1[:, None, :], w2, b2[:, None, :])


def _block_plan(counts, n_tokens, region_rows):
    n_blocks = -(-n_tokens * TOP_K // MOE_BM) + N_EXPERTS
    n_blocks = -(-n_blocks // MOE_STEP_BLOCKS) * MOE_STEP_BLOCKS
    blocks_e = (counts + MOE_BM - 1) // MOE_BM
    blk_end = jnp.cumsum(blocks_e)
    blk_start = blk_end - blocks_e
    n_used = blk_end[-1:]
    blk = jnp.maximum(jnp.minimum(jnp.arange(n_blocks, dtype=I32), n_used - 1), 0)
    expert = jnp.minimum(jnp.sum((blk_end[None, :] <= blk[:, None]).astype(I32), axis=1), N_EXPERTS - 1)
    within = blk - blk_start[expert]
    block_xs = expert * (region_rows // MOE_BM) + within
    valid = jnp.clip(counts[expert] - within * MOE_BM, 0, MOE_BM)
    ids = jnp.arange(N_EXPERTS, dtype=I32)
    later = jnp.where((ids[None, :] > ids[:, None]) & (counts[None, :] > 0), ids[None, :], N_EXPERTS)
    next_expert = jnp.min(later, axis=1)
    next_expert = jnp.where(next_expert == N_EXPERTS, -1, next_expert)
    return (expert, block_xs.astype(I32), valid.astype(I32), n_used.astype(I32), next_expert.astype(I32),
            (blk_start * MOE_BM).astype(I32))


def _combine_kernel(n_main, n_tokens, dest_ref, hp_ref, hs_ref, gp_ref, gs_ref, nf_ref, yb_hbm, op_ref, os_ref,
                    *scratch):
    ybufs, sem = scratch[:-1], scratch[-1]
    i = pl.program_id(0)
    last = pl.num_programs(0) - 1
    ring = len(ybufs)
    ahead = ring - 1

    def row_copy(tile, sl, t, kk):
        row = pl.multiple_of(dest_ref[kk * n_tokens + tile * COMB_TT + t], SUBLANES)
        rows = t * SUBLANES if isinstance(t, int) else pl.multiple_of(t * SUBLANES, SUBLANES)
        return pltpu.make_async_copy(yb_hbm.at[pl.ds(row, SUBLANES), :],
                                     ybufs[sl].at[kk, pl.ds(rows, SUBLANES), :], sem.at[sl])

    def wait_tile(sl):
        for kk in range(TOP_K):
            pltpu.make_async_copy(yb_hbm.at[pl.ds(0, COMB_TT * SUBLANES), :], ybufs[sl].at[kk], sem.at[sl]).wait()

    @pl.when(i == 0)
    def _():
        def issue(g, carry):
            for uu in range(ISSUE_UNROLL):
                for kk in range(TOP_K):
                    for tile in range(ahead):
                        row_copy(jnp.minimum(tile, last), tile, g * ISSUE_UNROLL + uu, kk).start()
            return carry

        lax.fori_loop(0, COMB_TT // ISSUE_UNROLL, issue, 0)

    def step(cur, h_ref, gate_ref, o_ref):
        wait_tile(cur)
        later = jnp.minimum(i + ahead, last)
        for t in range(COMB_TT):
            for kk in range(TOP_K):
                row_copy(later, (cur + ahead) % ring, t, kk).start(priority=kk % 2)
        pad = jnp.zeros((COMB_TT - SUBLANES, COMB_TT), F32)
        gates = jnp.concatenate([gate_ref[...], pad], axis=0).T
        y = gates[:, 0:1] * _rows_from_tiles(ybufs[cur].at[0], COMB_TT)
        for kk in range(1, TOP_K):
            y = y + gates[:, kk:kk + 1] * _rows_from_tiles(ybufs[cur].at[kk], COMB_TT)
        h = h_ref[...] + y
        o_ref[...] = (h * lax.rsqrt(jnp.mean(h * h, axis=-1, keepdims=True) + NORM_EPS)) * nf_ref[...]

    for cur in range(ring):
        @pl.when((i % ring == cur) & (i < n_main))
        def _(cur=cur):
            step(cur, hp_ref, gp_ref, op_ref)

        @pl.when((i % ring == cur) & (i >= n_main))
        def _(cur=cur):
            step(cur, hs_ref, gs_ref, os_ref)

    @pl.when(i == last)
    def _():
        for cur in range(ring):
            @pl.when(i % ring == cur)
            def _(cur=cur):
                for extra in range(1, ring):
                    wait_tile((cur + extra) % ring)


def _combine(dest, h_p, h_s, gate_p, gate_s, norm_f, yb):
    n_main = h_p.shape[0] // COMB_TT
    n_tail = h_s.shape[0] // COMB_TT
    main = lambda i, d: (jnp.minimum(i, n_main - 1), 0)
    tail = lambda i, d: (jnp.maximum(i - n_main, 0), 0)
    grid_spec = pltpu.PrefetchScalarGridSpec(
        num_scalar_prefetch=1,
        grid=(n_main + n_tail,),
        in_specs=[
            pl.BlockSpec((COMB_TT, D_MODEL), main),
            pl.BlockSpec((COMB_TT, D_MODEL), tail),
            pl.BlockSpec((SUBLANES, COMB_TT), lambda i, d: (0, jnp.minimum(i, n_main - 1))),
            pl.BlockSpec((SUBLANES, COMB_TT), lambda i, d: (0, jnp.maximum(i - n_main, 0))),
            pl.BlockSpec((1, D_MODEL), lambda i, d: (0, 0)),
            pl.BlockSpec(memory_space=pl.ANY),
        ],
        out_specs=[
            pl.BlockSpec((COMB_TT, D_MODEL), main),
            pl.BlockSpec((COMB_TT, D_MODEL), tail),
        ],
        scratch_shapes=[
            *[pltpu.VMEM((TOP_K, COMB_TT * SUBLANES, LANES), F32) for _ in range(COMB_RING)],
            pltpu.SemaphoreType.DMA((COMB_RING,)),
        ],
    )
    return pl.pallas_call(
        functools.partial(_combine_kernel, n_main, h_p.shape[0] + h_s.shape[0]),
        grid_spec=grid_spec,
        out_shape=[jax.ShapeDtypeStruct(h_p.shape, F32), jax.ShapeDtypeStruct(h_s.shape, F32)],
        compiler_params=_params(("arbitrary",)),
        name="combine",
    )(dest, h_p, h_s, gate_p, gate_s, norm_f, yb)


def kernel(x_prompt, x_sample, state_conv, state_ret, norm1, w_in, conv_w, w_a, w_b, w_o, norm2, w_router,
           b_router, w_e1, b_e1, w_e2, b_e2, norm_f):
    batch, seq_len, _ = x_prompt.shape
    n_dec, dec_seq, _ = x_sample.shape
    depth = norm1.shape[0]
    assert dec_seq == 1 and depth == 1, "single-token decode step of a one-layer trunk"
    past_len = 16384
    t_p = batch * seq_len
    n_tokens = t_p + n_dec
    region_rows = _expert_region_rows(n_tokens)
    lyr = 0
    xp = x_prompt.reshape(t_p, D_MODEL)
    xs = x_sample.reshape(n_dec, D_MODEL)
    g1 = norm1[lyr][None, :]
    w_in_b = w_in[lyr].astype(BF16)
    wa, wb, wo = w_a[lyr].astype(BF16), w_b[lyr].astype(BF16), w_o[lyr].astype(BF16)
    g2 = norm2[lyr][None, :]
    wr_t = w_router[lyr].T
    wr_hi = wr_t.astype(BF16)
    wr_split = jnp.concatenate([wr_hi, (wr_t - wr_hi.astype(F32)).astype(BF16)], axis=0)
    br_col = b_router[lyr][:, None]

    proj_p = _inproj(xp, g1, w_in_b, INPROJ_TM, BF16)
    r_p, ret_p = _retention_prompt(proj_p, batch, seq_len)
    h_p, route_p, gate_p, conv_p, counts, dispatch = _merge(
        proj_p, r_p, xp, None, conv_w[lyr], wa, wb, wo, g2, wr_split, jnp.broadcast_to(br_col, (N_EXPERTS, MERGE_TM)),
        jnp.zeros((N_EXPERTS, LANES), F32), None, region_rows, MERGE_TM, seq_len // MERGE_TM)
    proj_s = _inproj(xs, g1, w_in_b, n_dec, F32)
    r_s, ret_s = _retention_sample(proj_s, state_ret[lyr], past_len)
    hist = (state_conv[lyr][:, 0, :], state_conv[lyr][:, 1, :])
    h_s, route_s, gate_s, u_s, counts, dispatch = _merge(
        proj_s, r_s, xs, hist, conv_w[lyr], wa, wb, wo, g2, wr_split, jnp.broadcast_to(br_col, (N_EXPERTS, n_dec)),
        counts, dispatch, region_rows, n_dec, 1)
    conv_s = jnp.stack([state_conv[lyr][:, 1, :], u_s], axis=1)

    counts_i = counts[:, 0].astype(I32)
    block_expert, block_xs, valid_rows, n_used, next_expert, row_start = _block_plan(
        counts_i, n_tokens, region_rows)
    yb = _moe_blocks(dispatch, block_expert, block_xs, valid_rows, n_used, next_expert,
                     w_e1[lyr], b_e1[lyr], w_e2[lyr], b_e2[lyr])
    route = jnp.concatenate([route_p, route_s], axis=1)
    expert_ids = jnp.arange(N_EXPERTS, dtype=I32)[:, None, None]
    first_row = jnp.sum(jnp.where(route[None, :TOP_K] == expert_ids, row_start[:, None, None], 0), axis=0)
    dest = ((first_row + route[TOP_K:]) * SUBLANES).reshape(TOP_K * n_tokens)
    y_p, y_s = _combine(dest, h_p, h_s, gate_p, gate_s, norm_f[None, :], yb)

    y_prompt = y_p.reshape(batch, seq_len, D_MODEL)
    y_sample = y_s.reshape(n_dec, 1, D_MODEL)
    return (y_prompt, y_sample, conv_p[None], ret_p[None], conv_s[None], ret_s[None])
```

```python
import functools

import jax
import jax.numpy as jnp
from jax import lax
from jax.experimental import pallas as pl
from jax.experimental.pallas import tpu as pltpu

F32 = jnp.float32
BF16 = jnp.bfloat16
I32 = jnp.int32

D_MODEL = 1024
CONV_W = 3
N_HEADS = 4
DK = 256
DV = 512
HALF = DK // 2
RET_QK = N_HEADS * DK
RET_V = N_HEADS * DV
ROPE_BASE = 10000.0
N_EXPERTS = 32
TOP_K = 4
D_FF = 1024
SWIGLU_ALPHA = 1.702
SWIGLU_LIMIT = 7.0
NORM_EPS = 1e-5
N_IN_COLS = 3 * D_MODEL + 2 * RET_QK + 2 * RET_V + 2 * D_MODEL
COL_CB, COL_CC, COL_CH = 0, D_MODEL, 2 * D_MODEL
COL_Q = 3 * D_MODEL
COL_K = COL_Q + RET_QK
COL_V = COL_K + RET_QK
COL_G = COL_V + RET_V
COL_GA = COL_G + RET_V
COL_GB = COL_GA + D_MODEL

RET_CHUNK = 256
MERGE_TM = 256
INPROJ_TM = 2048
INPROJ_TN = 1024
INPROJ_TN_SAMPLE = 2816
MOE_BM = 256
MOE_STEP_BLOCKS = 4
COMB_TT = 128
COMB_RING = 3
SAMPLE_GROUP = 16
LANES = 128
SUBLANES = 8
ISSUE_UNROLL = 8
DMA_QUEUES = 2
VMEM_LIMIT = 56 * 1024 * 1024


def _params(sem):
    return pltpu.CompilerParams(dimension_semantics=sem, vmem_limit_bytes=VMEM_LIMIT)


def _expert_region_rows(n_tokens):
    return -(-n_tokens // MOE_BM) * MOE_BM


def _inproj_kernel(x_ref, g_ref, w_ref, o_ref, xn_ref):
    @pl.when(pl.program_id(1) == 0)
    def _():
        x = x_ref[...]
        ms = jnp.mean(x * x, axis=-1, keepdims=True)
        xn_ref[...] = ((x * lax.rsqrt(ms + NORM_EPS)) * g_ref[...]).astype(BF16)

    o_ref[...] = jnp.dot(xn_ref[...], w_ref[...], preferred_element_type=F32).astype(o_ref.dtype)


def _inproj(x, gain, w_bf16, tm, tn, out_dtype):
    m = x.shape[0]
    n = w_bf16.shape[1]
    return pl.pallas_call(
        _inproj_kernel,
        grid=(m // tm, n // tn),
        in_specs=[
            pl.BlockSpec((tm, D_MODEL), lambda i, j: (i, 0)),
            pl.BlockSpec((1, D_MODEL), lambda i, j: (0, 0)),
            pl.BlockSpec((D_MODEL, tn), lambda i, j: (0, j)),
        ],
        out_specs=pl.BlockSpec((tm, tn), lambda i, j: (i, j)),
        out_shape=jax.ShapeDtypeStruct((m, n), out_dtype),
        scratch_shapes=[pltpu.VMEM((tm, D_MODEL), BF16)],
        compiler_params=_params(("arbitrary", "arbitrary")),
        name="inproj",
    )(x, gain, w_bf16)


def _rotary(x, cos, sin):
    x1 = x[:, :HALF]
    x2 = x[:, HALF:]
    return jnp.concatenate([x1 * cos - x2 * sin, x1 * sin + x2 * cos], axis=-1)


def _group_norm_gate(o, g):
    o = o * lax.rsqrt(jnp.mean(o * o, axis=-1, keepdims=True) + NORM_EPS)
    return (g * jax.nn.sigmoid(g)) * o


def _ret_chunk_kernel(q_ref, k_ref, v0_ref, v1_ref, g0_ref, g1_ref, cos_ref, sin_ref, mask_ref, qd_ref, kd_ref,
                      cd_ref, r_ref, s_ref):
    @pl.when(pl.program_id(1) == 0)
    def _():
        s_ref[...] = jnp.zeros_like(s_ref)

    cos = cos_ref[...]
    sin = sin_ref[...]
    v_refs = (v0_ref, v1_ref)
    g_refs = (g0_ref, g1_ref)
    for h in range(N_HEADS):
        half_cols = slice((h % 2) * DV, (h % 2 + 1) * DV)
        q = _rotary(q_ref[:, h * DK:(h + 1) * DK].astype(F32), cos, sin)
        k = _rotary(k_ref[:, h * DK:(h + 1) * DK].astype(F32), cos, sin) * (DK ** -0.5)
        v = v_refs[h // 2][:, half_cols]
        qd = jnp.concatenate([qd_ref[h], qd_ref[h]], axis=-1)
        kd = jnp.concatenate([kd_ref[h], kd_ref[h]], axis=-1)
        state = s_ref[0, h]
        scores = lax.dot_general(q.astype(BF16), k.astype(BF16), (((1,), (1,)), ((), ())),
                                 preferred_element_type=F32) * mask_ref[h]
        intra = jnp.dot(scores.astype(BF16), v, preferred_element_type=F32)
        cross = jnp.dot((q * qd).astype(BF16), state.astype(BF16), preferred_element_type=F32)
        kv = jnp.dot((k * kd).T.astype(BF16), v, preferred_element_type=F32)
        s_ref[0, h] = cd_ref[h] * state + kv
        g = g_refs[h // 2][:, half_cols].astype(F32)
        r_ref[:, h * DV:(h + 1) * DV] = _group_norm_gate(intra + cross, g).astype(r_ref.dtype)


def _retention_tables(chunk, seq_len, pos0):
    log_g = jnp.log1p(-jnp.exp2(-5.0 - jnp.arange(N_HEADS, dtype=F32)))
    pos = jnp.arange(chunk, dtype=F32)
    diff = pos[:, None] - pos[None, :]
    mask = jnp.where(diff >= 0, jnp.exp(jnp.maximum(diff, 0.0)[None] * log_g[:, None, None]), 0.0)
    q_decay = jnp.exp((pos + 1.0)[None, :] * log_g[:, None])[..., None]
    k_decay = jnp.exp((chunk - 1.0 - pos)[None, :] * log_g[:, None])[..., None]
    chunk_decay = jnp.exp(chunk * log_g)[:, None, None]
    qd = jnp.broadcast_to(q_decay, (N_HEADS, chunk, LANES))
    kd = jnp.broadcast_to(k_decay, (N_HEADS, chunk, LANES))
    cd = jnp.broadcast_to(chunk_decay, (N_HEADS, 1, DV))
    positions = pos0 + jnp.arange(seq_len, dtype=jnp.int32)
    inv = ROPE_BASE ** (-jnp.linspace(0.0, 1.0, HALF, dtype=F32))
    ang = positions.astype(F32)[:, None] * inv[None, :]
    return mask, qd, kd, cd, jnp.cos(ang), jnp.sin(ang)


def _retention_prompt(proj, batch, seq_len):
    c = RET_CHUNK
    nc = seq_len // c
    mask, qd, kd, cd, cos, sin = _retention_tables(c, seq_len, 0)
    row = lambda b, j: b * nc + j
    full3 = lambda b, j: (0, 0, 0)
    return pl.pallas_call(
        _ret_chunk_kernel,
        grid=(batch, nc),
        in_specs=[
            pl.BlockSpec((c, RET_QK), lambda b, j: (row(b, j), COL_Q // RET_QK)),
            pl.BlockSpec((c, RET_QK), lambda b, j: (row(b, j), COL_K // RET_QK)),
            pl.BlockSpec((c, RET_QK), lambda b, j: (row(b, j), COL_V // RET_QK)),
            pl.BlockSpec((c, RET_QK), lambda b, j: (row(b, j), COL_V // RET_QK + 1)),
            pl.BlockSpec((c, RET_QK), lambda b, j: (row(b, j), COL_G // RET_QK)),
            pl.BlockSpec((c, RET_QK), lambda b, j: (row(b, j), COL_G // RET_QK + 1)),
            pl.BlockSpec((c, HALF), lambda b, j: (j, 0)),
            pl.BlockSpec((c, HALF), lambda b, j: (j, 0)),
            pl.BlockSpec((N_HEADS, c, c), full3),
            pl.BlockSpec((N_HEADS, c, LANES), full3),
            pl.BlockSpec((N_HEADS, c, LANES), full3),
            pl.BlockSpec((N_HEADS, 1, DV), full3),
        ],
        out_specs=[
            pl.BlockSpec((c, RET_V), lambda b, j: (row(b, j), 0)),
            pl.BlockSpec((1, N_HEADS, DK, DV), lambda b, j: (b, 0, 0, 0)),
        ],
        out_shape=[
            jax.ShapeDtypeStruct((batch * seq_len, RET_V), BF16),
            jax.ShapeDtypeStruct((batch, N_HEADS, DK, DV), F32),
        ],
        compiler_params=_params(("arbitrary", "arbitrary")),
        name="retention_chunk",
    )(proj, proj, proj, proj, proj, proj, cos, sin, mask, qd, kd, cd)


def _ret_step_kernel(q_ref, k_ref, v_ref, g_ref, cos_ref, sin_ref, gam_ref, s_ref, r_ref, so_ref):
    cos = cos_ref[...]
    sin = sin_ref[...]
    gam = gam_ref[0]
    q = _rotary(q_ref[...], cos, sin)
    k = _rotary(k_ref[...], cos, sin) * (DK ** -0.5)
    v = v_ref[...]
    intra = jnp.sum(q * k, axis=-1, keepdims=True) * v
    q_t = (q * gam[:, :DK]).T
    k_t = k.T
    cross = []
    for s in range(SAMPLE_GROUP):
        state = s_ref[s, 0]
        cross.append(jnp.sum(q_t[:, s:s + 1] * state, axis=0, keepdims=True))
        so_ref[s, 0] = gam * state + k_t[:, s:s + 1] * v[s:s + 1, :]
    o = intra + jnp.concatenate(cross, axis=0)
    r_ref[...] = _group_norm_gate(o, g_ref[...])


def _retention_sample(proj, state, pos0):
    n = proj.shape[0]
    grp = SAMPLE_GROUP
    log_g = jnp.log1p(-jnp.exp2(-5.0 - jnp.arange(N_HEADS, dtype=F32)))
    gam = jnp.broadcast_to(jnp.exp(log_g)[:, None, None], (N_HEADS, 1, DV))
    _, _, _, _, cos, sin = _retention_tables(1, 1, pos0)
    return pl.pallas_call(
        _ret_step_kernel,
        grid=(n // grp, N_HEADS),
        in_specs=[
            pl.BlockSpec((grp, DK), lambda j, h: (j, COL_Q // DK + h)),
            pl.BlockSpec((grp, DK), lambda j, h: (j, COL_K // DK + h)),
            pl.BlockSpec((grp, DV), lambda j, h: (j, COL_V // DV + h)),
            pl.BlockSpec((grp, DV), lambda j, h: (j, COL_G // DV + h)),
            pl.BlockSpec((1, HALF), lambda j, h: (0, 0)),
            pl.BlockSpec((1, HALF), lambda j, h: (0, 0)),
            pl.BlockSpec((1, 1, DV), lambda j, h: (h, 0, 0)),
            pl.BlockSpec((grp, 1, DK, DV), lambda j, h: (j, h, 0, 0)),
        ],
        out_specs=[
            pl.BlockSpec((grp, DV), lambda j, h: (j, h)),
            pl.BlockSpec((grp, 1, DK, DV), lambda j, h: (j, h, 0, 0)),
        ],
        out_shape=[
            jax.ShapeDtypeStruct((n, RET_V), F32),
            jax.ShapeDtypeStruct(state.shape, F32),
        ],
        compiler_params=_params(("arbitrary", "arbitrary")),
        name="retention_step",
    )(proj, proj, proj, proj, cos, sin, gam, state)


def _merge_kernel(per_token_hist, chained, tiles_per_seq, region_rows, *refs):
    refs = list(refs)
    cb_ref, cc_ref, ch_ref, ga_ref, gb_ref, r_ref, x_ref = refs[:7]
    del refs[:7]
    if per_token_hist:
        h0_ref, h1_ref = refs[:2]
        del refs[:2]
    cw_ref, wa_ref, wb_ref, wo_ref, n2_ref, wr_ref, br_ref, cnt_in = refs[:8]
    del refs[:8]
    if chained:
        del refs[:1]
    h_out, route_out, gate_out, u_out, cnt_out, xs_out = refs[:6]
    del refs[:6]
    if not per_token_hist:
        hist_ref = refs.pop(0)
    cnt_ref, hn_st, dvm, dsm, sem = refs

    i = pl.program_id(0)
    last = pl.num_programs(0) - 1
    tm = x_ref.shape[0]
    cur = i % 2
    prev = 1 - cur

    def wait_scatters(p):
        for _ in range(TOP_K):
            pltpu.make_async_copy(hn_st.at[p], xs_out.at[pl.ds(0, tm * SUBLANES), :], sem.at[p]).wait()

    def dests_to_smem(p):
        return pltpu.make_async_copy(dvm, dsm.at[p], sem.at[2])

    def row_scatter(p, t, kk):
        rows = t * SUBLANES if isinstance(t, int) else pl.multiple_of(t * SUBLANES, SUBLANES)
        dst = xs_out.at[pl.ds(pl.multiple_of(dsm[p, kk, t], SUBLANES), SUBLANES), :]
        return pltpu.make_async_copy(hn_st.at[p, pl.ds(rows, SUBLANES), :], dst, sem.at[p])

    @pl.when(i == 0)
    def _():
        cnt_ref[...] = cnt_in[...]
        hn_st[1] = jnp.zeros((tm * SUBLANES, LANES), F32)
        spare = lax.broadcasted_iota(I32, (SUBLANES, tm), 0) * tm + lax.broadcasted_iota(I32, (SUBLANES, tm), 1)
        dvm[...] = (N_EXPERTS * region_rows + spare) * SUBLANES
        dests_to_smem(1).start()
        dests_to_smem(1).wait()

    if not per_token_hist:
        @pl.when(i % tiles_per_seq == 0)
        def _():
            hist_ref[...] = jnp.zeros_like(hist_ref)

    @pl.when(i > 0)
    def _():
        dests_to_smem(prev).wait()

    for t in range(tm):
        for kk in range(TOP_K):
            row_scatter(prev, t, kk).start(priority=kk % DMA_QUEUES)

    u = cc_ref[...].astype(F32) * ch_ref[...].astype(F32)
    if per_token_hist:
        u2 = h0_ref[...]
        u1 = h1_ref[...]
        u_out[...] = u
    else:
        hm2 = hist_ref[0:1, :]
        hm1 = hist_ref[1:2, :]
        row = lax.broadcasted_iota(I32, u.shape, 0)
        u1 = jnp.where(row == 0, hm1, pltpu.roll(u, 1, axis=0))
        u2 = jnp.where(row == 0, hm2, jnp.where(row == 1, hm1, pltpu.roll(u, 2, axis=0)))
        last2 = u[tm - (CONV_W - 1):, :]
        hist_ref[...] = last2
        u_out[0] = last2
    conv = cw_ref[0:1, :] * u2 + cw_ref[1:2, :] * u1 + cw_ref[2:3, :] * u
    z = (cb_ref[...].astype(F32) * conv).astype(BF16)
    o_a = jnp.dot(z, wa_ref[...], preferred_element_type=F32)
    o_b = jnp.dot(r_ref[...].astype(BF16), wb_ref[...], preferred_element_type=F32)
    merged = (jax.nn.sigmoid(ga_ref[...].astype(F32)) * o_a
              + jax.nn.sigmoid(gb_ref[...].astype(F32)) * o_b)
    m = jnp.dot(merged.astype(BF16), wo_ref[...], preferred_element_type=F32)
    h = x_ref[...] + m
    h_out[...] = h
    hn = (h * lax.rsqrt(jnp.mean(h * h, axis=-1, keepdims=True) + NORM_EPS)) * n2_ref[...]
    hn_hi = hn.astype(BF16)
    hn_lo = (hn - hn_hi.astype(F32)).astype(BF16)
    nt = (((1,), (1,)), ((), ()))
    part = lax.dot_general(wr_ref[...], hn_hi, nt, preferred_element_type=F32)
    logits = (part[:N_EXPERTS] + part[N_EXPERTS:] + br_ref[...]
              + lax.dot_general(wr_ref[:N_EXPERTS, :], hn_lo, nt, preferred_element_type=F32))
    expert = lax.broadcasted_iota(I32, logits.shape, 0)
    vals, idxs = [], []
    for _ in range(TOP_K):
        top = jnp.max(logits, axis=0, keepdims=True)
        arg = jnp.min(jnp.where(logits == top, expert, N_EXPERTS), axis=0, keepdims=True)
        vals.append(top)
        idxs.append(arg)
        logits = jnp.where(expert == arg, -jnp.inf, logits)
    exps = [jnp.exp(val - vals[0]) for val in vals]
    denom = exps[0] + exps[1] + exps[2] + exps[3]

    picks = [expert == idx for idx in idxs]
    chosen = jnp.zeros(logits.shape, F32)
    for pick in picks:
        chosen = jnp.where(pick, 1.0, chosen)
    before = (lax.broadcasted_iota(I32, (tm, tm), 0) < lax.broadcasted_iota(I32, (tm, tm), 1))
    prefix = jnp.dot(chosen.astype(BF16), jnp.where(before, 1.0, 0.0).astype(BF16),
                     preferred_element_type=F32)
    counts = cnt_ref[...]
    base = jnp.concatenate([counts] * (tm // LANES), axis=1) + prefix
    ranks = [jnp.sum(jnp.where(pick, base, 0.0), axis=0, keepdims=True).astype(I32) for pick in picks]
    counts = counts + jnp.sum(chosen, axis=1, keepdims=True)
    cnt_ref[...] = counts
    cnt_out[...] = counts

    slot_row = lax.broadcasted_iota(I32, (SUBLANES, tm), 0)
    route = jnp.zeros((SUBLANES, tm), I32)
    gates = jnp.zeros((SUBLANES, tm), F32)
    dests = jnp.zeros((SUBLANES, tm), I32)
    for kk in range(TOP_K):
        route = jnp.where(slot_row == kk, idxs[kk], route)
        route = jnp.where(slot_row == TOP_K + kk, ranks[kk], route)
        gates = jnp.where(slot_row == kk, exps[kk] / denom, gates)
        dests = jnp.where(slot_row == kk, (idxs[kk] * region_rows + ranks[kk]) * SUBLANES, dests)
    route_out[...] = route
    gate_out[...] = gates

    @pl.when(i > 0)
    def _():
        wait_scatters(cur)

    for c in range(D_MODEL // LANES):
        hn_st[cur, pl.ds(c, tm, stride=SUBLANES), :] = hn[:, c * LANES:(c + 1) * LANES]
    dvm[...] = dests
    dests_to_smem(cur).start()

    @pl.when(i == last)
    def _():
        dests_to_smem(cur).wait()
        wait_scatters(prev)

        def issue(g, carry):
            for uu in range(ISSUE_UNROLL):
                for kk in range(TOP_K):
                    row_scatter(cur, g * ISSUE_UNROLL + uu, kk).start()
            return carry

        lax.fori_loop(0, tm // ISSUE_UNROLL, issue, 0)
        wait_scatters(cur)


def _merge(proj, r, x, hist, conv_w, wa, wb, wo, norm2, w_router, b_router, counts_in, xs_prev, region_rows,
           tm, tiles_per_seq):
    t = x.shape[0]
    per_token = hist is not None
    chained = xs_prev is not None
    col = lambda c: (lambda i: (i, c // D_MODEL))
    const2 = lambda i: (0, 0)
    tile = pl.BlockSpec((tm, D_MODEL), lambda i: (i, 0))
    in_specs = [
        pl.BlockSpec((tm, D_MODEL), col(COL_CB)),
        pl.BlockSpec((tm, D_MODEL), col(COL_CC)),
        pl.BlockSpec((tm, D_MODEL), col(COL_CH)),
        pl.BlockSpec((tm, D_MODEL), col(COL_GA)),
        pl.BlockSpec((tm, D_MODEL), col(COL_GB)),
        pl.BlockSpec((tm, RET_V), lambda i: (i, 0)),
        tile,
    ]
    args = [proj, proj, proj, proj, proj, r, x]
    if per_token:
        in_specs += [tile, tile]
        args += list(hist)
    in_specs += [
        pl.BlockSpec((CONV_W, D_MODEL), const2),
        pl.BlockSpec((D_MODEL, D_MODEL), const2),
        pl.BlockSpec((RET_V, D_MODEL), const2),
        pl.BlockSpec((D_MODEL, D_MODEL), const2),
        pl.BlockSpec((1, D_MODEL), const2),
        pl.BlockSpec((2 * N_EXPERTS, D_MODEL), const2),
        pl.BlockSpec((N_EXPERTS, tm), const2),
        pl.BlockSpec((N_EXPERTS, LANES), const2),
    ]
    args += [conv_w, wa, wb, wo, norm2, w_router, b_router, counts_in]
    aliases = {}
    if chained:
        aliases = {len(args): 5}
        in_specs.append(pl.BlockSpec(memory_space=pl.ANY))
        args.append(xs_prev)
    lanes_tile = pl.BlockSpec((SUBLANES, tm), lambda i: (0, i))
    if per_token:
        u_spec = tile
        u_shape = jax.ShapeDtypeStruct((t, D_MODEL), F32)
        scratch = []
    else:
        n_seq = t // (tm * tiles_per_seq)
        u_spec = pl.BlockSpec((1, CONV_W - 1, D_MODEL), lambda i: (i // tiles_per_seq, 0, 0))
        u_shape = jax.ShapeDtypeStruct((n_seq, CONV_W - 1, D_MODEL), F32)
        scratch = [pltpu.VMEM((CONV_W - 1, D_MODEL), F32)]
    out_specs = [tile, lanes_tile, lanes_tile, u_spec, pl.BlockSpec((N_EXPERTS, LANES), const2),
                 pl.BlockSpec(memory_space=pl.ANY)]
    out_shape = [
        jax.ShapeDtypeStruct((t, D_MODEL), F32),
        jax.ShapeDtypeStruct((SUBLANES, t), I32),
        jax.ShapeDtypeStruct((SUBLANES, t), F32),
        u_shape,
        jax.ShapeDtypeStruct((N_EXPERTS, LANES), F32),
        jax.ShapeDtypeStruct(((N_EXPERTS * region_rows + TOP_K * MERGE_TM) * SUBLANES, LANES), F32),
    ]
    scratch += [
        pltpu.VMEM((N_EXPERTS, LANES), F32),
        pltpu.VMEM((2, tm * SUBLANES, LANES), F32),
        pltpu.VMEM((SUBLANES, tm), I32),
        pltpu.SMEM((2, SUBLANES, tm), I32),
        pltpu.SemaphoreType.DMA((3,)),
    ]
    return pl.pallas_call(
        functools.partial(_merge_kernel, per_token, chained, tiles_per_seq, region_rows),
        grid=(t // tm,),
        in_specs=in_specs,
        out_specs=out_specs,
        out_shape=out_shape,
        scratch_shapes=scratch,
        input_output_aliases=aliases,
        compiler_params=_params(("arbitrary",)),
        name="merge_sample" if per_token else "merge_prompt",
    )(*args)


def _rows_from_tiles(ref, n_rows):
    return jnp.concatenate([ref[pl.ds(c, n_rows, stride=SUBLANES), :] for c in range(D_MODEL // LANES)], axis=1)


def _rows_to_tiles(ref, rows):
    for c in range(D_MODEL // LANES):
        ref[pl.ds(c, rows.shape[0], stride=SUBLANES), :] = rows[:, c * LANES:(c + 1) * LANES]


def _moe_kernel(be_ref, bx_ref, vr_ref, nused_ref, next_ref, *refs):
    x_refs = refs[:MOE_STEP_BLOCKS]
    w1_hbm, b1_ref, w2_hbm, b2_ref, o_ref, w1f, w2f, w1b, w2b, sem = refs[MOE_STEP_BLOCKS:]
    n_used = nused_ref[0]

    def weight_copies(e):
        return (pltpu.make_async_copy(w1_hbm.at[e], w1f, sem.at[0]),
                pltpu.make_async_copy(w2_hbm.at[e], w2f, sem.at[1]))

    def block(i, x_ref, o_view):
        @pl.when(i < n_used)
        def _():
            e = be_ref[i]
            prev = be_ref[jnp.maximum(i - 1, 0)]

            @pl.when(i == 0)
            def _():
                for cp in weight_copies(e):
                    cp.start()

            @pl.when((i == 0) | (e != prev))
            def _():
                for cp in weight_copies(e):
                    cp.wait()
                w1b[...] = w1f[...].astype(BF16)
                w2b[...] = w2f[...].astype(BF16)
                nxt = next_ref[e]

                @pl.when(nxt >= 0)
                def _():
                    for cp in weight_copies(nxt):
                        cp.start()

            live = lax.broadcasted_iota(I32, (MOE_BM, 1), 0) < vr_ref[i]
            x = jnp.where(live, _rows_from_tiles(x_ref, MOE_BM), 0.0).astype(BF16)
            h1 = jnp.dot(x, w1b[...], preferred_element_type=F32) + b1_ref[e]
            gate = jnp.minimum(h1[:, :D_FF], SWIGLU_LIMIT)
            up = jnp.clip(h1[:, D_FF:], -SWIGLU_LIMIT, SWIGLU_LIMIT)
            glu = gate * jax.nn.sigmoid(SWIGLU_ALPHA * gate)
            act = ((up + 1.0) * glu).astype(BF16)
            _rows_to_tiles(o_view, jnp.dot(act, w2b[...], preferred_element_type=F32) + b2_ref[e])

        @pl.when(i >= n_used)
        def _():
            o_view[...] = jnp.zeros(o_view.shape, F32)

    for sub in range(MOE_STEP_BLOCKS):
        rows = MOE_BM * SUBLANES
        block(pl.program_id(0) * MOE_STEP_BLOCKS + sub, x_refs[sub], o_ref.at[pl.ds(sub * rows, rows), :])


def _moe_blocks(xs, block_expert, block_xs, valid_rows, n_used, next_expert, w1, b1, w2, b2):
    n_blocks = block_expert.shape[0]
    rows = MOE_BM * SUBLANES
    x_spec = lambda sub: pl.BlockSpec(
        (rows, LANES), lambda i, be, bx, vr, nu, nx: (bx[i * MOE_STEP_BLOCKS + sub], 0))
    whole3 = lambda i, be, bx, vr, nu, nx: (0, 0, 0)
    grid_spec = pltpu.PrefetchScalarGridSpec(
        num_scalar_prefetch=5,
        grid=(n_blocks // MOE_STEP_BLOCKS,),
        in_specs=[x_spec(sub) for sub in range(MOE_STEP_BLOCKS)] + [
            pl.BlockSpec(memory_space=pl.ANY),
            pl.BlockSpec((N_EXPERTS, 1, 2 * D_FF), whole3),
            pl.BlockSpec(memory_space=pl.ANY),
            pl.BlockSpec((N_EXPERTS, 1, D_MODEL), whole3),
        ],
        out_specs=pl.BlockSpec((MOE_STEP_BLOCKS * rows, LANES), lambda i, be, bx, vr, nu, nx: (i, 0)),
        scratch_shapes=[
            pltpu.VMEM((D_MODEL, 2 * D_FF), F32),
            pltpu.VMEM((D_FF, D_MODEL), F32),
            pltpu.VMEM((D_MODEL, 2 * D_FF), BF16),
            pltpu.VMEM((D_FF, D_MODEL), BF16),
            pltpu.SemaphoreType.DMA((2,)),
        ],
    )
    return pl.pallas_call(
        _moe_kernel,
        grid_spec=grid_spec,
        out_shape=jax.ShapeDtypeStruct((n_blocks * rows, LANES), F32),
        compiler_params=_params(("arbitrary",)),
        name="moe_experts",
    )(block_expert, block_xs, valid_rows, n_used, next_expert, *([xs] * MOE_STEP_BLOCKS),
      w1, b1[:, None, :], w2, b2[:, None, :])


def _block_plan(counts, n_tokens, region_rows):
    n_blocks = -(-n_tokens * TOP_K // MOE_BM) + N_EXPERTS
    n_blocks = -(-n_blocks // MOE_STEP_BLOCKS) * MOE_STEP_BLOCKS
    blocks_e = (counts + MOE_BM - 1) // MOE_BM
    blk_end = jnp.cumsum(blocks_e)
    blk_start = blk_end - blocks_e
    n_used = blk_end[-1:]
    blk = jnp.maximum(jnp.minimum(jnp.arange(n_blocks, dtype=I32), n_used - 1), 0)
    expert = jnp.minimum(jnp.sum((blk_end[None, :] <= blk[:, None]).astype(I32), axis=1), N_EXPERTS - 1)
    within = blk - blk_start[expert]
    block_xs = expert * (region_rows // MOE_BM) + within
    valid = jnp.clip(counts[expert] - within * MOE_BM, 0, MOE_BM)
    ids = jnp.arange(N_EXPERTS, dtype=I32)
    later = jnp.where((ids[None, :] > ids[:, None]) & (counts[None, :] > 0), ids[None, :], N_EXPERTS)
    next_expert = jnp.min(later, axis=1)
    next_expert = jnp.where(next_expert == N_EXPERTS, -1, next_expert)
    return (expert, block_xs.astype(I32), valid.astype(I32), n_used.astype(I32), next_expert.astype(I32),
            (blk_start * MOE_BM).astype(I32))


def _combine_kernel(n_main, n_tokens, dest_ref, hp_ref, hs_ref, gp_ref, gs_ref, nf_ref, yb_hbm, op_ref, os_ref,
                    *scratch):
    ybufs, sem = scratch[:-1], scratch[-1]
    i = pl.program_id(0)
    last = pl.num_programs(0) - 1
    ring = len(ybufs)
    ahead = ring - 1

    def row_copy(tile, sl, t, kk):
        row = pl.multiple_of(dest_ref[kk * n_tokens + tile * COMB_TT + t], SUBLANES)
        rows = t * SUBLANES if isinstance(t, int) else pl.multiple_of(t * SUBLANES, SUBLANES)
        return pltpu.make_async_copy(yb_hbm.at[pl.ds(row, SUBLANES), :],
                                     ybufs[sl].at[kk, pl.ds(rows, SUBLANES), :], sem.at[sl])

    def wait_tile(sl):
        for kk in range(TOP_K):
            pltpu.make_async_copy(yb_hbm.at[pl.ds(0, COMB_TT * SUBLANES), :], ybufs[sl].at[kk], sem.at[sl]).wait()

    @pl.when(i == 0)
    def _():
        def issue(g, carry):
            for uu in range(ISSUE_UNROLL):
                for kk in range(TOP_K):
                    for tile in range(ahead):
                        row_copy(jnp.minimum(tile, last), tile, g * ISSUE_UNROLL + uu, kk).start()
            return carry

        lax.fori_loop(0, COMB_TT // ISSUE_UNROLL, issue, 0)

    def step(cur, h_ref, gate_ref, o_ref):
        wait_tile(cur)
        later = jnp.minimum(i + ahead, last)
        for t in range(COMB_TT):
            for kk in range(TOP_K):
                row_copy(later, (cur + ahead) % ring, t, kk).start(priority=kk % DMA_QUEUES)
        pad = jnp.zeros((COMB_TT - SUBLANES, COMB_TT), F32)
        gates = jnp.concatenate([gate_ref[...], pad], axis=0).T
        y = gates[:, 0:1] * _rows_from_tiles(ybufs[cur].at[0], COMB_TT)
        for kk in range(1, TOP_K):
            y = y + gates[:, kk:kk + 1] * _rows_from_tiles(ybufs[cur].at[kk], COMB_TT)
        h = h_ref[...] + y
        o_ref[...] = (h * lax.rsqrt(jnp.mean(h * h, axis=-1, keepdims=True) + NORM_EPS)) * nf_ref[...]

    for cur in range(ring):
        @pl.when((i % ring == cur) & (i < n_main))
        def _(cur=cur):
            step(cur, hp_ref, gp_ref, op_ref)

        @pl.when((i % ring == cur) & (i >= n_main))
        def _(cur=cur):
            step(cur, hs_ref, gs_ref, os_ref)

    @pl.when(i == last)
    def _():
        for cur in range(ring):
            @pl.when(i % ring == cur)
            def _(cur=cur):
                for extra in range(1, ring):
                    wait_tile((cur + extra) % ring)


def _combine(dest, h_p, h_s, gate_p, gate_s, norm_f, yb):
    n_main = h_p.shape[0] // COMB_TT
    n_tail = h_s.shape[0] // COMB_TT
    main = lambda i, d: (jnp.minimum(i, n_main - 1), 0)
    tail = lambda i, d: (jnp.maximum(i - n_main, 0), 0)
    grid_spec = pltpu.PrefetchScalarGridSpec(
        num_scalar_prefetch=1,
        grid=(n_main + n_tail,),
        in_specs=[
            pl.BlockSpec((COMB_TT, D_MODEL), main),
            pl.BlockSpec((COMB_TT, D_MODEL), tail),
            pl.BlockSpec((SUBLANES, COMB_TT), lambda i, d: (0, jnp.minimum(i, n_main - 1))),
            pl.BlockSpec((SUBLANES, COMB_TT), lambda i, d: (0, jnp.maximum(i - n_main, 0))),
            pl.BlockSpec((1, D_MODEL), lambda i, d: (0, 0)),
            pl.BlockSpec(memory_space=pl.ANY),
        ],
        out_specs=[
            pl.BlockSpec((COMB_TT, D_MODEL), main),
            pl.BlockSpec((COMB_TT, D_MODEL), tail),
        ],
        scratch_shapes=[
            *[pltpu.VMEM((TOP_K, COMB_TT * SUBLANES, LANES), F32) for _ in range(COMB_RING)],
            pltpu.SemaphoreType.DMA((COMB_RING,)),
        ],
    )
    return pl.pallas_call(
        functools.partial(_combine_kernel, n_main, h_p.shape[0] + h_s.shape[0]),
        grid_spec=grid_spec,
        out_shape=[jax.ShapeDtypeStruct(h_p.shape, F32), jax.ShapeDtypeStruct(h_s.shape, F32)],
        compiler_params=_params(("arbitrary",)),
        name="combine",
    )(dest, h_p, h_s, gate_p, gate_s, norm_f, yb)


def kernel(x_prompt, x_sample, state_conv, state_ret, norm1, w_in, conv_w, w_a, w_b, w_o, norm2, w_router,
           b_router, w_e1, b_e1, w_e2, b_e2, norm_f):
    batch, seq_len, _ = x_prompt.shape
    n_dec, dec_seq, _ = x_sample.shape
    depth = norm1.shape[0]
    assert dec_seq == 1 and depth == 1, "single-token decode step of a one-layer trunk"
    past_len = 16384
    t_p = batch * seq_len
    n_tokens = t_p + n_dec
    region_rows = _expert_region_rows(n_tokens)
    lyr = 0
    xp = x_prompt.reshape(t_p, D_MODEL)
    xs = x_sample.reshape(n_dec, D_MODEL)
    g1 = norm1[lyr][None, :]
    w_in_b = w_in[lyr].astype(BF16)
    wa, wb, wo = w_a[lyr].astype(BF16), w_b[lyr].astype(BF16), w_o[lyr].astype(BF16)
    g2 = norm2[lyr][None, :]
    wr_t = w_router[lyr].T
    wr_hi = wr_t.astype(BF16)
    wr_split = jnp.concatenate([wr_hi, (wr_t - wr_hi.astype(F32)).astype(BF16)], axis=0)
    br_col = b_router[lyr][:, None]

    proj_p = _inproj(xp, g1, w_in_b, INPROJ_TM, INPROJ_TN, BF16)
    r_p, ret_p = _retention_prompt(proj_p, batch, seq_len)
    h_p, route_p, gate_p, conv_p, counts, dispatch = _merge(
        proj_p, r_p, xp, None, conv_w[lyr], wa, wb, wo, g2, wr_split, jnp.broadcast_to(br_col, (N_EXPERTS, MERGE_TM)),
        jnp.zeros((N_EXPERTS, LANES), F32), None, region_rows, MERGE_TM, seq_len // MERGE_TM)
    proj_s = _inproj(xs, g1, w_in_b, n_dec, INPROJ_TN_SAMPLE, F32)
    r_s, ret_s = _retention_sample(proj_s, state_ret[lyr], past_len)
    hist = (state_conv[lyr][:, 0, :], state_conv[lyr][:, 1, :])
    h_s, route_s, gate_s, u_s, counts, dispatch = _merge(
        proj_s, r_s, xs, hist, conv_w[lyr], wa, wb, wo, g2, wr_split, jnp.broadcast_to(br_col, (N_EXPERTS, n_dec)),
        counts, dispatch, region_rows, n_dec, 1)
    conv_s = jnp.stack([state_conv[lyr][:, 1, :], u_s], axis=1)

    counts_i = counts[:, 0].astype(I32)
    block_expert, block_xs, valid_rows, n_used, next_expert, row_start = _block_plan(
        counts_i, n_tokens, region_rows)
    yb = _moe_blocks(dispatch, block_expert, block_xs, valid_rows, n_used, next_expert,
                     w_e1[lyr], b_e1[lyr], w_e2[lyr], b_e2[lyr])
    route = jnp.concatenate([route_p, route_s], axis=1)
    expert_ids = jnp.arange(N_EXPERTS, dtype=I32)[:, None, None]
    first_row = jnp.sum(jnp.where(route[None, :TOP_K] == expert_ids, row_start[:, None, None], 0), axis=0)
    dest = ((first_row + route[TOP_K:]) * SUBLANES).reshape(TOP_K * n_tokens)
    y_p, y_s = _combine(dest, h_p, h_s, gate_p, gate_s, norm_f[None, :], yb)

    y_prompt = y_p.reshape(batch, seq_len, D_MODEL)
    y_sample = y_s.reshape(n_dec, 1, D_MODEL)
    return (y_prompt, y_sample, conv_p[None], ret_p[None], conv_s[None], ret_s[None])
```

```python
import functools

import jax
import jax.numpy as jnp
from jax import lax
from jax.experimental import pallas as pl
from jax.experimental.pallas import tpu as pltpu

F32 = jnp.float32
BF16 = jnp.bfloat16
I32 = jnp.int32

D_MODEL = 1024
CONV_W = 3
N_HEADS = 4
DK = 256
DV = 512
HALF = DK // 2
RET_QK = N_HEADS * DK
RET_V = N_HEADS * DV
ROPE_BASE = 10000.0
N_EXPERTS = 32
TOP_K = 4
D_FF = 1024
SWIGLU_ALPHA = 1.702
SWIGLU_LIMIT = 7.0
NORM_EPS = 1e-5
N_IN_COLS = 3 * D_MODEL + 2 * RET_QK + 2 * RET_V + 2 * D_MODEL
COL_CB, COL_CC, COL_CH = 0, D_MODEL, 2 * D_MODEL
COL_Q = 3 * D_MODEL
COL_K = COL_Q + RET_QK
COL_V = COL_K + RET_QK
COL_G = COL_V + RET_V
COL_GA = COL_G + RET_V
COL_GB = COL_GA + D_MODEL

RET_CHUNK = 256
MERGE_TM = 256
MERGE_GROUPS = 2
INPROJ_TM = 2048
INPROJ_TN = 1024
INPROJ_TN_SAMPLE = 2816
MOE_BM = 256
MOE_STEP_BLOCKS = 4
COMB_TT = 128
COMB_RING = 3
SAMPLE_GROUP = 16
LANES = 128
SUBLANES = 8
ISSUE_UNROLL = 8
DMA_QUEUES = 2
VMEM_LIMIT = 56 * 1024 * 1024


def _params(sem):
    return pltpu.CompilerParams(dimension_semantics=sem, vmem_limit_bytes=VMEM_LIMIT)


def _expert_region_rows(n_tokens):
    return -(-n_tokens // MOE_BM) * MOE_BM


def _inproj_kernel(x_ref, g_ref, w_ref, o_ref, xn_ref):
    @pl.when(pl.program_id(1) == 0)
    def _():
        x = x_ref[...]
        ms = jnp.mean(x * x, axis=-1, keepdims=True)
        xn_ref[...] = ((x * lax.rsqrt(ms + NORM_EPS)) * g_ref[...]).astype(BF16)

    o_ref[...] = jnp.dot(xn_ref[...], w_ref[...], preferred_element_type=F32).astype(o_ref.dtype)


def _inproj(x, gain, w_bf16, tm, tn, out_dtype):
    m = x.shape[0]
    n = w_bf16.shape[1]
    return pl.pallas_call(
        _inproj_kernel,
        grid=(m // tm, n // tn),
        in_specs=[
            pl.BlockSpec((tm, D_MODEL), lambda i, j: (i, 0)),
            pl.BlockSpec((1, D_MODEL), lambda i, j: (0, 0)),
            pl.BlockSpec((D_MODEL, tn), lambda i, j: (0, j)),
        ],
        out_specs=pl.BlockSpec((tm, tn), lambda i, j: (i, j)),
        out_shape=jax.ShapeDtypeStruct((m, n), out_dtype),
        scratch_shapes=[pltpu.VMEM((tm, D_MODEL), BF16)],
        compiler_params=_params(("arbitrary", "arbitrary")),
        name="inproj",
    )(x, gain, w_bf16)


def _rotary(x, cos, sin):
    x1 = x[:, :HALF]
    x2 = x[:, HALF:]
    return jnp.concatenate([x1 * cos - x2 * sin, x1 * sin + x2 * cos], axis=-1)


def _group_norm_gate(o, g):
    o = o * lax.rsqrt(jnp.mean(o * o, axis=-1, keepdims=True) + NORM_EPS)
    return (g * jax.nn.sigmoid(g)) * o


def _ret_chunk_kernel(q_ref, k_ref, v0_ref, v1_ref, g0_ref, g1_ref, cos_ref, sin_ref, mask_ref, qd_ref, kd_ref,
                      cd_ref, r_ref, s_ref):
    @pl.when(pl.program_id(1) == 0)
    def _():
        s_ref[...] = jnp.zeros_like(s_ref)

    cos = cos_ref[...]
    sin = sin_ref[...]
    v_refs = (v0_ref, v1_ref)
    g_refs = (g0_ref, g1_ref)
    for h in range(N_HEADS):
        half_cols = slice((h % 2) * DV, (h % 2 + 1) * DV)
        q = _rotary(q_ref[:, h * DK:(h + 1) * DK].astype(F32), cos, sin)
        k = _rotary(k_ref[:, h * DK:(h + 1) * DK].astype(F32), cos, sin) * (DK ** -0.5)
        v = v_refs[h // 2][:, half_cols]
        qd = jnp.concatenate([qd_ref[h], qd_ref[h]], axis=-1)
        kd = jnp.concatenate([kd_ref[h], kd_ref[h]], axis=-1)
        state = s_ref[0, h]
        scores = lax.dot_general(q.astype(BF16), k.astype(BF16), (((1,), (1,)), ((), ())),
                                 preferred_element_type=F32) * mask_ref[h]
        intra = jnp.dot(scores.astype(BF16), v, preferred_element_type=F32)
        cross = jnp.dot((q * qd).astype(BF16), state.astype(BF16), preferred_element_type=F32)
        kv = jnp.dot((k * kd).T.astype(BF16), v, preferred_element_type=F32)
        s_ref[0, h] = cd_ref[h] * state + kv
        g = g_refs[h // 2][:, half_cols].astype(F32)
        r_ref[:, h * DV:(h + 1) * DV] = _group_norm_gate(intra + cross, g).astype(r_ref.dtype)


def _retention_tables(chunk, seq_len, pos0):
    log_g = jnp.log1p(-jnp.exp2(-5.0 - jnp.arange(N_HEADS, dtype=F32)))
    pos = jnp.arange(chunk, dtype=F32)
    diff = pos[:, None] - pos[None, :]
    mask = jnp.where(diff >= 0, jnp.exp(jnp.maximum(diff, 0.0)[None] * log_g[:, None, None]), 0.0)
    q_decay = jnp.exp((pos + 1.0)[None, :] * log_g[:, None])[..., None]
    k_decay = jnp.exp((chunk - 1.0 - pos)[None, :] * log_g[:, None])[..., None]
    chunk_decay = jnp.exp(chunk * log_g)[:, None, None]
    qd = jnp.broadcast_to(q_decay, (N_HEADS, chunk, LANES))
    kd = jnp.broadcast_to(k_decay, (N_HEADS, chunk, LANES))
    cd = jnp.broadcast_to(chunk_decay, (N_HEADS, 1, DV))
    positions = pos0 + jnp.arange(seq_len, dtype=jnp.int32)
    inv = ROPE_BASE ** (-jnp.linspace(0.0, 1.0, HALF, dtype=F32))
    ang = positions.astype(F32)[:, None] * inv[None, :]
    return mask, qd, kd, cd, jnp.cos(ang), jnp.sin(ang)


def _retention_prompt(proj, batch, seq_len):
    c = RET_CHUNK
    nc = seq_len // c
    mask, qd, kd, cd, cos, sin = _retention_tables(c, seq_len, 0)
    row = lambda b, j: b * nc + j
    full3 = lambda b, j: (0, 0, 0)
    return pl.pallas_call(
        _ret_chunk_kernel,
        grid=(batch, nc),
        in_specs=[
            pl.BlockSpec((c, RET_QK), lambda b, j: (row(b, j), COL_Q // RET_QK)),
            pl.BlockSpec((c, RET_QK), lambda b, j: (row(b, j), COL_K // RET_QK)),
            pl.BlockSpec((c, RET_QK), lambda b, j: (row(b, j), COL_V // RET_QK)),
            pl.BlockSpec((c, RET_QK), lambda b, j: (row(b, j), COL_V // RET_QK + 1)),
            pl.BlockSpec((c, RET_QK), lambda b, j: (row(b, j), COL_G // RET_QK)),
            pl.BlockSpec((c, RET_QK), lambda b, j: (row(b, j), COL_G // RET_QK + 1)),
            pl.BlockSpec((c, HALF), lambda b, j: (j, 0)),
            pl.BlockSpec((c, HALF), lambda b, j: (j, 0)),
            pl.BlockSpec((N_HEADS, c, c), full3),
            pl.BlockSpec((N_HEADS, c, LANES), full3),
            pl.BlockSpec((N_HEADS, c, LANES), full3),
            pl.BlockSpec((N_HEADS, 1, DV), full3),
        ],
        out_specs=[
            pl.BlockSpec((c, RET_V), lambda b, j: (row(b, j), 0)),
            pl.BlockSpec((1, N_HEADS, DK, DV), lambda b, j: (b, 0, 0, 0)),
        ],
        out_shape=[
            jax.ShapeDtypeStruct((batch * seq_len, RET_V), BF16),
            jax.ShapeDtypeStruct((batch, N_HEADS, DK, DV), F32),
        ],
        compiler_params=_params(("arbitrary", "arbitrary")),
        name="retention_chunk",
    )(proj, proj, proj, proj, proj, proj, cos, sin, mask, qd, kd, cd)


def _ret_step_kernel(q_ref, k_ref, v_ref, g_ref, cos_ref, sin_ref, gam_ref, s_ref, r_ref, so_ref):
    cos = cos_ref[...]
    sin = sin_ref[...]
    gam = gam_ref[0]
    q = _rotary(q_ref[...], cos, sin)
    k = _rotary(k_ref[...], cos, sin) * (DK ** -0.5)
    v = v_ref[...]
    intra = jnp.sum(q * k, axis=-1, keepdims=True) * v
    q_t = (q * gam[:, :DK]).T
    k_t = k.T
    cross = []
    for s in range(SAMPLE_GROUP):
        state = s_ref[s, 0]
        cross.append(jnp.sum(q_t[:, s:s + 1] * state, axis=0, keepdims=True))
        so_ref[s, 0] = gam * state + k_t[:, s:s + 1] * v[s:s + 1, :]
    o = intra + jnp.concatenate(cross, axis=0)
    r_ref[...] = _group_norm_gate(o, g_ref[...])


def _retention_sample(proj, state, pos0):
    n = proj.shape[0]
    grp = SAMPLE_GROUP
    log_g = jnp.log1p(-jnp.exp2(-5.0 - jnp.arange(N_HEADS, dtype=F32)))
    gam = jnp.broadcast_to(jnp.exp(log_g)[:, None, None], (N_HEADS, 1, DV))
    _, _, _, _, cos, sin = _retention_tables(1, 1, pos0)
    return pl.pallas_call(
        _ret_step_kernel,
        grid=(n // grp, N_HEADS),
        in_specs=[
            pl.BlockSpec((grp, DK), lambda j, h: (j, COL_Q // DK + h)),
            pl.BlockSpec((grp, DK), lambda j, h: (j, COL_K // DK + h)),
            pl.BlockSpec((grp, DV), lambda j, h: (j, COL_V // DV + h)),
            pl.BlockSpec((grp, DV), lambda j, h: (j, COL_G // DV + h)),
            pl.BlockSpec((1, HALF), lambda j, h: (0, 0)),
            pl.BlockSpec((1, HALF), lambda j, h: (0, 0)),
            pl.BlockSpec((1, 1, DV), lambda j, h: (h, 0, 0)),
            pl.BlockSpec((grp, 1, DK, DV), lambda j, h: (j, h, 0, 0)),
        ],
        out_specs=[
            pl.BlockSpec((grp, DV), lambda j, h: (j, h)),
            pl.BlockSpec((grp, 1, DK, DV), lambda j, h: (j, h, 0, 0)),
        ],
        out_shape=[
            jax.ShapeDtypeStruct((n, RET_V), F32),
            jax.ShapeDtypeStruct(state.shape, F32),
        ],
        compiler_params=_params(("arbitrary", "arbitrary")),
        name="retention_step",
    )(proj, proj, proj, proj, cos, sin, gam, state)


def _merge_kernel(per_token_hist, chained, tiles_per_seq, region_rows, *refs):
    refs = list(refs)
    cb_ref, cc_ref, ch_ref, ga_ref, gb_ref, r_ref, x_ref = refs[:7]
    del refs[:7]
    if per_token_hist:
        h0_ref, h1_ref = refs[:2]
        del refs[:2]
    cw_ref, wa_ref, wb_ref, wo_ref, n2_ref, wr_ref, br_ref, cnt_in = refs[:8]
    del refs[:8]
    if chained:
        del refs[:1]
    h_out, route_out, gate_out, u_out, cnt_out, xs_out = refs[:6]
    del refs[:6]
    if not per_token_hist:
        hist_ref = refs.pop(0)
    cnt_ref, hn_st, dvm, dsm, sem = refs

    i = pl.program_id(0)
    last = pl.num_programs(0) - 1
    tm = x_ref.shape[0]
    cur = i % 2
    prev = 1 - cur

    def wait_scatters(p):
        for _ in range(TOP_K):
            pltpu.make_async_copy(hn_st.at[p], xs_out.at[pl.ds(0, tm * SUBLANES), :], sem.at[p]).wait()

    def dests_to_smem(p):
        return pltpu.make_async_copy(dvm, dsm.at[p], sem.at[2])

    def row_scatter(p, t, kk):
        rows = t * SUBLANES if isinstance(t, int) else pl.multiple_of(t * SUBLANES, SUBLANES)
        dst = xs_out.at[pl.ds(pl.multiple_of(dsm[p, kk, t], SUBLANES), SUBLANES), :]
        return pltpu.make_async_copy(hn_st.at[p, pl.ds(rows, SUBLANES), :], dst, sem.at[p])

    @pl.when(i == 0)
    def _():
        cnt_ref[...] = cnt_in[...]
        hn_st[1] = jnp.zeros((tm * SUBLANES, LANES), F32)
        spare = lax.broadcasted_iota(I32, (SUBLANES, tm), 0) * tm + lax.broadcasted_iota(I32, (SUBLANES, tm), 1)
        dvm[...] = (N_EXPERTS * region_rows + spare) * SUBLANES
        dests_to_smem(1).start()
        dests_to_smem(1).wait()

    if not per_token_hist:
        @pl.when(i % tiles_per_seq == 0)
        def _():
            hist_ref[...] = jnp.zeros_like(hist_ref)

    @pl.when(i > 0)
    def _():
        dests_to_smem(prev).wait()

    for t in range(tm):
        for kk in range(TOP_K):
            row_scatter(prev, t, kk).start(priority=kk % DMA_QUEUES)

    u = cc_ref[...].astype(F32) * ch_ref[...].astype(F32)
    if per_token_hist:
        u2 = h0_ref[...]
        u1 = h1_ref[...]
        u_out[...] = u
    else:
        hm2 = hist_ref[0:1, :]
        hm1 = hist_ref[1:2, :]
        row = lax.broadcasted_iota(I32, u.shape, 0)
        u1 = jnp.where(row == 0, hm1, pltpu.roll(u, 1, axis=0))
        u2 = jnp.where(row == 0, hm2, jnp.where(row == 1, hm1, pltpu.roll(u, 2, axis=0)))
        last2 = u[tm - (CONV_W - 1):, :]
        hist_ref[...] = last2
        u_out[0] = last2
    conv = cw_ref[0:1, :] * u2 + cw_ref[1:2, :] * u1 + cw_ref[2:3, :] * u
    z = (cb_ref[...].astype(F32) * conv).astype(BF16)
    groups = [slice(g * tm // MERGE_GROUPS, (g + 1) * tm // MERGE_GROUPS) for g in range(MERGE_GROUPS)]
    nt = (((1,), (1,)), ((), ()))
    o_ab = [(jnp.dot(z[rows], wa_ref[...], preferred_element_type=F32),
             jnp.dot(r_ref[rows, :].astype(BF16), wb_ref[...], preferred_element_type=F32)) for rows in groups]
    ms = []
    for rows, (o_a, o_b) in zip(groups, o_ab):
        merged = (jax.nn.sigmoid(ga_ref[rows, :].astype(F32)) * o_a
                  + jax.nn.sigmoid(gb_ref[rows, :].astype(F32)) * o_b)
        ms.append(jnp.dot(merged.astype(BF16), wo_ref[...], preferred_element_type=F32))
    hn_groups, logit_groups = [], []
    for rows, m in zip(groups, ms):
        h = x_ref[rows, :] + m
        h_out[rows, :] = h
        hn = (h * lax.rsqrt(jnp.mean(h * h, axis=-1, keepdims=True) + NORM_EPS)) * n2_ref[...]
        hn_hi = hn.astype(BF16)
        hn_lo = (hn - hn_hi.astype(F32)).astype(BF16)
        part = lax.dot_general(wr_ref[...], hn_hi, nt, preferred_element_type=F32)
        logit_groups.append(part[:N_EXPERTS] + part[N_EXPERTS:]
                            + lax.dot_general(wr_ref[:N_EXPERTS, :], hn_lo, nt, preferred_element_type=F32))
        hn_groups.append(hn)
    logits = jnp.concatenate(logit_groups, axis=1) + br_ref[...]
    expert = lax.broadcasted_iota(I32, logits.shape, 0)
    vals, idxs = [], []
    for _ in range(TOP_K):
        top = jnp.max(logits, axis=0, keepdims=True)
        arg = jnp.min(jnp.where(logits == top, expert, N_EXPERTS), axis=0, keepdims=True)
        vals.append(top)
        idxs.append(arg)
        logits = jnp.where(expert == arg, -jnp.inf, logits)
    exps = [jnp.exp(val - vals[0]) for val in vals]
    denom = exps[0] + exps[1] + exps[2] + exps[3]

    picks = [expert == idx for idx in idxs]
    chosen = jnp.zeros(logits.shape, F32)
    for pick in picks:
        chosen = jnp.where(pick, 1.0, chosen)
    before = (lax.broadcasted_iota(I32, (tm, tm), 0) < lax.broadcasted_iota(I32, (tm, tm), 1))
    prefix = jnp.dot(chosen.astype(BF16), jnp.where(before, 1.0, 0.0).astype(BF16),
                     preferred_element_type=F32)
    counts = cnt_ref[...]
    base = jnp.concatenate([counts] * (tm // LANES), axis=1) + prefix
    ranks = [jnp.sum(jnp.where(pick, base, 0.0), axis=0, keepdims=True).astype(I32) for pick in picks]
    counts = counts + jnp.sum(chosen, axis=1, keepdims=True)
    cnt_ref[...] = counts
    cnt_out[...] = counts

    slot_row = lax.broadcasted_iota(I32, (SUBLANES, tm), 0)
    route = jnp.zeros((SUBLANES, tm), I32)
    gates = jnp.zeros((SUBLANES, tm), F32)
    dests = jnp.zeros((SUBLANES, tm), I32)
    for kk in range(TOP_K):
        route = jnp.where(slot_row == kk, idxs[kk], route)
        dest = (idxs[kk] * region_rows + ranks[kk]) * SUBLANES
        route = jnp.where(slot_row == TOP_K + kk, dest, route)
        gates = jnp.where(slot_row == kk, exps[kk] / denom, gates)
        dests = jnp.where(slot_row == kk, dest, dests)
    route_out[...] = route
    gate_out[...] = gates

    @pl.when(i > 0)
    def _():
        wait_scatters(cur)

    group = tm // MERGE_GROUPS
    for g, hn in enumerate(hn_groups):
        for c in range(D_MODEL // LANES):
            hn_st[cur, pl.ds(g * group * SUBLANES + c, group, stride=SUBLANES), :] = hn[:, c * LANES:(c + 1) * LANES]
    dvm[...] = dests
    dests_to_smem(cur).start()

    @pl.when(i == last)
    def _():
        dests_to_smem(cur).wait()
        wait_scatters(prev)

        def issue(g, carry):
            for uu in range(ISSUE_UNROLL):
                for kk in range(TOP_K):
                    row_scatter(cur, g * ISSUE_UNROLL + uu, kk).start()
            return carry

        lax.fori_loop(0, tm // ISSUE_UNROLL, issue, 0)
        wait_scatters(cur)


def _merge(proj, r, x, hist, conv_w, wa, wb, wo, norm2, w_router, b_router, counts_in, xs_prev, region_rows,
           tm, tiles_per_seq):
    t = x.shape[0]
    per_token = hist is not None
    chained = xs_prev is not None
    col = lambda c: (lambda i: (i, c // D_MODEL))
    const2 = lambda i: (0, 0)
    tile = pl.BlockSpec((tm, D_MODEL), lambda i: (i, 0))
    in_specs = [
        pl.BlockSpec((tm, D_MODEL), col(COL_CB)),
        pl.BlockSpec((tm, D_MODEL), col(COL_CC)),
        pl.BlockSpec((tm, D_MODEL), col(COL_CH)),
        pl.BlockSpec((tm, D_MODEL), col(COL_GA)),
        pl.BlockSpec((tm, D_MODEL), col(COL_GB)),
        pl.BlockSpec((tm, RET_V), lambda i: (i, 0)),
        tile,
    ]
    args = [proj, proj, proj, proj, proj, r, x]
    if per_token:
        in_specs += [tile, tile]
        args += list(hist)
    in_specs += [
        pl.BlockSpec((CONV_W, D_MODEL), const2),
        pl.BlockSpec((D_MODEL, D_MODEL), const2),
        pl.BlockSpec((RET_V, D_MODEL), const2),
        pl.BlockSpec((D_MODEL, D_MODEL), const2),
        pl.BlockSpec((1, D_MODEL), const2),
        pl.BlockSpec((2 * N_EXPERTS, D_MODEL), const2),
        pl.BlockSpec((N_EXPERTS, tm), const2),
        pl.BlockSpec((N_EXPERTS, LANES), const2),
    ]
    args += [conv_w, wa, wb, wo, norm2, w_router, b_router, counts_in]
    aliases = {}
    if chained:
        aliases = {len(args): 5}
        in_specs.append(pl.BlockSpec(memory_space=pl.ANY))
        args.append(xs_prev)
    lanes_tile = pl.BlockSpec((SUBLANES, tm), lambda i: (0, i))
    if per_token:
        u_spec = tile
        u_shape = jax.ShapeDtypeStruct((t, D_MODEL), F32)
        scratch = []
    else:
        n_seq = t // (tm * tiles_per_seq)
        u_spec = pl.BlockSpec((1, CONV_W - 1, D_MODEL), lambda i: (i // tiles_per_seq, 0, 0))
        u_shape = jax.ShapeDtypeStruct((n_seq, CONV_W - 1, D_MODEL), F32)
        scratch = [pltpu.VMEM((CONV_W - 1, D_MODEL), F32)]
    out_specs = [tile, lanes_tile, lanes_tile, u_spec, pl.BlockSpec((N_EXPERTS, LANES), const2),
                 pl.BlockSpec(memory_space=pl.ANY)]
    out_shape = [
        jax.ShapeDtypeStruct((t, D_MODEL), F32),
        jax.ShapeDtypeStruct((SUBLANES, t), I32),
        jax.ShapeDtypeStruct((SUBLANES, t), F32),
        u_shape,
        jax.ShapeDtypeStruct((N_EXPERTS, LANES), F32),
        jax.ShapeDtypeStruct(((N_EXPERTS * region_rows + TOP_K * MERGE_TM) * SUBLANES, LANES), F32),
    ]
    scratch += [
        pltpu.VMEM((N_EXPERTS, LANES), F32),
        pltpu.VMEM((2, tm * SUBLANES, LANES), F32),
        pltpu.VMEM((SUBLANES, tm), I32),
        pltpu.SMEM((2, SUBLANES, tm), I32),
        pltpu.SemaphoreType.DMA((3,)),
    ]
    return pl.pallas_call(
        functools.partial(_merge_kernel, per_token, chained, tiles_per_seq, region_rows),
        grid=(t // tm,),
        in_specs=in_specs,
        out_specs=out_specs,
        out_shape=out_shape,
        scratch_shapes=scratch,
        input_output_aliases=aliases,
        compiler_params=_params(("arbitrary",)),
        name="merge_sample" if per_token else "merge_prompt",
    )(*args)


def _rows_from_tiles(ref, n_rows):
    return jnp.concatenate([ref[pl.ds(c, n_rows, stride=SUBLANES), :] for c in range(D_MODEL // LANES)], axis=1)


def _rows_to_tiles(ref, rows):
    for c in range(D_MODEL // LANES):
        ref[pl.ds(c, rows.shape[0], stride=SUBLANES), :] = rows[:, c * LANES:(c + 1) * LANES]


def _moe_kernel(be_ref, bx_ref, vr_ref, nused_ref, next_ref, *refs):
    x_refs = refs[:MOE_STEP_BLOCKS]
    w1_hbm, b1_ref, w2_hbm, b2_ref, y_hbm, w1f, w2f, w1b, w2b, obuf, sem, osem = refs[MOE_STEP_BLOCKS:]
    n_used = nused_ref[0]
    step = pl.program_id(0)
    rows = MOE_BM * SUBLANES

    def out_copy(at_step, sub):
        blk = at_step * MOE_STEP_BLOCKS + sub
        dst = y_hbm.at[pl.ds(pl.multiple_of(bx_ref[blk] * rows, rows), rows), :]
        return pltpu.make_async_copy(obuf.at[at_step % 2, sub], dst, osem.at[at_step % 2])

    def wait_outputs(at_step):
        for sub in range(MOE_STEP_BLOCKS):
            @pl.when((at_step >= 0) & (at_step * MOE_STEP_BLOCKS + sub < n_used))
            def _(sub=sub):
                out_copy(at_step, sub).wait()

    wait_outputs(step - 2)

    def weight_copies(e):
        return (pltpu.make_async_copy(w1_hbm.at[e], w1f, sem.at[0]),
                pltpu.make_async_copy(w2_hbm.at[e], w2f, sem.at[1]))

    def block(sub, x_ref):
        i = step * MOE_STEP_BLOCKS + sub
        o_view = obuf.at[step % 2, sub]

        @pl.when(i < n_used)
        def _():
            e = be_ref[i]
            prev = be_ref[jnp.maximum(i - 1, 0)]

            @pl.when(i == 0)
            def _():
                for cp in weight_copies(e):
                    cp.start()

            @pl.when((i == 0) | (e != prev))
            def _():
                for cp in weight_copies(e):
                    cp.wait()
                w1b[...] = w1f[...].astype(BF16)
                w2b[...] = w2f[...].astype(BF16)
                nxt = next_ref[e]

                @pl.when(nxt >= 0)
                def _():
                    for cp in weight_copies(nxt):
                        cp.start()

            live = lax.broadcasted_iota(I32, (MOE_BM, 1), 0) < vr_ref[i]
            x = jnp.where(live, _rows_from_tiles(x_ref, MOE_BM), 0.0).astype(BF16)
            h1 = jnp.dot(x, w1b[...], preferred_element_type=F32) + b1_ref[e]
            gate = jnp.minimum(h1[:, :D_FF], SWIGLU_LIMIT)
            up = jnp.clip(h1[:, D_FF:], -SWIGLU_LIMIT, SWIGLU_LIMIT)
            glu = gate * jax.nn.sigmoid(SWIGLU_ALPHA * gate)
            act = ((up + 1.0) * glu).astype(BF16)
            _rows_to_tiles(o_view, jnp.dot(act, w2b[...], preferred_element_type=F32) + b2_ref[e])
            out_copy(step, sub).start()

    for sub in range(MOE_STEP_BLOCKS):
        block(sub, x_refs[sub])

    @pl.when(step == pl.num_programs(0) - 1)
    def _():
        wait_outputs(step - 1)
        wait_outputs(step)


def _moe_blocks(xs, block_expert, block_xs, valid_rows, n_used, next_expert, w1, b1, w2, b2):
    n_blocks = block_expert.shape[0]
    rows = MOE_BM * SUBLANES
    x_spec = lambda sub: pl.BlockSpec(
        (rows, LANES), lambda i, be, bx, vr, nu, nx: (bx[i * MOE_STEP_BLOCKS + sub], 0))
    whole3 = lambda i, be, bx, vr, nu, nx: (0, 0, 0)
    grid_spec = pltpu.PrefetchScalarGridSpec(
        num_scalar_prefetch=5,
        grid=(n_blocks // MOE_STEP_BLOCKS,),
        in_specs=[x_spec(sub) for sub in range(MOE_STEP_BLOCKS)] + [
            pl.BlockSpec(memory_space=pl.ANY),
            pl.BlockSpec((N_EXPERTS, 1, 2 * D_FF), whole3),
            pl.BlockSpec(memory_space=pl.ANY),
            pl.BlockSpec((N_EXPERTS, 1, D_MODEL), whole3),
        ],
        out_specs=pl.BlockSpec(memory_space=pl.ANY),
        scratch_shapes=[
            pltpu.VMEM((D_MODEL, 2 * D_FF), F32),
            pltpu.VMEM((D_FF, D_MODEL), F32),
            pltpu.VMEM((D_MODEL, 2 * D_FF), BF16),
            pltpu.VMEM((D_FF, D_MODEL), BF16),
            pltpu.VMEM((2, MOE_STEP_BLOCKS, rows, LANES), F32),
            pltpu.SemaphoreType.DMA((2,)),
            pltpu.SemaphoreType.DMA((2,)),
        ],
    )
    return pl.pallas_call(
        _moe_kernel,
        grid_spec=grid_spec,
        out_shape=jax.ShapeDtypeStruct(xs.shape, F32),
        compiler_params=_params(("arbitrary",)),
        name="moe_experts",
    )(block_expert, block_xs, valid_rows, n_used, next_expert, *([xs] * MOE_STEP_BLOCKS),
      w1, b1[:, None, :], w2, b2[:, None, :])


def _block_plan(counts, n_tokens, region_rows):
    n_blocks = -(-n_tokens * TOP_K // MOE_BM) + N_EXPERTS
    n_blocks = -(-n_blocks // MOE_STEP_BLOCKS) * MOE_STEP_BLOCKS
    blocks_e = (counts + MOE_BM - 1) // MOE_BM
    blk_end = jnp.cumsum(blocks_e)
    blk_start = blk_end - blocks_e
    n_used = blk_end[-1:]
    blk = jnp.maximum(jnp.minimum(jnp.arange(n_blocks, dtype=I32), n_used - 1), 0)
    expert = jnp.minimum(jnp.sum((blk_end[None, :] <= blk[:, None]).astype(I32), axis=1), N_EXPERTS - 1)
    within = blk - blk_start[expert]
    block_xs = expert * (region_rows // MOE_BM) + within
    valid = jnp.clip(counts[expert] - within * MOE_BM, 0, MOE_BM)
    ids = jnp.arange(N_EXPERTS, dtype=I32)
    later = jnp.where((ids[None, :] > ids[:, None]) & (counts[None, :] > 0), ids[None, :], N_EXPERTS)
    next_expert = jnp.min(later, axis=1)
    next_expert = jnp.where(next_expert == N_EXPERTS, -1, next_expert)
    return expert, block_xs.astype(I32), valid.astype(I32), n_used.astype(I32), next_expert.astype(I32)


def _combine_kernel(n_main, n_tokens, dest_ref, hp_ref, hs_ref, gp_ref, gs_ref, nf_ref, yb_hbm, op_ref, os_ref,
                    *scratch):
    ybufs, sem = scratch[:-1], scratch[-1]
    i = pl.program_id(0)
    last = pl.num_programs(0) - 1
    ring = len(ybufs)
    ahead = ring - 1

    def row_copy(tile, sl, t, kk):
        row = pl.multiple_of(dest_ref[kk * n_tokens + tile * COMB_TT + t], SUBLANES)
        rows = t * SUBLANES if isinstance(t, int) else pl.multiple_of(t * SUBLANES, SUBLANES)
        return pltpu.make_async_copy(yb_hbm.at[pl.ds(row, SUBLANES), :],
                                     ybufs[sl].at[kk, pl.ds(rows, SUBLANES), :], sem.at[sl])

    def wait_tile(sl):
        for kk in range(TOP_K):
            pltpu.make_async_copy(yb_hbm.at[pl.ds(0, COMB_TT * SUBLANES), :], ybufs[sl].at[kk], sem.at[sl]).wait()

    @pl.when(i == 0)
    def _():
        def issue(g, carry):
            for uu in range(ISSUE_UNROLL):
                for kk in range(TOP_K):
                    for tile in range(ahead):
                        row_copy(jnp.minimum(tile, last), tile, g * ISSUE_UNROLL + uu, kk).start()
            return carry

        lax.fori_loop(0, COMB_TT // ISSUE_UNROLL, issue, 0)

    def step(cur, h_ref, gate_ref, o_ref):
        wait_tile(cur)
        later = jnp.minimum(i + ahead, last)
        for t in range(COMB_TT):
            for kk in range(TOP_K):
                row_copy(later, (cur + ahead) % ring, t, kk).start(priority=kk % DMA_QUEUES)
        pad = jnp.zeros((COMB_TT - SUBLANES, COMB_TT), F32)
        gates = jnp.concatenate([gate_ref[...], pad], axis=0).T
        y = gates[:, 0:1] * _rows_from_tiles(ybufs[cur].at[0], COMB_TT)
        for kk in range(1, TOP_K):
            y = y + gates[:, kk:kk + 1] * _rows_from_tiles(ybufs[cur].at[kk], COMB_TT)
        h = h_ref[...] + y
        o_ref[...] = (h * lax.rsqrt(jnp.mean(h * h, axis=-1, keepdims=True) + NORM_EPS)) * nf_ref[...]

    for cur in range(ring):
        @pl.when((i % ring == cur) & (i < n_main))
        def _(cur=cur):
            step(cur, hp_ref, gp_ref, op_ref)

        @pl.when((i % ring == cur) & (i >= n_main))
        def _(cur=cur):
            step(cur, hs_ref, gs_ref, os_ref)

    @pl.when(i == last)
    def _():
        for cur in range(ring):
            @pl.when(i % ring == cur)
            def _(cur=cur):
                for extra in range(1, ring):
                    wait_tile((cur + extra) % ring)


def _combine(dest, h_p, h_s, gate_p, gate_s, norm_f, yb):
    n_main = h_p.shape[0] // COMB_TT
    n_tail = h_s.shape[0] // COMB_TT
    main = lambda i, d: (jnp.minimum(i, n_main - 1), 0)
    tail = lambda i, d: (jnp.maximum(i - n_main, 0), 0)
    grid_spec = pltpu.PrefetchScalarGridSpec(
        num_scalar_prefetch=1,
        grid=(n_main + n_tail,),
        in_specs=[
            pl.BlockSpec((COMB_TT, D_MODEL), main),
            pl.BlockSpec((COMB_TT, D_MODEL), tail),
            pl.BlockSpec((SUBLANES, COMB_TT), lambda i, d: (0, jnp.minimum(i, n_main - 1))),
            pl.BlockSpec((SUBLANES, COMB_TT), lambda i, d: (0, jnp.maximum(i - n_main, 0))),
            pl.BlockSpec((1, D_MODEL), lambda i, d: (0, 0)),
            pl.BlockSpec(memory_space=pl.ANY),
        ],
        out_specs=[
            pl.BlockSpec((COMB_TT, D_MODEL), main),
            pl.BlockSpec((COMB_TT, D_MODEL), tail),
        ],
        scratch_shapes=[
            *[pltpu.VMEM((TOP_K, COMB_TT * SUBLANES, LANES), F32) for _ in range(COMB_RING)],
            pltpu.SemaphoreType.DMA((COMB_RING,)),
        ],
    )
    return pl.pallas_call(
        functools.partial(_combine_kernel, n_main, h_p.shape[0] + h_s.shape[0]),
        grid_spec=grid_spec,
        out_shape=[jax.ShapeDtypeStruct(h_p.shape, F32), jax.ShapeDtypeStruct(h_s.shape, F32)],
        compiler_params=_params(("arbitrary",)),
        name="combine",
    )(dest, h_p, h_s, gate_p, gate_s, norm_f, yb)


def kernel(x_prompt, x_sample, state_conv, state_ret, norm1, w_in, conv_w, w_a, w_b, w_o, norm2, w_router,
           b_router, w_e1, b_e1, w_e2, b_e2, norm_f):
    batch, seq_len, _ = x_prompt.shape
    n_dec, dec_seq, _ = x_sample.shape
    depth = norm1.shape[0]
    assert dec_seq == 1 and depth == 1, "single-token decode step of a one-layer trunk"
    past_len = 16384
    t_p = batch * seq_len
    n_tokens = t_p + n_dec
    region_rows = _expert_region_rows(n_tokens)
    lyr = 0
    xp = x_prompt.reshape(t_p, D_MODEL)
    xs = x_sample.reshape(n_dec, D_MODEL)
    g1 = norm1[lyr][None, :]
    w_in_b = w_in[lyr].astype(BF16)
    wa, wb, wo = w_a[lyr].astype(BF16), w_b[lyr].astype(BF16), w_o[lyr].astype(BF16)
    g2 = norm2[lyr][None, :]
    wr_t = w_router[lyr].T
    wr_hi = wr_t.astype(BF16)
    wr_split = jnp.concatenate([wr_hi, (wr_t - wr_hi.astype(F32)).astype(BF16)], axis=0)
    br_col = b_router[lyr][:, None]

    proj_p = _inproj(xp, g1, w_in_b, INPROJ_TM, INPROJ_TN, BF16)
    r_p, ret_p = _retention_prompt(proj_p, batch, seq_len)
    h_p, route_p, gate_p, conv_p, counts, dispatch = _merge(
        proj_p, r_p, xp, None, conv_w[lyr], wa, wb, wo, g2, wr_split, jnp.broadcast_to(br_col, (N_EXPERTS, MERGE_TM)),
        jnp.zeros((N_EXPERTS, LANES), F32), None, region_rows, MERGE_TM, seq_len // MERGE_TM)
    proj_s = _inproj(xs, g1, w_in_b, n_dec, INPROJ_TN_SAMPLE, F32)
    r_s, ret_s = _retention_sample(proj_s, state_ret[lyr], past_len)
    hist = (state_conv[lyr][:, 0, :], state_conv[lyr][:, 1, :])
    h_s, route_s, gate_s, u_s, counts, dispatch = _merge(
        proj_s, r_s, xs, hist, conv_w[lyr], wa, wb, wo, g2, wr_split, jnp.broadcast_to(br_col, (N_EXPERTS, n_dec)),
        counts, dispatch, region_rows, n_dec, 1)
    conv_s = jnp.stack([state_conv[lyr][:, 1, :], u_s], axis=1)

    counts_i = counts[:, 0].astype(I32)
    block_expert, block_xs, valid_rows, n_used, next_expert = _block_plan(counts_i, n_tokens, region_rows)
    yb = _moe_blocks(dispatch, block_expert, block_xs, valid_rows, n_used, next_expert,
                     w_e1[lyr], b_e1[lyr], w_e2[lyr], b_e2[lyr])
    dest = jnp.concatenate([route_p[TOP_K:], route_s[TOP_K:]], axis=1).reshape(TOP_K * n_tokens)
    y_p, y_s = _combine(dest, h_p, h_s, gate_p, gate_s, norm_f[None, :], yb)

    y_prompt = y_p.reshape(batch, seq_len, D_MODEL)
    y_sample = y_s.reshape(n_dec, 1, D_MODEL)
    return (y_prompt, y_sample, conv_p[None], ret_p[None], conv_s[None], ret_s[None])
```

```python
import functools

import jax
import jax.numpy as jnp
from jax import lax
from jax.experimental import pallas as pl
from jax.experimental.pallas import tpu as pltpu

F32 = jnp.float32
BF16 = jnp.bfloat16
I32 = jnp.int32

D_MODEL = 1024
CONV_W = 3
N_HEADS = 4
DK = 256
DV = 512
HALF = DK // 2
RET_QK = N_HEADS * DK
RET_V = N_HEADS * DV
ROPE_BASE = 10000.0
N_EXPERTS = 32
TOP_K = 4
D_FF = 1024
SWIGLU_ALPHA = 1.702
SWIGLU_LIMIT = 7.0
NORM_EPS = 1e-5
N_IN_COLS = 3 * D_MODEL + 2 * RET_QK + 2 * RET_V + 2 * D_MODEL
COL_CB, COL_CC, COL_CH = 0, D_MODEL, 2 * D_MODEL
COL_Q = 3 * D_MODEL
COL_K = COL_Q + RET_QK
COL_V = COL_K + RET_QK
COL_G = COL_V + RET_V
COL_GA = COL_G + RET_V
COL_GB = COL_GA + D_MODEL

RET_CHUNK = 256
MERGE_TM = 256
MERGE_GROUPS = 2
INPROJ_TM = 2048
INPROJ_TN = 1024
INPROJ_TN_SAMPLE = 2816
MOE_BM = 256
MOE_STEP_BLOCKS = 4
COMB_TT = 128
COMB_RING = 3
SAMPLE_GROUP = 16
LANES = 128
SUBLANES = 8
ISSUE_UNROLL = 8
DMA_QUEUES = 2
VMEM_LIMIT = 56 * 1024 * 1024


def _params(sem):
    return pltpu.CompilerParams(dimension_semantics=sem, vmem_limit_bytes=VMEM_LIMIT)


def _expert_region_rows(n_tokens):
    return -(-n_tokens // MOE_BM) * MOE_BM


def _inproj_kernel(x_ref, g_ref, w_ref, o_ref, xn_ref):
    @pl.when(pl.program_id(1) == 0)
    def _():
        x = x_ref[...]
        ms = jnp.mean(x * x, axis=-1, keepdims=True)
        xn_ref[...] = ((x * lax.rsqrt(ms + NORM_EPS)) * g_ref[...]).astype(BF16)

    o_ref[...] = jnp.dot(xn_ref[...], w_ref[...].astype(BF16), preferred_element_type=F32).astype(o_ref.dtype)


def _inproj(x, gain, w_bf16, tm, tn, out_dtype):
    m = x.shape[0]
    n = w_bf16.shape[1]
    return pl.pallas_call(
        _inproj_kernel,
        grid=(m // tm, n // tn),
        in_specs=[
            pl.BlockSpec((tm, D_MODEL), lambda i, j: (i, 0)),
            pl.BlockSpec((1, D_MODEL), lambda i, j: (0, 0)),
            pl.BlockSpec((D_MODEL, tn), lambda i, j: (0, j)),
        ],
        out_specs=pl.BlockSpec((tm, tn), lambda i, j: (i, j)),
        out_shape=jax.ShapeDtypeStruct((m, n), out_dtype),
        scratch_shapes=[pltpu.VMEM((tm, D_MODEL), BF16)],
        compiler_params=_params(("arbitrary", "arbitrary")),
        name="inproj",
    )(x, gain, w_bf16)


def _rotary(x, cos, sin):
    x1 = x[:, :HALF]
    x2 = x[:, HALF:]
    return jnp.concatenate([x1 * cos - x2 * sin, x1 * sin + x2 * cos], axis=-1)


def _group_norm_gate(o, g):
    o = o * lax.rsqrt(jnp.mean(o * o, axis=-1, keepdims=True) + NORM_EPS)
    return (g * jax.nn.sigmoid(g)) * o


def _ret_chunk_kernel(q_ref, k_ref, v0_ref, v1_ref, g0_ref, g1_ref, cos_ref, sin_ref, mask_ref, qd_ref, kd_ref,
                      cd_ref, r_ref, s_ref):
    @pl.when(pl.program_id(1) == 0)
    def _():
        s_ref[...] = jnp.zeros_like(s_ref)

    cos = cos_ref[...]
    sin = sin_ref[...]
    v_refs = (v0_ref, v1_ref)
    g_refs = (g0_ref, g1_ref)
    for h in range(N_HEADS):
        half_cols = slice((h % 2) * DV, (h % 2 + 1) * DV)
        q = _rotary(q_ref[:, h * DK:(h + 1) * DK].astype(F32), cos, sin)
        k = _rotary(k_ref[:, h * DK:(h + 1) * DK].astype(F32), cos, sin) * (DK ** -0.5)
        v = v_refs[h // 2][:, half_cols]
        qd = jnp.concatenate([qd_ref[h], qd_ref[h]], axis=-1)
        kd = jnp.concatenate([kd_ref[h], kd_ref[h]], axis=-1)
        state = s_ref[0, h]
        scores = lax.dot_general(q.astype(BF16), k.astype(BF16), (((1,), (1,)), ((), ())),
                                 preferred_element_type=F32) * mask_ref[h]
        intra = jnp.dot(scores.astype(BF16), v, preferred_element_type=F32)
        cross = jnp.dot((q * qd).astype(BF16), state.astype(BF16), preferred_element_type=F32)
        kv = jnp.dot((k * kd).T.astype(BF16), v, preferred_element_type=F32)
        s_ref[0, h] = cd_ref[h] * state + kv
        g = g_refs[h // 2][:, half_cols].astype(F32)
        r_ref[:, h * DV:(h + 1) * DV] = _group_norm_gate(intra + cross, g).astype(r_ref.dtype)


def _retention_tables(chunk, seq_len, pos0):
    log_g = jnp.log1p(-jnp.exp2(-5.0 - jnp.arange(N_HEADS, dtype=F32)))
    pos = jnp.arange(chunk, dtype=F32)
    diff = pos[:, None] - pos[None, :]
    mask = jnp.where(diff >= 0, jnp.exp(jnp.maximum(diff, 0.0)[None] * log_g[:, None, None]), 0.0)
    q_decay = jnp.exp((pos + 1.0)[None, :] * log_g[:, None])[..., None]
    k_decay = jnp.exp((chunk - 1.0 - pos)[None, :] * log_g[:, None])[..., None]
    chunk_decay = jnp.exp(chunk * log_g)[:, None, None]
    qd = jnp.broadcast_to(q_decay, (N_HEADS, chunk, LANES))
    kd = jnp.broadcast_to(k_decay, (N_HEADS, chunk, LANES))
    cd = jnp.broadcast_to(chunk_decay, (N_HEADS, 1, DV))
    positions = pos0 + jnp.arange(seq_len, dtype=jnp.int32)
    inv = ROPE_BASE ** (-jnp.linspace(0.0, 1.0, HALF, dtype=F32))
    ang = positions.astype(F32)[:, None] * inv[None, :]
    return mask, qd, kd, cd, jnp.cos(ang), jnp.sin(ang)


def _retention_prompt(proj, batch, seq_len):
    c = RET_CHUNK
    nc = seq_len // c
    mask, qd, kd, cd, cos, sin = _retention_tables(c, seq_len, 0)
    row = lambda b, j: b * nc + j
    full3 = lambda b, j: (0, 0, 0)
    return pl.pallas_call(
        _ret_chunk_kernel,
        grid=(batch, nc),
        in_specs=[
            pl.BlockSpec((c, RET_QK), lambda b, j: (row(b, j), COL_Q // RET_QK)),
            pl.BlockSpec((c, RET_QK), lambda b, j: (row(b, j), COL_K // RET_QK)),
            pl.BlockSpec((c, RET_QK), lambda b, j: (row(b, j), COL_V // RET_QK)),
            pl.BlockSpec((c, RET_QK), lambda b, j: (row(b, j), COL_V // RET_QK + 1)),
            pl.BlockSpec((c, RET_QK), lambda b, j: (row(b, j), COL_G // RET_QK)),
            pl.BlockSpec((c, RET_QK), lambda b, j: (row(b, j), COL_G // RET_QK + 1)),
            pl.BlockSpec((c, HALF), lambda b, j: (j, 0)),
            pl.BlockSpec((c, HALF), lambda b, j: (j, 0)),
            pl.BlockSpec((N_HEADS, c, c), full3),
            pl.BlockSpec((N_HEADS, c, LANES), full3),
            pl.BlockSpec((N_HEADS, c, LANES), full3),
            pl.BlockSpec((N_HEADS, 1, DV), full3),
        ],
        out_specs=[
            pl.BlockSpec((c, RET_V), lambda b, j: (row(b, j), 0)),
            pl.BlockSpec((1, N_HEADS, DK, DV), lambda b, j: (b, 0, 0, 0)),
        ],
        out_shape=[
            jax.ShapeDtypeStruct((batch * seq_len, RET_V), BF16),
            jax.ShapeDtypeStruct((batch, N_HEADS, DK, DV), F32),
        ],
        compiler_params=_params(("arbitrary", "arbitrary")),
        name="retention_chunk",
    )(proj, proj, proj, proj, proj, proj, cos, sin, mask, qd, kd, cd)


def _ret_step_kernel(q_ref, k_ref, v_ref, g_ref, cos_ref, sin_ref, gam_ref, s_ref, r_ref, so_ref):
    cos = cos_ref[...]
    sin = sin_ref[...]
    gam = gam_ref[0]
    q = _rotary(q_ref[...], cos, sin)
    k = _rotary(k_ref[...], cos, sin) * (DK ** -0.5)
    v = v_ref[...]
    intra = jnp.sum(q * k, axis=-1, keepdims=True) * v
    q_t = (q * gam[:, :DK]).T
    k_t = k.T
    cross = []
    for s in range(SAMPLE_GROUP):
        state = s_ref[s, 0]
        cross.append(jnp.sum(q_t[:, s:s + 1] * state, axis=0, keepdims=True))
        so_ref[s, 0] = gam * state + k_t[:, s:s + 1] * v[s:s + 1, :]
    o = intra + jnp.concatenate(cross, axis=0)
    r_ref[...] = _group_norm_gate(o, g_ref[...])


def _retention_sample(proj, state, pos0):
    n = proj.shape[0]
    grp = SAMPLE_GROUP
    log_g = jnp.log1p(-jnp.exp2(-5.0 - jnp.arange(N_HEADS, dtype=F32)))
    gam = jnp.broadcast_to(jnp.exp(log_g)[:, None, None], (N_HEADS, 1, DV))
    _, _, _, _, cos, sin = _retention_tables(1, 1, pos0)
    return pl.pallas_call(
        _ret_step_kernel,
        grid=(n // grp, N_HEADS),
        in_specs=[
            pl.BlockSpec((grp, DK), lambda j, h: (j, COL_Q // DK + h)),
            pl.BlockSpec((grp, DK), lambda j, h: (j, COL_K // DK + h)),
            pl.BlockSpec((grp, DV), lambda j, h: (j, COL_V // DV + h)),
            pl.BlockSpec((grp, DV), lambda j, h: (j, COL_G // DV + h)),
            pl.BlockSpec((1, HALF), lambda j, h: (0, 0)),
            pl.BlockSpec((1, HALF), lambda j, h: (0, 0)),
            pl.BlockSpec((1, 1, DV), lambda j, h: (h, 0, 0)),
            pl.BlockSpec((grp, 1, DK, DV), lambda j, h: (j, h, 0, 0)),
        ],
        out_specs=[
            pl.BlockSpec((grp, DV), lambda j, h: (j, h)),
            pl.BlockSpec((grp, 1, DK, DV), lambda j, h: (j, h, 0, 0)),
        ],
        out_shape=[
            jax.ShapeDtypeStruct((n, RET_V), F32),
            jax.ShapeDtypeStruct(state.shape, F32),
        ],
        compiler_params=_params(("arbitrary", "arbitrary")),
        name="retention_step",
    )(proj, proj, proj, proj, cos, sin, gam, state)


def _merge_kernel(per_token_hist, chained, tiles_per_seq, region_rows, *refs):
    refs = list(refs)
    cb_ref, cc_ref, ch_ref, ga_ref, gb_ref, r_ref, x_ref = refs[:7]
    del refs[:7]
    if per_token_hist:
        h0_ref, h1_ref = refs[:2]
        del refs[:2]
    cw_ref, wa_ref, wb_ref, wo_ref, n2_ref, wr_ref, br_ref, cnt_in = refs[:8]
    del refs[:8]
    if chained:
        del refs[:1]
    h_out, route_out, gate_out, u_out, cnt_out, xs_out = refs[:6]
    del refs[:6]
    if not per_token_hist:
        hist_ref = refs.pop(0)
    cnt_ref, hn_st, dvm, dsm, sem = refs

    i = pl.program_id(0)
    last = pl.num_programs(0) - 1
    tm = x_ref.shape[0]
    cur = i % 2
    prev = 1 - cur

    def wait_scatters(p):
        for _ in range(TOP_K):
            pltpu.make_async_copy(hn_st.at[p], xs_out.at[pl.ds(0, tm * SUBLANES), :], sem.at[p]).wait()

    def dests_to_smem(p):
        return pltpu.make_async_copy(dvm, dsm.at[p], sem.at[2])

    def row_scatter(p, t, kk):
        rows = t * SUBLANES if isinstance(t, int) else pl.multiple_of(t * SUBLANES, SUBLANES)
        dst = xs_out.at[pl.ds(pl.multiple_of(dsm[p, kk, t], SUBLANES), SUBLANES), :]
        return pltpu.make_async_copy(hn_st.at[p, pl.ds(rows, SUBLANES), :], dst, sem.at[p])

    @pl.when(i == 0)
    def _():
        cnt_ref[...] = cnt_in[...]
        hn_st[1] = jnp.zeros((tm * SUBLANES, LANES), F32)
        spare = lax.broadcasted_iota(I32, (SUBLANES, tm), 0) * tm + lax.broadcasted_iota(I32, (SUBLANES, tm), 1)
        dvm[...] = (N_EXPERTS * region_rows + spare) * SUBLANES
        dests_to_smem(1).start()
        dests_to_smem(1).wait()

    if not per_token_hist:
        @pl.when(i % tiles_per_seq == 0)
        def _():
            hist_ref[...] = jnp.zeros_like(hist_ref)

    @pl.when(i > 0)
    def _():
        dests_to_smem(prev).wait()

    for t in range(tm):
        for kk in range(TOP_K):
            row_scatter(prev, t, kk).start(priority=kk % DMA_QUEUES)

    u = cc_ref[...].astype(F32) * ch_ref[...].astype(F32)
    if per_token_hist:
        u2 = h0_ref[...]
        u1 = h1_ref[...]
        u_out[...] = u
    else:
        hm2 = hist_ref[0:1, :]
        hm1 = hist_ref[1:2, :]
        row = lax.broadcasted_iota(I32, u.shape, 0)
        u1 = jnp.where(row == 0, hm1, pltpu.roll(u, 1, axis=0))
        u2 = jnp.where(row == 0, hm2, jnp.where(row == 1, hm1, pltpu.roll(u, 2, axis=0)))
        last2 = u[tm - (CONV_W - 1):, :]
        hist_ref[...] = last2
        u_out[0] = last2
    conv = cw_ref[0:1, :] * u2 + cw_ref[1:2, :] * u1 + cw_ref[2:3, :] * u
    z = (cb_ref[...].astype(F32) * conv).astype(BF16)
    groups = [slice(g * tm // MERGE_GROUPS, (g + 1) * tm // MERGE_GROUPS) for g in range(MERGE_GROUPS)]
    nt = (((1,), (1,)), ((), ()))
    o_ab = [(jnp.dot(z[rows], wa_ref[...], preferred_element_type=F32),
             jnp.dot(r_ref[rows, :].astype(BF16), wb_ref[...], preferred_element_type=F32)) for rows in groups]
    ms = []
    for rows, (o_a, o_b) in zip(groups, o_ab):
        merged = (jax.nn.sigmoid(ga_ref[rows, :].astype(F32)) * o_a
                  + jax.nn.sigmoid(gb_ref[rows, :].astype(F32)) * o_b)
        ms.append(jnp.dot(merged.astype(BF16), wo_ref[...], preferred_element_type=F32))
    hn_groups, logit_groups = [], []
    for rows, m in zip(groups, ms):
        h = x_ref[rows, :] + m
        h_out[rows, :] = h
        hn = (h * lax.rsqrt(jnp.mean(h * h, axis=-1, keepdims=True) + NORM_EPS)) * n2_ref[...]
        hn_hi = hn.astype(BF16)
        hn_lo = (hn - hn_hi.astype(F32)).astype(BF16)
        part = lax.dot_general(wr_ref[...], hn_hi, nt, preferred_element_type=F32)
        logit_groups.append(part[:N_EXPERTS] + part[N_EXPERTS:]
                            + lax.dot_general(wr_ref[:N_EXPERTS, :], hn_lo, nt, preferred_element_type=F32))
        hn_groups.append(hn)
    logits = jnp.concatenate(logit_groups, axis=1) + br_ref[...]
    expert = lax.broadcasted_iota(I32, logits.shape, 0)
    vals, idxs = [], []
    for _ in range(TOP_K):
        top = jnp.max(logits, axis=0, keepdims=True)
        arg = jnp.min(jnp.where(logits == top, expert, N_EXPERTS), axis=0, keepdims=True)
        vals.append(top)
        idxs.append(arg)
        logits = jnp.where(expert == arg, -jnp.inf, logits)
    exps = [jnp.exp(val - vals[0]) for val in vals]
    denom = exps[0] + exps[1] + exps[2] + exps[3]

    picks = [expert == idx for idx in idxs]
    chosen = jnp.zeros(logits.shape, F32)
    for pick in picks:
        chosen = jnp.where(pick, 1.0, chosen)
    before = (lax.broadcasted_iota(I32, (tm, tm), 0) < lax.broadcasted_iota(I32, (tm, tm), 1))
    prefix = jnp.dot(chosen.astype(BF16), jnp.where(before, 1.0, 0.0).astype(BF16),
                     preferred_element_type=F32)
    counts = cnt_ref[...]
    base = jnp.concatenate([counts] * (tm // LANES), axis=1) + prefix
    ranks = [jnp.sum(jnp.where(pick, base, 0.0), axis=0, keepdims=True).astype(I32) for pick in picks]
    counts = counts + jnp.sum(chosen, axis=1, keepdims=True)
    cnt_ref[...] = counts
    cnt_out[...] = counts

    slot_row = lax.broadcasted_iota(I32, (SUBLANES, tm), 0)
    route = jnp.zeros((SUBLANES, tm), I32)
    gates = jnp.zeros((SUBLANES, tm), F32)
    dests = jnp.zeros((SUBLANES, tm), I32)
    for kk in range(TOP_K):
        route = jnp.where(slot_row == kk, idxs[kk], route)
        dest = (idxs[kk] * region_rows + ranks[kk]) * SUBLANES
        route = jnp.where(slot_row == TOP_K + kk, dest, route)
        gates = jnp.where(slot_row == kk, exps[kk] / denom, gates)
        dests = jnp.where(slot_row == kk, dest, dests)
    route_out[...] = route
    gate_out[...] = gates

    @pl.when(i > 0)
    def _():
        wait_scatters(cur)

    group = tm // MERGE_GROUPS
    for g, hn in enumerate(hn_groups):
        for c in range(D_MODEL // LANES):
            hn_st[cur, pl.ds(g * group * SUBLANES + c, group, stride=SUBLANES), :] = hn[:, c * LANES:(c + 1) * LANES]
    dvm[...] = dests
    dests_to_smem(cur).start()

    @pl.when(i == last)
    def _():
        dests_to_smem(cur).wait()
        wait_scatters(prev)

        def issue(g, carry):
            for uu in range(ISSUE_UNROLL):
                for kk in range(TOP_K):
                    row_scatter(cur, g * ISSUE_UNROLL + uu, kk).start()
            return carry

        lax.fori_loop(0, tm // ISSUE_UNROLL, issue, 0)
        wait_scatters(cur)


def _merge(proj, r, x, hist, conv_w, wa, wb, wo, norm2, w_router, b_router, counts_in, xs_prev, region_rows,
           tm, tiles_per_seq):
    t = x.shape[0]
    per_token = hist is not None
    chained = xs_prev is not None
    col = lambda c: (lambda i: (i, c // D_MODEL))
    const2 = lambda i: (0, 0)
    tile = pl.BlockSpec((tm, D_MODEL), lambda i: (i, 0))
    in_specs = [
        pl.BlockSpec((tm, D_MODEL), col(COL_CB)),
        pl.BlockSpec((tm, D_MODEL), col(COL_CC)),
        pl.BlockSpec((tm, D_MODEL), col(COL_CH)),
        pl.BlockSpec((tm, D_MODEL), col(COL_GA)),
        pl.BlockSpec((tm, D_MODEL), col(COL_GB)),
        pl.BlockSpec((tm, RET_V), lambda i: (i, 0)),
        tile,
    ]
    args = [proj, proj, proj, proj, proj, r, x]
    if per_token:
        in_specs += [tile, tile]
        args += list(hist)
    in_specs += [
        pl.BlockSpec((CONV_W, D_MODEL), const2),
        pl.BlockSpec((D_MODEL, D_MODEL), const2),
        pl.BlockSpec((RET_V, D_MODEL), const2),
        pl.BlockSpec((D_MODEL, D_MODEL), const2),
        pl.BlockSpec((1, D_MODEL), const2),
        pl.BlockSpec((2 * N_EXPERTS, D_MODEL), const2),
        pl.BlockSpec((N_EXPERTS, tm), const2),
        pl.BlockSpec((N_EXPERTS, LANES), const2),
    ]
    args += [conv_w, wa, wb, wo, norm2, w_router, b_router, counts_in]
    aliases = {}
    if chained:
        aliases = {len(args): 5}
        in_specs.append(pl.BlockSpec(memory_space=pl.ANY))
        args.append(xs_prev)
    lanes_tile = pl.BlockSpec((SUBLANES, tm), lambda i: (0, i))
    if per_token:
        u_spec = tile
        u_shape = jax.ShapeDtypeStruct((t, D_MODEL), F32)
        scratch = []
    else:
        n_seq = t // (tm * tiles_per_seq)
        u_spec = pl.BlockSpec((1, CONV_W - 1, D_MODEL), lambda i: (i // tiles_per_seq, 0, 0))
        u_shape = jax.ShapeDtypeStruct((n_seq, CONV_W - 1, D_MODEL), F32)
        scratch = [pltpu.VMEM((CONV_W - 1, D_MODEL), F32)]
    out_specs = [tile, lanes_tile, lanes_tile, u_spec, pl.BlockSpec((N_EXPERTS, LANES), const2),
                 pl.BlockSpec(memory_space=pl.ANY)]
    out_shape = [
        jax.ShapeDtypeStruct((t, D_MODEL), F32),
        jax.ShapeDtypeStruct((SUBLANES, t), I32),
        jax.ShapeDtypeStruct((SUBLANES, t), F32),
        u_shape,
        jax.ShapeDtypeStruct((N_EXPERTS, LANES), F32),
        jax.ShapeDtypeStruct(((N_EXPERTS * region_rows + TOP_K * MERGE_TM) * SUBLANES, LANES), F32),
    ]
    scratch += [
        pltpu.VMEM((N_EXPERTS, LANES), F32),
        pltpu.VMEM((2, tm * SUBLANES, LANES), F32),
        pltpu.VMEM((SUBLANES, tm), I32),
        pltpu.SMEM((2, SUBLANES, tm), I32),
        pltpu.SemaphoreType.DMA((3,)),
    ]
    return pl.pallas_call(
        functools.partial(_merge_kernel, per_token, chained, tiles_per_seq, region_rows),
        grid=(t // tm,),
        in_specs=in_specs,
        out_specs=out_specs,
        out_shape=out_shape,
        scratch_shapes=scratch,
        input_output_aliases=aliases,
        compiler_params=_params(("arbitrary",)),
        name="merge_sample" if per_token else "merge_prompt",
    )(*args)


def _rows_from_tiles(ref, n_rows):
    return jnp.concatenate([ref[pl.ds(c, n_rows, stride=SUBLANES), :] for c in range(D_MODEL // LANES)], axis=1)


def _rows_to_tiles(ref, rows):
    for c in range(D_MODEL // LANES):
        ref[pl.ds(c, rows.shape[0], stride=SUBLANES), :] = rows[:, c * LANES:(c + 1) * LANES]


def _moe_kernel(be_ref, bx_ref, vr_ref, nused_ref, next_ref, *refs):
    x_refs = refs[:MOE_STEP_BLOCKS]
    w1_hbm, b1_ref, w2_hbm, b2_ref, y_hbm, w1f, w2f, w1b, w2b, obuf, sem, osem = refs[MOE_STEP_BLOCKS:]
    n_used = nused_ref[0]
    step = pl.program_id(0)
    rows = MOE_BM * SUBLANES

    def out_copy(at_step, sub):
        blk = at_step * MOE_STEP_BLOCKS + sub
        dst = y_hbm.at[pl.ds(pl.multiple_of(bx_ref[blk] * rows, rows), rows), :]
        return pltpu.make_async_copy(obuf.at[at_step % 2, sub], dst, osem.at[at_step % 2])

    def wait_outputs(at_step):
        for sub in range(MOE_STEP_BLOCKS):
            @pl.when((at_step >= 0) & (at_step * MOE_STEP_BLOCKS + sub < n_used))
            def _(sub=sub):
                out_copy(at_step, sub).wait()

    wait_outputs(step - 2)

    def weight_copies(e):
        return (pltpu.make_async_copy(w1_hbm.at[e], w1f, sem.at[0]),
                pltpu.make_async_copy(w2_hbm.at[e], w2f, sem.at[1]))

    def block(sub, x_ref):
        i = step * MOE_STEP_BLOCKS + sub
        o_view = obuf.at[step % 2, sub]

        @pl.when(i < n_used)
        def _():
            e = be_ref[i]
            prev = be_ref[jnp.maximum(i - 1, 0)]

            @pl.when(i == 0)
            def _():
                for cp in weight_copies(e):
                    cp.start()

            @pl.when((i == 0) | (e != prev))
            def _():
                for cp in weight_copies(e):
                    cp.wait()
                w1b[...] = w1f[...].astype(BF16)
                w2b[...] = w2f[...].astype(BF16)
                nxt = next_ref[e]

                @pl.when(nxt >= 0)
                def _():
                    for cp in weight_copies(nxt):
                        cp.start()

            live = lax.broadcasted_iota(I32, (MOE_BM, 1), 0) < vr_ref[i]
            x = jnp.where(live, _rows_from_tiles(x_ref, MOE_BM), 0.0).astype(BF16)
            h1 = jnp.dot(x, w1b[...], preferred_element_type=F32) + b1_ref[e]
            gate = jnp.minimum(h1[:, :D_FF], SWIGLU_LIMIT)
            up = jnp.clip(h1[:, D_FF:], -SWIGLU_LIMIT, SWIGLU_LIMIT)
            glu = gate * jax.nn.sigmoid(SWIGLU_ALPHA * gate)
            act = ((up + 1.0) * glu).astype(BF16)
            _rows_to_tiles(o_view, jnp.dot(act, w2b[...], preferred_element_type=F32) + b2_ref[e])
            out_copy(step, sub).start()

    for sub in range(MOE_STEP_BLOCKS):
        block(sub, x_refs[sub])

    @pl.when(step == pl.num_programs(0) - 1)
    def _():
        wait_outputs(step - 1)
        wait_outputs(step)


def _moe_blocks(xs, block_expert, block_xs, valid_rows, n_used, next_expert, w1, b1, w2, b2):
    n_blocks = block_expert.shape[0]
    rows = MOE_BM * SUBLANES
    x_spec = lambda sub: pl.BlockSpec(
        (rows, LANES), lambda i, be, bx, vr, nu, nx: (bx[i * MOE_STEP_BLOCKS + sub], 0))
    whole3 = lambda i, be, bx, vr, nu, nx: (0, 0, 0)
    grid_spec = pltpu.PrefetchScalarGridSpec(
        num_scalar_prefetch=5,
        grid=(n_blocks // MOE_STEP_BLOCKS,),
        in_specs=[x_spec(sub) for sub in range(MOE_STEP_BLOCKS)] + [
            pl.BlockSpec(memory_space=pl.ANY),
            pl.BlockSpec((N_EXPERTS, 1, 2 * D_FF), whole3),
            pl.BlockSpec(memory_space=pl.ANY),
            pl.BlockSpec((N_EXPERTS, 1, D_MODEL), whole3),
        ],
        out_specs=pl.BlockSpec(memory_space=pl.ANY),
        scratch_shapes=[
            pltpu.VMEM((D_MODEL, 2 * D_FF), F32),
            pltpu.VMEM((D_FF, D_MODEL), F32),
            pltpu.VMEM((D_MODEL, 2 * D_FF), BF16),
            pltpu.VMEM((D_FF, D_MODEL), BF16),
            pltpu.VMEM((2, MOE_STEP_BLOCKS, rows, LANES), F32),
            pltpu.SemaphoreType.DMA((2,)),
            pltpu.SemaphoreType.DMA((2,)),
        ],
    )
    return pl.pallas_call(
        _moe_kernel,
        grid_spec=grid_spec,
        out_shape=jax.ShapeDtypeStruct(xs.shape, F32),
        compiler_params=_params(("arbitrary",)),
        name="moe_experts",
    )(block_expert, block_xs, valid_rows, n_used, next_expert, *([xs] * MOE_STEP_BLOCKS),
      w1, b1[:, None, :], w2, b2[:, None, :])


def _block_plan(counts, n_tokens, region_rows):
    n_blocks = -(-n_tokens * TOP_K // MOE_BM) + N_EXPERTS
    n_blocks = -(-n_blocks // MOE_STEP_BLOCKS) * MOE_STEP_BLOCKS
    blocks_e = (counts + MOE_BM - 1) // MOE_BM
    blk_end = jnp.cumsum(blocks_e)
    blk_start = blk_end - blocks_e
    n_used = blk_end[-1:]
    blk = jnp.maximum(jnp.minimum(jnp.arange(n_blocks, dtype=I32), n_used - 1), 0)
    expert = jnp.minimum(jnp.sum((blk_end[None, :] <= blk[:, None]).astype(I32), axis=1), N_EXPERTS - 1)
    ids = jnp.arange(N_EXPERTS, dtype=I32)
    mine = ids[None, :] == expert[:, None]
    within = blk - jnp.sum(jnp.where(mine, blk_start[None, :], 0), axis=1)
    block_xs = expert * (region_rows // MOE_BM) + within
    valid = jnp.clip(jnp.sum(jnp.where(mine, counts[None, :], 0), axis=1) - within * MOE_BM, 0, MOE_BM)
    later = jnp.where((ids[None, :] > ids[:, None]) & (counts[None, :] > 0), ids[None, :], N_EXPERTS)
    next_expert = jnp.min(later, axis=1)
    next_expert = jnp.where(next_expert == N_EXPERTS, -1, next_expert)
    return expert, block_xs.astype(I32), valid.astype(I32), n_used.astype(I32), next_expert.astype(I32)


def _combine_kernel(n_main, n_tokens, dest_ref, hp_ref, hs_ref, gp_ref, gs_ref, nf_ref, yb_hbm, op_ref, os_ref,
                    *scratch):
    ybufs, sem = scratch[:-1], scratch[-1]
    i = pl.program_id(0)
    last = pl.num_programs(0) - 1
    ring = len(ybufs)
    ahead = ring - 1

    def row_copy(tile, sl, t, kk):
        row = pl.multiple_of(dest_ref[kk * n_tokens + tile * COMB_TT + t], SUBLANES)
        rows = t * SUBLANES if isinstance(t, int) else pl.multiple_of(t * SUBLANES, SUBLANES)
        return pltpu.make_async_copy(yb_hbm.at[pl.ds(row, SUBLANES), :],
                                     ybufs[sl].at[kk, pl.ds(rows, SUBLANES), :], sem.at[sl])

    def wait_tile(sl):
        for kk in range(TOP_K):
            pltpu.make_async_copy(yb_hbm.at[pl.ds(0, COMB_TT * SUBLANES), :], ybufs[sl].at[kk], sem.at[sl]).wait()

    @pl.when(i == 0)
    def _():
        def issue(g, carry):
            for uu in range(ISSUE_UNROLL):
                for kk in range(TOP_K):
                    for tile in range(ahead):
                        row_copy(jnp.minimum(tile, last), tile, g * ISSUE_UNROLL + uu, kk).start()
            return carry

        lax.fori_loop(0, COMB_TT // ISSUE_UNROLL, issue, 0)

    def step(cur, h_ref, gate_ref, o_ref):
        wait_tile(cur)
        later = jnp.minimum(i + ahead, last)
        for t in range(COMB_TT):
            for kk in range(TOP_K):
                row_copy(later, (cur + ahead) % ring, t, kk).start(priority=kk % DMA_QUEUES)
        pad = jnp.zeros((COMB_TT - SUBLANES, COMB_TT), F32)
        gates = jnp.concatenate([gate_ref[...], pad], axis=0).T
        y = gates[:, 0:1] * _rows_from_tiles(ybufs[cur].at[0], COMB_TT)
        for kk in range(1, TOP_K):
            y = y + gates[:, kk:kk + 1] * _rows_from_tiles(ybufs[cur].at[kk], COMB_TT)
        h = h_ref[...] + y
        o_ref[...] = (h * lax.rsqrt(jnp.mean(h * h, axis=-1, keepdims=True) + NORM_EPS)) * nf_ref[...]

    for cur in range(ring):
        @pl.when((i % ring == cur) & (i < n_main))
        def _(cur=cur):
            step(cur, hp_ref, gp_ref, op_ref)

        @pl.when((i % ring == cur) & (i >= n_main))
        def _(cur=cur):
            step(cur, hs_ref, gs_ref, os_ref)

    @pl.when(i == last)
    def _():
        for cur in range(ring):
            @pl.when(i % ring == cur)
            def _(cur=cur):
                for extra in range(1, ring):
                    wait_tile((cur + extra) % ring)


def _combine(dest, h_p, h_s, gate_p, gate_s, norm_f, yb):
    n_main = h_p.shape[0] // COMB_TT
    n_tail = h_s.shape[0] // COMB_TT
    main = lambda i, d: (jnp.minimum(i, n_main - 1), 0)
    tail = lambda i, d: (jnp.maximum(i - n_main, 0), 0)
    grid_spec = pltpu.PrefetchScalarGridSpec(
        num_scalar_prefetch=1,
        grid=(n_main + n_tail,),
        in_specs=[
            pl.BlockSpec((COMB_TT, D_MODEL), main),
            pl.BlockSpec((COMB_TT, D_MODEL), tail),
            pl.BlockSpec((SUBLANES, COMB_TT), lambda i, d: (0, jnp.minimum(i, n_main - 1))),
            pl.BlockSpec((SUBLANES, COMB_TT), lambda i, d: (0, jnp.maximum(i - n_main, 0))),
            pl.BlockSpec((1, D_MODEL), lambda i, d: (0, 0)),
            pl.BlockSpec(memory_space=pl.ANY),
        ],
        out_specs=[
            pl.BlockSpec((COMB_TT, D_MODEL), main),
            pl.BlockSpec((COMB_TT, D_MODEL), tail),
        ],
        scratch_shapes=[
            *[pltpu.VMEM((TOP_K, COMB_TT * SUBLANES, LANES), F32) for _ in range(COMB_RING)],
            pltpu.SemaphoreType.DMA((COMB_RING,)),
        ],
    )
    return pl.pallas_call(
        functools.partial(_combine_kernel, n_main, h_p.shape[0] + h_s.shape[0]),
        grid_spec=grid_spec,
        out_shape=[jax.ShapeDtypeStruct(h_p.shape, F32), jax.ShapeDtypeStruct(h_s.shape, F32)],
        compiler_params=_params(("arbitrary",)),
        name="combine",
    )(dest, h_p, h_s, gate_p, gate_s, norm_f, yb)


def kernel(x_prompt, x_sample, state_conv, state_ret, norm1, w_in, conv_w, w_a, w_b, w_o, norm2, w_router,
           b_router, w_e1, b_e1, w_e2, b_e2, norm_f):
    batch, seq_len, _ = x_prompt.shape
    n_dec, dec_seq, _ = x_sample.shape
    depth = norm1.shape[0]
    assert dec_seq == 1 and depth == 1, "single-token decode step of a one-layer trunk"
    past_len = 16384
    t_p = batch * seq_len
    n_tokens = t_p + n_dec
    region_rows = _expert_region_rows(n_tokens)
    lyr = 0
    xp = x_prompt.reshape(t_p, D_MODEL)
    xs = x_sample.reshape(n_dec, D_MODEL)
    g1 = norm1[lyr][None, :]
    w_in_b = w_in[lyr]
    wa, wb, wo = w_a[lyr].astype(BF16), w_b[lyr].astype(BF16), w_o[lyr].astype(BF16)
    g2 = norm2[lyr][None, :]
    wr_t = w_router[lyr].T
    wr_hi = wr_t.astype(BF16)
    wr_split = jnp.concatenate([wr_hi, (wr_t - wr_hi.astype(F32)).astype(BF16)], axis=0)
    br_col = b_router[lyr][:, None]

    proj_p = _inproj(xp, g1, w_in_b, INPROJ_TM, INPROJ_TN, BF16)
    r_p, ret_p = _retention_prompt(proj_p, batch, seq_len)
    h_p, route_p, gate_p, conv_p, counts, dispatch = _merge(
        proj_p, r_p, xp, None, conv_w[lyr], wa, wb, wo, g2, wr_split, jnp.broadcast_to(br_col, (N_EXPERTS, MERGE_TM)),
        jnp.zeros((N_EXPERTS, LANES), F32), None, region_rows, MERGE_TM, seq_len // MERGE_TM)
    proj_s = _inproj(xs, g1, w_in_b, n_dec, INPROJ_TN_SAMPLE, F32)
    r_s, ret_s = _retention_sample(proj_s, state_ret[lyr], past_len)
    hist = (state_conv[lyr][:, 0, :], state_conv[lyr][:, 1, :])
    h_s, route_s, gate_s, u_s, counts, dispatch = _merge(
        proj_s, r_s, xs, hist, conv_w[lyr], wa, wb, wo, g2, wr_split, jnp.broadcast_to(br_col, (N_EXPERTS, n_dec)),
        counts, dispatch, region_rows, n_dec, 1)
    conv_s = jnp.stack([state_conv[lyr][:, 1, :], u_s], axis=1)

    counts_i = counts[:, 0].astype(I32)
    block_expert, block_xs, valid_rows, n_used, next_expert = _block_plan(counts_i, n_tokens, region_rows)
    yb = _moe_blocks(dispatch, block_expert, block_xs, valid_rows, n_used, next_expert,
                     w_e1[lyr], b_e1[lyr], w_e2[lyr], b_e2[lyr])
    dest = jnp.concatenate([route_p[TOP_K:], route_s[TOP_K:]], axis=1).reshape(TOP_K * n_tokens)
    y_p, y_s = _combine(dest, h_p, h_s, gate_p, gate_s, norm_f[None, :], yb)

    y_prompt = y_p.reshape(batch, seq_len, D_MODEL)
    y_sample = y_s.reshape(n_dec, 1, D_MODEL)
    return (y_prompt, y_sample, conv_p[None], ret_p[None], conv_s[None], ret_s[None])
```

```python
import functools

import jax
import jax.numpy as jnp
from jax import lax
from jax.experimental import pallas as pl
from jax.experimental.pallas import tpu as pltpu

F32 = jnp.float32
BF16 = jnp.bfloat16
I32 = jnp.int32

D_MODEL = 1024
CONV_W = 3
N_HEADS = 4
DK = 256
DV = 512
HALF = DK // 2
RET_QK = N_HEADS * DK
RET_V = N_HEADS * DV
ROPE_BASE = 10000.0
N_EXPERTS = 32
TOP_K = 4
D_FF = 1024
SWIGLU_ALPHA = 1.702
SWIGLU_LIMIT = 7.0
NORM_EPS = 1e-5
N_IN_COLS = 3 * D_MODEL + 2 * RET_QK + 2 * RET_V + 2 * D_MODEL
COL_CB, COL_CC, COL_CH = 0, D_MODEL, 2 * D_MODEL
COL_Q = 3 * D_MODEL
COL_K = COL_Q + RET_QK
COL_V = COL_K + RET_QK
COL_G = COL_V + RET_V
COL_GA = COL_G + RET_V
COL_GB = COL_GA + D_MODEL

RET_CHUNK = 256
MERGE_TM = 256
MERGE_GROUPS = 2
INPROJ_TM = 2048
INPROJ_TN = 1024
INPROJ_TN_SAMPLE = 2816
MOE_BM = 256
MOE_STEP_BLOCKS = 4
COMB_TT = 128
COMB_RING = 3
SAMPLE_GROUP = 16
LANES = 128
SUBLANES = 8
ISSUE_UNROLL = 8
DMA_QUEUES = 2
VMEM_LIMIT = 56 * 1024 * 1024


def _params(sem):
    return pltpu.CompilerParams(dimension_semantics=sem, vmem_limit_bytes=VMEM_LIMIT)


def _expert_region_rows(n_tokens):
    return -(-n_tokens // MOE_BM) * MOE_BM


def _inproj_kernel(x_ref, g_ref, w_ref, o_ref, xn_ref):
    @pl.when(pl.program_id(1) == 0)
    def _():
        x = x_ref[...]
        ms = jnp.mean(x * x, axis=-1, keepdims=True)
        xn_ref[...] = ((x * lax.rsqrt(ms + NORM_EPS)) * g_ref[...]).astype(BF16)

    o_ref[...] = jnp.dot(xn_ref[...], w_ref[...].astype(BF16), preferred_element_type=F32).astype(o_ref.dtype)


def _inproj(x, gain, w_bf16, tm, tn, out_dtype):
    m = x.shape[0]
    n = w_bf16.shape[1]
    return pl.pallas_call(
        _inproj_kernel,
        grid=(m // tm, n // tn),
        in_specs=[
            pl.BlockSpec((tm, D_MODEL), lambda i, j: (i, 0)),
            pl.BlockSpec((1, D_MODEL), lambda i, j: (0, 0)),
            pl.BlockSpec((D_MODEL, tn), lambda i, j: (0, j)),
        ],
        out_specs=pl.BlockSpec((tm, tn), lambda i, j: (i, j)),
        out_shape=jax.ShapeDtypeStruct((m, n), out_dtype),
        scratch_shapes=[pltpu.VMEM((tm, D_MODEL), BF16)],
        compiler_params=_params(("arbitrary", "arbitrary")),
        name="inproj",
    )(x, gain, w_bf16)


def _rotary(x, cos, sin):
    x1 = x[:, :HALF]
    x2 = x[:, HALF:]
    return jnp.concatenate([x1 * cos - x2 * sin, x1 * sin + x2 * cos], axis=-1)


def _group_norm_gate(o, g):
    o = o * lax.rsqrt(jnp.mean(o * o, axis=-1, keepdims=True) + NORM_EPS)
    return (g * jax.nn.sigmoid(g)) * o


def _ret_chunk_kernel(q_ref, k_ref, v0_ref, v1_ref, g0_ref, g1_ref, cos_ref, sin_ref, mask_ref, qd_ref, kd_ref,
                      cd_ref, r_ref, s_ref):
    @pl.when(pl.program_id(1) == 0)
    def _():
        s_ref[...] = jnp.zeros_like(s_ref)

    cos = cos_ref[...]
    sin = sin_ref[...]
    v_refs = (v0_ref, v1_ref)
    g_refs = (g0_ref, g1_ref)
    for h in range(N_HEADS):
        half_cols = slice((h % 2) * DV, (h % 2 + 1) * DV)
        q = _rotary(q_ref[:, h * DK:(h + 1) * DK].astype(F32), cos, sin)
        k = _rotary(k_ref[:, h * DK:(h + 1) * DK].astype(F32), cos, sin) * (DK ** -0.5)
        v = v_refs[h // 2][:, half_cols]
        qd = jnp.concatenate([qd_ref[h], qd_ref[h]], axis=-1)
        kd = jnp.concatenate([kd_ref[h], kd_ref[h]], axis=-1)
        state = s_ref[0, h]
        scores = lax.dot_general(q.astype(BF16), k.astype(BF16), (((1,), (1,)), ((), ())),
                                 preferred_element_type=F32) * mask_ref[h]
        intra = jnp.dot(scores.astype(BF16), v, preferred_element_type=F32)
        cross = jnp.dot((q * qd).astype(BF16), state.astype(BF16), preferred_element_type=F32)
        kv = jnp.dot((k * kd).T.astype(BF16), v, preferred_element_type=F32)
        s_ref[0, h] = cd_ref[h] * state + kv
        g = g_refs[h // 2][:, half_cols].astype(F32)
        r_ref[:, h * DV:(h + 1) * DV] = _group_norm_gate(intra + cross, g).astype(r_ref.dtype)


def _retention_tables(chunk, seq_len, pos0):
    log_g = jnp.log1p(-jnp.exp2(-5.0 - jnp.arange(N_HEADS, dtype=F32)))
    pos = jnp.arange(chunk, dtype=F32)
    diff = pos[:, None] - pos[None, :]
    mask = jnp.where(diff >= 0, jnp.exp(jnp.maximum(diff, 0.0)[None] * log_g[:, None, None]), 0.0)
    q_decay = jnp.exp((pos + 1.0)[None, :] * log_g[:, None])[..., None]
    k_decay = jnp.exp((chunk - 1.0 - pos)[None, :] * log_g[:, None])[..., None]
    chunk_decay = jnp.exp(chunk * log_g)[:, None, None]
    qd = jnp.broadcast_to(q_decay, (N_HEADS, chunk, LANES))
    kd = jnp.broadcast_to(k_decay, (N_HEADS, chunk, LANES))
    cd = jnp.broadcast_to(chunk_decay, (N_HEADS, 1, DV))
    positions = pos0 + jnp.arange(seq_len, dtype=jnp.int32)
    inv = ROPE_BASE ** (-jnp.linspace(0.0, 1.0, HALF, dtype=F32))
    ang = positions.astype(F32)[:, None] * inv[None, :]
    return mask, qd, kd, cd, jnp.cos(ang), jnp.sin(ang)


def _retention_prompt(proj, batch, seq_len):
    c = RET_CHUNK
    nc = seq_len // c
    mask, qd, kd, cd, cos, sin = _retention_tables(c, seq_len, 0)
    row = lambda b, j: b * nc + j
    full3 = lambda b, j: (0, 0, 0)
    return pl.pallas_call(
        _ret_chunk_kernel,
        grid=(batch, nc),
        in_specs=[
            pl.BlockSpec((c, RET_QK), lambda b, j: (row(b, j), COL_Q // RET_QK)),
            pl.BlockSpec((c, RET_QK), lambda b, j: (row(b, j), COL_K // RET_QK)),
            pl.BlockSpec((c, RET_QK), lambda b, j: (row(b, j), COL_V // RET_QK)),
            pl.BlockSpec((c, RET_QK), lambda b, j: (row(b, j), COL_V // RET_QK + 1)),
            pl.BlockSpec((c, RET_QK), lambda b, j: (row(b, j), COL_G // RET_QK)),
            pl.BlockSpec((c, RET_QK), lambda b, j: (row(b, j), COL_G // RET_QK + 1)),
            pl.BlockSpec((c, HALF), lambda b, j: (j, 0)),
            pl.BlockSpec((c, HALF), lambda b, j: (j, 0)),
            pl.BlockSpec((N_HEADS, c, c), full3),
            pl.BlockSpec((N_HEADS, c, LANES), full3),
            pl.BlockSpec((N_HEADS, c, LANES), full3),
            pl.BlockSpec((N_HEADS, 1, DV), full3),
        ],
        out_specs=[
            pl.BlockSpec((c, RET_V), lambda b, j: (row(b, j), 0)),
            pl.BlockSpec((1, N_HEADS, DK, DV), lambda b, j: (b, 0, 0, 0)),
        ],
        out_shape=[
            jax.ShapeDtypeStruct((batch * seq_len, RET_V), BF16),
            jax.ShapeDtypeStruct((batch, N_HEADS, DK, DV), F32),
        ],
        compiler_params=_params(("arbitrary", "arbitrary")),
        name="retention_chunk",
    )(proj, proj, proj, proj, proj, proj, cos, sin, mask, qd, kd, cd)


def _ret_step_kernel(q_ref, k_ref, v_ref, g_ref, cos_ref, sin_ref, gam_ref, s_ref, r_ref, so_ref):
    cos = cos_ref[...]
    sin = sin_ref[...]
    gam = gam_ref[0]
    q = _rotary(q_ref[...], cos, sin)
    k = _rotary(k_ref[...], cos, sin) * (DK ** -0.5)
    v = v_ref[...]
    intra = jnp.sum(q * k, axis=-1, keepdims=True) * v
    q_t = (q * gam[:, :DK]).T
    k_t = k.T
    cross = []
    for s in range(SAMPLE_GROUP):
        state = s_ref[s, 0]
        cross.append(jnp.sum(q_t[:, s:s + 1] * state, axis=0, keepdims=True))
        so_ref[s, 0] = gam * state + k_t[:, s:s + 1] * v[s:s + 1, :]
    o = intra + jnp.concatenate(cross, axis=0)
    r_ref[...] = _group_norm_gate(o, g_ref[...])


def _retention_sample(proj, state, pos0):
    n = proj.shape[0]
    grp = SAMPLE_GROUP
    log_g = jnp.log1p(-jnp.exp2(-5.0 - jnp.arange(N_HEADS, dtype=F32)))
    gam = jnp.broadcast_to(jnp.exp(log_g)[:, None, None], (N_HEADS, 1, DV))
    _, _, _, _, cos, sin = _retention_tables(1, 1, pos0)
    return pl.pallas_call(
        _ret_step_kernel,
        grid=(n // grp, N_HEADS),
        in_specs=[
            pl.BlockSpec((grp, DK), lambda j, h: (j, COL_Q // DK + h)),
            pl.BlockSpec((grp, DK), lambda j, h: (j, COL_K // DK + h)),
            pl.BlockSpec((grp, DV), lambda j, h: (j, COL_V // DV + h)),
            pl.BlockSpec((grp, DV), lambda j, h: (j, COL_G // DV + h)),
            pl.BlockSpec((1, HALF), lambda j, h: (0, 0)),
            pl.BlockSpec((1, HALF), lambda j, h: (0, 0)),
            pl.BlockSpec((1, 1, DV), lambda j, h: (h, 0, 0)),
            pl.BlockSpec((grp, 1, DK, DV), lambda j, h: (j, h, 0, 0)),
        ],
        out_specs=[
            pl.BlockSpec((grp, DV), lambda j, h: (j, h)),
            pl.BlockSpec((grp, 1, DK, DV), lambda j, h: (j, h, 0, 0)),
        ],
        out_shape=[
            jax.ShapeDtypeStruct((n, RET_V), F32),
            jax.ShapeDtypeStruct(state.shape, F32),
        ],
        compiler_params=_params(("arbitrary", "arbitrary")),
        name="retention_step",
    )(proj, proj, proj, proj, cos, sin, gam, state)


def _merge_kernel(per_token_hist, chained, tiles_per_seq, region_rows, *refs):
    refs = list(refs)
    cb_ref, cc_ref, ch_ref, ga_ref, gb_ref, r_ref, x_ref = refs[:7]
    del refs[:7]
    if per_token_hist:
        h0_ref, h1_ref = refs[:2]
        del refs[:2]
    cw_ref, wa_ref, wb_ref, wo_ref, n2_ref, wr_ref, br_ref, cnt_in = refs[:8]
    del refs[:8]
    if chained:
        del refs[:1]
    h_out, route_out, gate_out, u_out, cnt_out, xs_out = refs[:6]
    del refs[:6]
    if not per_token_hist:
        hist_ref = refs.pop(0)
    cnt_ref, hn_st, dvm, dsm, sem = refs

    i = pl.program_id(0)
    last = pl.num_programs(0) - 1
    tm = x_ref.shape[0]
    cur = i % 2
    prev = 1 - cur

    def wait_scatters(p):
        for _ in range(TOP_K):
            pltpu.make_async_copy(hn_st.at[p], xs_out.at[pl.ds(0, tm * SUBLANES), :], sem.at[p]).wait()

    def dests_to_smem(p):
        return pltpu.make_async_copy(dvm, dsm.at[p], sem.at[2])

    def row_scatter(p, t, kk):
        rows = t * SUBLANES if isinstance(t, int) else pl.multiple_of(t * SUBLANES, SUBLANES)
        dst = xs_out.at[pl.ds(pl.multiple_of(dsm[p, kk, t], SUBLANES), SUBLANES), :]
        return pltpu.make_async_copy(hn_st.at[p, pl.ds(rows, SUBLANES), :], dst, sem.at[p])

    @pl.when(i == 0)
    def _():
        cnt_ref[...] = cnt_in[...]
        hn_st[1] = jnp.zeros((tm * SUBLANES, LANES), F32)
        spare = lax.broadcasted_iota(I32, (SUBLANES, tm), 0) * tm + lax.broadcasted_iota(I32, (SUBLANES, tm), 1)
        dvm[...] = (N_EXPERTS * region_rows + spare) * SUBLANES
        dests_to_smem(1).start()
        dests_to_smem(1).wait()

    if not per_token_hist:
        @pl.when(i % tiles_per_seq == 0)
        def _():
            hist_ref[...] = jnp.zeros_like(hist_ref)

    @pl.when(i > 0)
    def _():
        dests_to_smem(prev).wait()
        wait_scatters(cur)

    for t in range(tm):
        for kk in range(TOP_K):
            row_scatter(prev, t, kk).start(priority=kk % DMA_QUEUES)

    u = cc_ref[...].astype(F32) * ch_ref[...].astype(F32)
    if per_token_hist:
        u2 = h0_ref[...]
        u1 = h1_ref[...]
        u_out[...] = u
    else:
        hm2 = hist_ref[0:1, :]
        hm1 = hist_ref[1:2, :]
        row = lax.broadcasted_iota(I32, u.shape, 0)
        u1 = jnp.where(row == 0, hm1, pltpu.roll(u, 1, axis=0))
        u2 = jnp.where(row == 0, hm2, jnp.where(row == 1, hm1, pltpu.roll(u, 2, axis=0)))
        last2 = u[tm - (CONV_W - 1):, :]
        hist_ref[...] = last2
        u_out[0] = last2
    conv = cw_ref[0:1, :] * u2 + cw_ref[1:2, :] * u1 + cw_ref[2:3, :] * u
    z = (cb_ref[...].astype(F32) * conv).astype(BF16)
    groups = [slice(g * tm // MERGE_GROUPS, (g + 1) * tm // MERGE_GROUPS) for g in range(MERGE_GROUPS)]
    nt = (((1,), (1,)), ((), ()))
    o_ab = [(jnp.dot(z[rows], wa_ref[...], preferred_element_type=F32),
             jnp.dot(r_ref[rows, :].astype(BF16), wb_ref[...], preferred_element_type=F32)) for rows in groups]
    ms = []
    for rows, (o_a, o_b) in zip(groups, o_ab):
        merged = (jax.nn.sigmoid(ga_ref[rows, :].astype(F32)) * o_a
                  + jax.nn.sigmoid(gb_ref[rows, :].astype(F32)) * o_b)
        ms.append(jnp.dot(merged.astype(BF16), wo_ref[...], preferred_element_type=F32))
    logit_groups = []
    for rows, m in zip(groups, ms):
        h = x_ref[rows, :] + m
        h_out[rows, :] = h
        hn = (h * lax.rsqrt(jnp.mean(h * h, axis=-1, keepdims=True) + NORM_EPS)) * n2_ref[...]
        hn_hi = hn.astype(BF16)
        hn_lo = (hn - hn_hi.astype(F32)).astype(BF16)
        part = lax.dot_general(wr_ref[...], hn_hi, nt, preferred_element_type=F32)
        logit_groups.append(part[:N_EXPERTS] + part[N_EXPERTS:]
                            + lax.dot_general(wr_ref[:N_EXPERTS, :], hn_lo, nt, preferred_element_type=F32))
        for c in range(D_MODEL // LANES):
            hn_st[cur, pl.ds(rows.start * SUBLANES + c, rows.stop - rows.start, stride=SUBLANES), :] = (
                hn[:, c * LANES:(c + 1) * LANES])
    logits = jnp.concatenate(logit_groups, axis=1) + br_ref[...]
    expert = lax.broadcasted_iota(I32, logits.shape, 0)
    vals, idxs = [], []
    for _ in range(TOP_K):
        top = jnp.max(logits, axis=0, keepdims=True)
        arg = jnp.min(jnp.where(logits == top, expert, N_EXPERTS), axis=0, keepdims=True)
        vals.append(top)
        idxs.append(arg)
        logits = jnp.where(expert == arg, -jnp.inf, logits)
    exps = [jnp.exp(val - vals[0]) for val in vals]
    denom = exps[0] + exps[1] + exps[2] + exps[3]

    picks = [expert == idx for idx in idxs]
    chosen = jnp.zeros(logits.shape, F32)
    for pick in picks:
        chosen = jnp.where(pick, 1.0, chosen)
    before = (lax.broadcasted_iota(I32, (tm, tm), 0) < lax.broadcasted_iota(I32, (tm, tm), 1))
    prefix = jnp.dot(chosen.astype(BF16), jnp.where(before, 1.0, 0.0).astype(BF16),
                     preferred_element_type=F32)
    counts = cnt_ref[...]
    base = jnp.concatenate([counts] * (tm // LANES), axis=1) + prefix
    ranks = [jnp.sum(jnp.where(pick, base, 0.0), axis=0, keepdims=True).astype(I32) for pick in picks]
    counts = counts + jnp.sum(chosen, axis=1, keepdims=True)
    cnt_ref[...] = counts
    cnt_out[...] = counts

    slot_row = lax.broadcasted_iota(I32, (SUBLANES, tm), 0)
    route = jnp.zeros((SUBLANES, tm), I32)
    gates = jnp.zeros((SUBLANES, tm), F32)
    dests = jnp.zeros((SUBLANES, tm), I32)
    for kk in range(TOP_K):
        route = jnp.where(slot_row == kk, idxs[kk], route)
        dest = (idxs[kk] * region_rows + ranks[kk]) * SUBLANES
        route = jnp.where(slot_row == TOP_K + kk, dest, route)
        gates = jnp.where(slot_row == kk, exps[kk] / denom, gates)
        dests = jnp.where(slot_row == kk, dest, dests)
    route_out[...] = route
    gate_out[...] = gates

    dvm[...] = dests
    dests_to_smem(cur).start()

    @pl.when(i == last)
    def _():
        dests_to_smem(cur).wait()
        wait_scatters(prev)

        def issue(g, carry):
            for uu in range(ISSUE_UNROLL):
                for kk in range(TOP_K):
                    row_scatter(cur, g * ISSUE_UNROLL + uu, kk).start()
            return carry

        lax.fori_loop(0, tm // ISSUE_UNROLL, issue, 0)
        wait_scatters(cur)


def _merge(proj, r, x, hist, conv_w, wa, wb, wo, norm2, w_router, b_router, counts_in, xs_prev, region_rows,
           tm, tiles_per_seq):
    t = x.shape[0]
    per_token = hist is not None
    chained = xs_prev is not None
    col = lambda c: (lambda i: (i, c // D_MODEL))
    const2 = lambda i: (0, 0)
    tile = pl.BlockSpec((tm, D_MODEL), lambda i: (i, 0))
    in_specs = [
        pl.BlockSpec((tm, D_MODEL), col(COL_CB)),
        pl.BlockSpec((tm, D_MODEL), col(COL_CC)),
        pl.BlockSpec((tm, D_MODEL), col(COL_CH)),
        pl.BlockSpec((tm, D_MODEL), col(COL_GA)),
        pl.BlockSpec((tm, D_MODEL), col(COL_GB)),
        pl.BlockSpec((tm, RET_V), lambda i: (i, 0)),
        tile,
    ]
    args = [proj, proj, proj, proj, proj, r, x]
    if per_token:
        in_specs += [tile, tile]
        args += list(hist)
    in_specs += [
        pl.BlockSpec((CONV_W, D_MODEL), const2),
        pl.BlockSpec((D_MODEL, D_MODEL), const2),
        pl.BlockSpec((RET_V, D_MODEL), const2),
        pl.BlockSpec((D_MODEL, D_MODEL), const2),
        pl.BlockSpec((1, D_MODEL), const2),
        pl.BlockSpec((2 * N_EXPERTS, D_MODEL), const2),
        pl.BlockSpec((N_EXPERTS, tm), const2),
        pl.BlockSpec((N_EXPERTS, LANES), const2),
    ]
    args += [conv_w, wa, wb, wo, norm2, w_router, b_router, counts_in]
    aliases = {}
    if chained:
        aliases = {len(args): 5}
        in_specs.append(pl.BlockSpec(memory_space=pl.ANY))
        args.append(xs_prev)
    lanes_tile = pl.BlockSpec((SUBLANES, tm), lambda i: (0, i))
    if per_token:
        u_spec = tile
        u_shape = jax.ShapeDtypeStruct((t, D_MODEL), F32)
        scratch = []
    else:
        n_seq = t // (tm * tiles_per_seq)
        u_spec = pl.BlockSpec((1, CONV_W - 1, D_MODEL), lambda i: (i // tiles_per_seq, 0, 0))
        u_shape = jax.ShapeDtypeStruct((n_seq, CONV_W - 1, D_MODEL), F32)
        scratch = [pltpu.VMEM((CONV_W - 1, D_MODEL), F32)]
    out_specs = [tile, lanes_tile, lanes_tile, u_spec, pl.BlockSpec((N_EXPERTS, LANES), const2),
                 pl.BlockSpec(memory_space=pl.ANY)]
    out_shape = [
        jax.ShapeDtypeStruct((t, D_MODEL), F32),
        jax.ShapeDtypeStruct((SUBLANES, t), I32),
        jax.ShapeDtypeStruct((SUBLANES, t), F32),
        u_shape,
        jax.ShapeDtypeStruct((N_EXPERTS, LANES), F32),
        jax.ShapeDtypeStruct(((N_EXPERTS * region_rows + TOP_K * MERGE_TM) * SUBLANES, LANES), F32),
    ]
    scratch += [
        pltpu.VMEM((N_EXPERTS, LANES), F32),
        pltpu.VMEM((2, tm * SUBLANES, LANES), F32),
        pltpu.VMEM((SUBLANES, tm), I32),
        pltpu.SMEM((2, SUBLANES, tm), I32),
        pltpu.SemaphoreType.DMA((3,)),
    ]
    return pl.pallas_call(
        functools.partial(_merge_kernel, per_token, chained, tiles_per_seq, region_rows),
        grid=(t // tm,),
        in_specs=in_specs,
        out_specs=out_specs,
        out_shape=out_shape,
        scratch_shapes=scratch,
        input_output_aliases=aliases,
        compiler_params=_params(("arbitrary",)),
        name="merge_sample" if per_token else "merge_prompt",
    )(*args)


def _rows_from_tiles(ref, n_rows):
    return jnp.concatenate([ref[pl.ds(c, n_rows, stride=SUBLANES), :] for c in range(D_MODEL // LANES)], axis=1)


def _rows_to_tiles(ref, rows):
    for c in range(D_MODEL // LANES):
        ref[pl.ds(c, rows.shape[0], stride=SUBLANES), :] = rows[:, c * LANES:(c + 1) * LANES]


def _moe_kernel(be_ref, bx_ref, vr_ref, nused_ref, next_ref, *refs):
    x_refs = refs[:MOE_STEP_BLOCKS]
    w1_hbm, b1_ref, w2_hbm, b2_ref, y_hbm, w1f, w2f, w1b, w2b, obuf, sem, osem = refs[MOE_STEP_BLOCKS:]
    n_used = nused_ref[0]
    step = pl.program_id(0)
    rows = MOE_BM * SUBLANES

    def out_copy(at_step, sub):
        blk = at_step * MOE_STEP_BLOCKS + sub
        dst = y_hbm.at[pl.ds(pl.multiple_of(bx_ref[blk] * rows, rows), rows), :]
        return pltpu.make_async_copy(obuf.at[at_step % 2, sub], dst, osem.at[at_step % 2])

    def wait_outputs(at_step):
        for sub in range(MOE_STEP_BLOCKS):
            @pl.when((at_step >= 0) & (at_step * MOE_STEP_BLOCKS + sub < n_used))
            def _(sub=sub):
                out_copy(at_step, sub).wait()

    wait_outputs(step - 2)

    def weight_copies(e):
        return (pltpu.make_async_copy(w1_hbm.at[e], w1f, sem.at[0]),
                pltpu.make_async_copy(w2_hbm.at[e], w2f, sem.at[1]))

    def block(sub, x_ref):
        i = step * MOE_STEP_BLOCKS + sub
        o_view = obuf.at[step % 2, sub]

        @pl.when(i < n_used)
        def _():
            e = be_ref[i]
            prev = be_ref[jnp.maximum(i - 1, 0)]

            @pl.when(i == 0)
            def _():
                for cp in weight_copies(e):
                    cp.start()

            @pl.when((i == 0) | (e != prev))
            def _():
                for cp in weight_copies(e):
                    cp.wait()
                w1b[...] = w1f[...].astype(BF16)
                w2b[...] = w2f[...].astype(BF16)
                nxt = next_ref[e]

                @pl.when(nxt >= 0)
                def _():
                    for cp in weight_copies(nxt):
                        cp.start()

            live = lax.broadcasted_iota(I32, (MOE_BM, 1), 0) < vr_ref[i]
            x = jnp.where(live, _rows_from_tiles(x_ref, MOE_BM), 0.0).astype(BF16)
            h1 = jnp.dot(x, w1b[...], preferred_element_type=F32) + b1_ref[e]
            gate = jnp.minimum(h1[:, :D_FF], SWIGLU_LIMIT)
            up = jnp.clip(h1[:, D_FF:], -SWIGLU_LIMIT, SWIGLU_LIMIT)
            glu = gate * jax.nn.sigmoid(SWIGLU_ALPHA * gate)
            act = ((up + 1.0) * glu).astype(BF16)
            _rows_to_tiles(o_view, jnp.dot(act, w2b[...], preferred_element_type=F32) + b2_ref[e])
            out_copy(step, sub).start()

    for sub in range(MOE_STEP_BLOCKS):
        block(sub, x_refs[sub])

    @pl.when(step == pl.num_programs(0) - 1)
    def _():
        wait_outputs(step - 1)
        wait_outputs(step)


def _moe_blocks(xs, block_expert, block_xs, valid_rows, n_used, next_expert, w1, b1, w2, b2):
    n_blocks = block_expert.shape[0]
    rows = MOE_BM * SUBLANES
    x_spec = lambda sub: pl.BlockSpec(
        (rows, LANES), lambda i, be, bx, vr, nu, nx: (bx[i * MOE_STEP_BLOCKS + sub], 0))
    whole3 = lambda i, be, bx, vr, nu, nx: (0, 0, 0)
    grid_spec = pltpu.PrefetchScalarGridSpec(
        num_scalar_prefetch=5,
        grid=(n_blocks // MOE_STEP_BLOCKS,),
        in_specs=[x_spec(sub) for sub in range(MOE_STEP_BLOCKS)] + [
            pl.BlockSpec(memory_space=pl.ANY),
            pl.BlockSpec((N_EXPERTS, 1, 2 * D_FF), whole3),
            pl.BlockSpec(memory_space=pl.ANY),
            pl.BlockSpec((N_EXPERTS, 1, D_MODEL), whole3),
        ],
        out_specs=pl.BlockSpec(memory_space=pl.ANY),
        scratch_shapes=[
            pltpu.VMEM((D_MODEL, 2 * D_FF), F32),
            pltpu.VMEM((D_FF, D_MODEL), F32),
            pltpu.VMEM((D_MODEL, 2 * D_FF), BF16),
            pltpu.VMEM((D_FF, D_MODEL), BF16),
            pltpu.VMEM((2, MOE_STEP_BLOCKS, rows, LANES), F32),
            pltpu.SemaphoreType.DMA((2,)),
            pltpu.SemaphoreType.DMA((2,)),
        ],
    )
    return pl.pallas_call(
        _moe_kernel,
        grid_spec=grid_spec,
        out_shape=jax.ShapeDtypeStruct(xs.shape, F32),
        compiler_params=_params(("arbitrary",)),
        name="moe_experts",
    )(block_expert, block_xs, valid_rows, n_used, next_expert, *([xs] * MOE_STEP_BLOCKS),
      w1, b1[:, None, :], w2, b2[:, None, :])


def _block_plan(counts, n_tokens, region_rows):
    n_blocks = -(-n_tokens * TOP_K // MOE_BM) + N_EXPERTS
    n_blocks = -(-n_blocks // MOE_STEP_BLOCKS) * MOE_STEP_BLOCKS
    blocks_e = (counts + MOE_BM - 1) // MOE_BM
    blk_end = jnp.cumsum(blocks_e)
    blk_start = blk_end - blocks_e
    n_used = blk_end[-1:]
    blk = jnp.maximum(jnp.minimum(jnp.arange(n_blocks, dtype=I32), n_used - 1), 0)
    expert = jnp.minimum(jnp.sum((blk_end[None, :] <= blk[:, None]).astype(I32), axis=1), N_EXPERTS - 1)
    ids = jnp.arange(N_EXPERTS, dtype=I32)
    mine = ids[None, :] == expert[:, None]
    within = blk - jnp.sum(jnp.where(mine, blk_start[None, :], 0), axis=1)
    block_xs = expert * (region_rows // MOE_BM) + within
    valid = jnp.clip(jnp.sum(jnp.where(mine, counts[None, :], 0), axis=1) - within * MOE_BM, 0, MOE_BM)
    later = jnp.where((ids[None, :] > ids[:, None]) & (counts[None, :] > 0), ids[None, :], N_EXPERTS)
    next_expert = jnp.min(later, axis=1)
    next_expert = jnp.where(next_expert == N_EXPERTS, -1, next_expert)
    return expert, block_xs.astype(I32), valid.astype(I32), n_used.astype(I32), next_expert.astype(I32)


def _combine_kernel(n_main, n_tokens, dest_ref, hp_ref, hs_ref, gp_ref, gs_ref, nf_ref, yb_hbm, op_ref, os_ref,
                    *scratch):
    ybufs, sem = scratch[:-1], scratch[-1]
    i = pl.program_id(0)
    last = pl.num_programs(0) - 1
    ring = len(ybufs)
    ahead = ring - 1

    def row_copy(tile, sl, t, kk):
        row = pl.multiple_of(dest_ref[kk * n_tokens + tile * COMB_TT + t], SUBLANES)
        rows = t * SUBLANES if isinstance(t, int) else pl.multiple_of(t * SUBLANES, SUBLANES)
        return pltpu.make_async_copy(yb_hbm.at[pl.ds(row, SUBLANES), :],
                                     ybufs[sl].at[kk, pl.ds(rows, SUBLANES), :], sem.at[sl])

    def wait_tile(sl):
        for kk in range(TOP_K):
            pltpu.make_async_copy(yb_hbm.at[pl.ds(0, COMB_TT * SUBLANES), :], ybufs[sl].at[kk], sem.at[sl]).wait()

    @pl.when(i == 0)
    def _():
        def issue(g, carry):
            for uu in range(ISSUE_UNROLL):
                for kk in range(TOP_K):
                    for tile in range(ahead):
                        row_copy(jnp.minimum(tile, last), tile, g * ISSUE_UNROLL + uu, kk).start()
            return carry

        lax.fori_loop(0, COMB_TT // ISSUE_UNROLL, issue, 0)

    def step(cur, h_ref, gate_ref, o_ref):
        wait_tile(cur)
        later = jnp.minimum(i + ahead, last)
        for t in range(COMB_TT):
            for kk in range(TOP_K):
                row_copy(later, (cur + ahead) % ring, t, kk).start(priority=kk % DMA_QUEUES)
        pad = jnp.zeros((COMB_TT - SUBLANES, COMB_TT), F32)
        gates = jnp.concatenate([gate_ref[...], pad], axis=0).T
        y = gates[:, 0:1] * _rows_from_tiles(ybufs[cur].at[0], COMB_TT)
        for kk in range(1, TOP_K):
            y = y + gates[:, kk:kk + 1] * _rows_from_tiles(ybufs[cur].at[kk], COMB_TT)
        h = h_ref[...] + y
        o_ref[...] = (h * lax.rsqrt(jnp.mean(h * h, axis=-1, keepdims=True) + NORM_EPS)) * nf_ref[...]

    for cur in range(ring):
        @pl.when((i % ring == cur) & (i < n_main))
        def _(cur=cur):
            step(cur, hp_ref, gp_ref, op_ref)

        @pl.when((i % ring == cur) & (i >= n_main))
        def _(cur=cur):
            step(cur, hs_ref, gs_ref, os_ref)

    @pl.when(i == last)
    def _():
        for cur in range(ring):
            @pl.when(i % ring == cur)
            def _(cur=cur):
                for extra in range(1, ring):
                    wait_tile((cur + extra) % ring)


def _combine(dest, h_p, h_s, gate_p, gate_s, norm_f, yb):
    n_main = h_p.shape[0] // COMB_TT
    n_tail = h_s.shape[0] // COMB_TT
    main = lambda i, d: (jnp.minimum(i, n_main - 1), 0)
    tail = lambda i, d: (jnp.maximum(i - n_main, 0), 0)
    grid_spec = pltpu.PrefetchScalarGridSpec(
        num_scalar_prefetch=1,
        grid=(n_main + n_tail,),
        in_specs=[
            pl.BlockSpec((COMB_TT, D_MODEL), main),
            pl.BlockSpec((COMB_TT, D_MODEL), tail),
            pl.BlockSpec((SUBLANES, COMB_TT), lambda i, d: (0, jnp.minimum(i, n_main - 1))),
            pl.BlockSpec((SUBLANES, COMB_TT), lambda i, d: (0, jnp.maximum(i - n_main, 0))),
            pl.BlockSpec((1, D_MODEL), lambda i, d: (0, 0)),
            pl.BlockSpec(memory_space=pl.ANY),
        ],
        out_specs=[
            pl.BlockSpec((COMB_TT, D_MODEL), main),
            pl.BlockSpec((COMB_TT, D_MODEL), tail),
        ],
        scratch_shapes=[
            *[pltpu.VMEM((TOP_K, COMB_TT * SUBLANES, LANES), F32) for _ in range(COMB_RING)],
            pltpu.SemaphoreType.DMA((COMB_RING,)),
        ],
    )
    return pl.pallas_call(
        functools.partial(_combine_kernel, n_main, h_p.shape[0] + h_s.shape[0]),
        grid_spec=grid_spec,
        out_shape=[jax.ShapeDtypeStruct(h_p.shape, F32), jax.ShapeDtypeStruct(h_s.shape, F32)],
        compiler_params=_params(("arbitrary",)),
        name="combine",
    )(dest, h_p, h_s, gate_p, gate_s, norm_f, yb)


def kernel(x_prompt, x_sample, state_conv, state_ret, norm1, w_in, conv_w, w_a, w_b, w_o, norm2, w_router,
           b_router, w_e1, b_e1, w_e2, b_e2, norm_f):
    batch, seq_len, _ = x_prompt.shape
    n_dec, dec_seq, _ = x_sample.shape
    depth = norm1.shape[0]
    assert dec_seq == 1 and depth == 1, "single-token decode step of a one-layer trunk"
    past_len = 16384
    t_p = batch * seq_len
    n_tokens = t_p + n_dec
    region_rows = _expert_region_rows(n_tokens)
    lyr = 0
    xp = x_prompt.reshape(t_p, D_MODEL)
    xs = x_sample.reshape(n_dec, D_MODEL)
    g1 = norm1[lyr][None, :]
    w_in_b = w_in[lyr]
    wa, wb, wo = w_a[lyr].astype(BF16), w_b[lyr].astype(BF16), w_o[lyr].astype(BF16)
    g2 = norm2[lyr][None, :]
    wr_t = w_router[lyr].T
    wr_hi = wr_t.astype(BF16)
    wr_split = jnp.concatenate([wr_hi, (wr_t - wr_hi.astype(F32)).astype(BF16)], axis=0)
    br_col = b_router[lyr][:, None]

    proj_p = _inproj(xp, g1, w_in_b, INPROJ_TM, INPROJ_TN, BF16)
    r_p, ret_p = _retention_prompt(proj_p, batch, seq_len)
    h_p, route_p, gate_p, conv_p, counts, dispatch = _merge(
        proj_p, r_p, xp, None, conv_w[lyr], wa, wb, wo, g2, wr_split, jnp.broadcast_to(br_col, (N_EXPERTS, MERGE_TM)),
        jnp.zeros((N_EXPERTS, LANES), F32), None, region_rows, MERGE_TM, seq_len // MERGE_TM)
    proj_s = _inproj(xs, g1, w_in_b, n_dec, INPROJ_TN_SAMPLE, F32)
    r_s, ret_s = _retention_sample(proj_s, state_ret[lyr], past_len)
    hist = (state_conv[lyr][:, 0, :], state_conv[lyr][:, 1, :])
    h_s, route_s, gate_s, u_s, counts, dispatch = _merge(
        proj_s, r_s, xs, hist, conv_w[lyr], wa, wb, wo, g2, wr_split, jnp.broadcast_to(br_col, (N_EXPERTS, n_dec)),
        counts, dispatch, region_rows, n_dec, 1)
    conv_s = jnp.stack([state_conv[lyr][:, 1, :], u_s], axis=1)

    counts_i = counts[:, 0].astype(I32)
    block_expert, block_xs, valid_rows, n_used, next_expert = _block_plan(counts_i, n_tokens, region_rows)
    yb = _moe_blocks(dispatch, block_expert, block_xs, valid_rows, n_used, next_expert,
                     w_e1[lyr], b_e1[lyr], w_e2[lyr], b_e2[lyr])
    dest = jnp.concatenate([route_p[TOP_K:], route_s[TOP_K:]], axis=1).reshape(TOP_K * n_tokens)
    y_p, y_s = _combine(dest, h_p, h_s, gate_p, gate_s, norm_f[None, :], yb)

    y_prompt = y_p.reshape(batch, seq_len, D_MODEL)
    y_sample = y_s.reshape(n_dec, 1, D_MODEL)
    return (y_prompt, y_sample, conv_p[None], ret_p[None], conv_s[None], ret_s[None])
```

```python
import functools

import jax
import jax.numpy as jnp
from jax import lax
from jax.experimental import pallas as pl
from jax.experimental.pallas import tpu as pltpu

F32 = jnp.float32
BF16 = jnp.bfloat16
I32 = jnp.int32

D_MODEL = 1024
CONV_W = 3
N_HEADS = 4
DK = 256
DV = 512
HALF = DK // 2
RET_QK = N_HEADS * DK
RET_V = N_HEADS * DV
ROPE_BASE = 10000.0
N_EXPERTS = 32
TOP_K = 4
D_FF = 1024
SWIGLU_ALPHA = 1.702
SWIGLU_LIMIT = 7.0
NORM_EPS = 1e-5
N_IN_COLS = 3 * D_MODEL + 2 * RET_QK + 2 * RET_V + 2 * D_MODEL
COL_CB, COL_CC, COL_CH = 0, D_MODEL, 2 * D_MODEL
COL_Q = 3 * D_MODEL
COL_K = COL_Q + RET_QK
COL_V = COL_K + RET_QK
COL_G = COL_V + RET_V
COL_GA = COL_G + RET_V
COL_GB = COL_GA + D_MODEL

RET_CHUNK = 256
MERGE_TM = 256
MERGE_GROUPS = 2
INPROJ_TM = 2048
INPROJ_TN = 1024
INPROJ_TN_SAMPLE = 2816
MOE_BM = 256
MOE_STEP_BLOCKS = 4
COMB_TT = 128
COMB_RING = 3
SAMPLE_GROUP = 16
LANES = 128
SUBLANES = 8
ISSUE_UNROLL = 8
DMA_QUEUES = 2
VMEM_LIMIT = 56 * 1024 * 1024


def _params(sem):
    return pltpu.CompilerParams(dimension_semantics=sem, vmem_limit_bytes=VMEM_LIMIT)


def _expert_region_rows(n_tokens):
    return -(-n_tokens // MOE_BM) * MOE_BM


def _inproj_kernel(x_ref, g_ref, w_ref, o_ref, xn_ref):
    @pl.when(pl.program_id(1) == 0)
    def _():
        x = x_ref[...]
        ms = jnp.mean(x * x, axis=-1, keepdims=True)
        xn_ref[...] = ((x * lax.rsqrt(ms + NORM_EPS)) * g_ref[...]).astype(BF16)

    o_ref[...] = jnp.dot(xn_ref[...], w_ref[...].astype(BF16), preferred_element_type=F32).astype(o_ref.dtype)


def _inproj(x, gain, w_bf16, tm, tn, out_dtype):
    m = x.shape[0]
    n = w_bf16.shape[1]
    return pl.pallas_call(
        _inproj_kernel,
        grid=(m // tm, n // tn),
        in_specs=[
            pl.BlockSpec((tm, D_MODEL), lambda i, j: (i, 0)),
            pl.BlockSpec((1, D_MODEL), lambda i, j: (0, 0)),
            pl.BlockSpec((D_MODEL, tn), lambda i, j: (0, j)),
        ],
        out_specs=pl.BlockSpec((tm, tn), lambda i, j: (i, j)),
        out_shape=jax.ShapeDtypeStruct((m, n), out_dtype),
        scratch_shapes=[pltpu.VMEM((tm, D_MODEL), BF16)],
        compiler_params=_params(("arbitrary", "arbitrary")),
        name="inproj",
    )(x, gain, w_bf16)


def _rotary(x, cos, sin):
    x1 = x[:, :HALF]
    x2 = x[:, HALF:]
    return jnp.concatenate([x1 * cos - x2 * sin, x1 * sin + x2 * cos], axis=-1)


def _group_norm_gate(o, g):
    o = o * lax.rsqrt(jnp.mean(o * o, axis=-1, keepdims=True) + NORM_EPS)
    return (g * jax.nn.sigmoid(g)) * o


def _ret_chunk_kernel(q_ref, k_ref, v0_ref, v1_ref, g0_ref, g1_ref, cos_ref, sin_ref, mask_ref, qd_ref, kd_ref,
                      cd_ref, r_ref, s_ref):
    @pl.when(pl.program_id(1) == 0)
    def _():
        s_ref[...] = jnp.zeros_like(s_ref)

    cos = cos_ref[...]
    sin = sin_ref[...]
    v_refs = (v0_ref, v1_ref)
    g_refs = (g0_ref, g1_ref)
    for h in range(N_HEADS):
        half_cols = slice((h % 2) * DV, (h % 2 + 1) * DV)
        q = _rotary(q_ref[:, h * DK:(h + 1) * DK].astype(F32), cos, sin)
        k = _rotary(k_ref[:, h * DK:(h + 1) * DK].astype(F32), cos, sin) * (DK ** -0.5)
        v = v_refs[h // 2][:, half_cols]
        qd = jnp.concatenate([qd_ref[h], qd_ref[h]], axis=-1)
        kd = jnp.concatenate([kd_ref[h], kd_ref[h]], axis=-1)
        state = s_ref[0, h]
        scores = lax.dot_general(q.astype(BF16), k.astype(BF16), (((1,), (1,)), ((), ())),
                                 preferred_element_type=F32) * mask_ref[h]
        intra = jnp.dot(scores.astype(BF16), v, preferred_element_type=F32)
        cross = jnp.dot((q * qd).astype(BF16), state.astype(BF16), preferred_element_type=F32)
        kv = jnp.dot((k * kd).T.astype(BF16), v, preferred_element_type=F32)
        s_ref[0, h] = cd_ref[h] * state + kv
        g = g_refs[h // 2][:, half_cols].astype(F32)
        r_ref[:, h * DV:(h + 1) * DV] = _group_norm_gate(intra + cross, g).astype(r_ref.dtype)


def _retention_tables(chunk, seq_len, pos0):
    log_g = jnp.log1p(-jnp.exp2(-5.0 - jnp.arange(N_HEADS, dtype=F32)))
    pos = jnp.arange(chunk, dtype=F32)
    diff = pos[:, None] - pos[None, :]
    mask = jnp.where(diff >= 0, jnp.exp(jnp.maximum(diff, 0.0)[None] * log_g[:, None, None]), 0.0)
    q_decay = jnp.exp((pos + 1.0)[None, :] * log_g[:, None])[..., None]
    k_decay = jnp.exp((chunk - 1.0 - pos)[None, :] * log_g[:, None])[..., None]
    chunk_decay = jnp.exp(chunk * log_g)[:, None, None]
    qd = jnp.broadcast_to(q_decay, (N_HEADS, chunk, LANES))
    kd = jnp.broadcast_to(k_decay, (N_HEADS, chunk, LANES))
    cd = jnp.broadcast_to(chunk_decay, (N_HEADS, 1, DV))
    positions = pos0 + jnp.arange(seq_len, dtype=jnp.int32)
    inv = ROPE_BASE ** (-jnp.linspace(0.0, 1.0, HALF, dtype=F32))
    ang = positions.astype(F32)[:, None] * inv[None, :]
    return mask, qd, kd, cd, jnp.cos(ang), jnp.sin(ang)


def _retention_prompt(proj, batch, seq_len):
    c = RET_CHUNK
    nc = seq_len // c
    mask, qd, kd, cd, cos, sin = _retention_tables(c, seq_len, 0)
    row = lambda b, j: b * nc + j
    full3 = lambda b, j: (0, 0, 0)
    return pl.pallas_call(
        _ret_chunk_kernel,
        grid=(batch, nc),
        in_specs=[
            pl.BlockSpec((c, RET_QK), lambda b, j: (row(b, j), COL_Q // RET_QK)),
            pl.BlockSpec((c, RET_QK), lambda b, j: (row(b, j), COL_K // RET_QK)),
            pl.BlockSpec((c, RET_QK), lambda b, j: (row(b, j), COL_V // RET_QK)),
            pl.BlockSpec((c, RET_QK), lambda b, j: (row(b, j), COL_V // RET_QK + 1)),
            pl.BlockSpec((c, RET_QK), lambda b, j: (row(b, j), COL_G // RET_QK)),
            pl.BlockSpec((c, RET_QK), lambda b, j: (row(b, j), COL_G // RET_QK + 1)),
            pl.BlockSpec((c, HALF), lambda b, j: (j, 0)),
            pl.BlockSpec((c, HALF), lambda b, j: (j, 0)),
            pl.BlockSpec((N_HEADS, c, c), full3),
            pl.BlockSpec((N_HEADS, c, LANES), full3),
            pl.BlockSpec((N_HEADS, c, LANES), full3),
            pl.BlockSpec((N_HEADS, 1, DV), full3),
        ],
        out_specs=[
            pl.BlockSpec((c, RET_V), lambda b, j: (row(b, j), 0)),
            pl.BlockSpec((1, N_HEADS, DK, DV), lambda b, j: (b, 0, 0, 0)),
        ],
        out_shape=[
            jax.ShapeDtypeStruct((batch * seq_len, RET_V), BF16),
            jax.ShapeDtypeStruct((batch, N_HEADS, DK, DV), F32),
        ],
        compiler_params=_params(("arbitrary", "arbitrary")),
        name="retention_chunk",
    )(proj, proj, proj, proj, proj, proj, cos, sin, mask, qd, kd, cd)


def _ret_step_kernel(q_ref, k_ref, v_ref, g_ref, cos_ref, sin_ref, gam_ref, s_ref, r_ref, so_ref):
    cos = cos_ref[...]
    sin = sin_ref[...]
    gam = gam_ref[0]
    q = _rotary(q_ref[...], cos, sin)
    k = _rotary(k_ref[...], cos, sin) * (DK ** -0.5)
    v = v_ref[...]
    intra = jnp.sum(q * k, axis=-1, keepdims=True) * v
    q_t = (q * gam[:, :DK]).T
    k_t = k.T
    cross = []
    for s in range(SAMPLE_GROUP):
        state = s_ref[s, 0]
        cross.append(jnp.sum(q_t[:, s:s + 1] * state, axis=0, keepdims=True))
        so_ref[s, 0] = gam * state + k_t[:, s:s + 1] * v[s:s + 1, :]
    o = intra + jnp.concatenate(cross, axis=0)
    r_ref[...] = _group_norm_gate(o, g_ref[...])


def _retention_sample(proj, state, pos0):
    n = proj.shape[0]
    grp = SAMPLE_GROUP
    log_g = jnp.log1p(-jnp.exp2(-5.0 - jnp.arange(N_HEADS, dtype=F32)))
    gam = jnp.broadcast_to(jnp.exp(log_g)[:, None, None], (N_HEADS, 1, DV))
    _, _, _, _, cos, sin = _retention_tables(1, 1, pos0)
    return pl.pallas_call(
        _ret_step_kernel,
        grid=(n // grp, N_HEADS),
        in_specs=[
            pl.BlockSpec((grp, DK), lambda j, h: (j, COL_Q // DK + h)),
            pl.BlockSpec((grp, DK), lambda j, h: (j, COL_K // DK + h)),
            pl.BlockSpec((grp, DV), lambda j, h: (j, COL_V // DV + h)),
            pl.BlockSpec((grp, DV), lambda j, h: (j, COL_G // DV + h)),
            pl.BlockSpec((1, HALF), lambda j, h: (0, 0)),
            pl.BlockSpec((1, HALF), lambda j, h: (0, 0)),
            pl.BlockSpec((1, 1, DV), lambda j, h: (h, 0, 0)),
            pl.BlockSpec((grp, 1, DK, DV), lambda j, h: (j, h, 0, 0)),
        ],
        out_specs=[
            pl.BlockSpec((grp, DV), lambda j, h: (j, h)),
            pl.BlockSpec((grp, 1, DK, DV), lambda j, h: (j, h, 0, 0)),
        ],
        out_shape=[
            jax.ShapeDtypeStruct((n, RET_V), F32),
            jax.ShapeDtypeStruct(state.shape, F32),
        ],
        compiler_params=_params(("arbitrary", "arbitrary")),
        name="retention_step",
    )(proj, proj, proj, proj, cos, sin, gam, state)


def _merge_kernel(per_token_hist, chained, tiles_per_seq, region_rows, *refs):
    refs = list(refs)
    cb_ref, cc_ref, ch_ref, ga_ref, gb_ref, r_ref, x_ref = refs[:7]
    del refs[:7]
    if per_token_hist:
        h0_ref, h1_ref = refs[:2]
        del refs[:2]
    cw_ref, wa_ref, wb_ref, wo_ref, n2_ref, wr_ref, br_ref, cnt_in = refs[:8]
    del refs[:8]
    if chained:
        del refs[:1]
    h_out, route_out, gate_out, u_out, cnt_out, xs_out = refs[:6]
    del refs[:6]
    if not per_token_hist:
        hist_ref = refs.pop(0)
    cnt_ref, hn_st, dvm, dsm, sem = refs

    i = pl.program_id(0)
    last = pl.num_programs(0) - 1
    tm = x_ref.shape[0]
    cur = i % 2
    prev = 1 - cur

    def wait_scatters(p):
        for _ in range(TOP_K):
            pltpu.make_async_copy(hn_st.at[p], xs_out.at[pl.ds(0, tm * SUBLANES), :], sem.at[p]).wait()

    def dests_to_smem(p):
        return pltpu.make_async_copy(dvm, dsm.at[p], sem.at[2])

    def row_scatter(p, t, kk):
        rows = t * SUBLANES if isinstance(t, int) else pl.multiple_of(t * SUBLANES, SUBLANES)
        dst = xs_out.at[pl.ds(pl.multiple_of(dsm[p, kk, t], SUBLANES), SUBLANES), :]
        return pltpu.make_async_copy(hn_st.at[p, pl.ds(rows, SUBLANES), :], dst, sem.at[p])

    @pl.when(i == 0)
    def _():
        cnt_ref[...] = cnt_in[...]
        hn_st[1] = jnp.zeros((tm * SUBLANES, LANES), F32)
        spare = lax.broadcasted_iota(I32, (SUBLANES, tm), 0) * tm + lax.broadcasted_iota(I32, (SUBLANES, tm), 1)
        dvm[...] = (N_EXPERTS * region_rows + spare) * SUBLANES
        dests_to_smem(1).start()
        dests_to_smem(1).wait()

    if not per_token_hist:
        @pl.when(i % tiles_per_seq == 0)
        def _():
            hist_ref[...] = jnp.zeros_like(hist_ref)

    @pl.when(i > 0)
    def _():
        dests_to_smem(prev).wait()
        wait_scatters(cur)

    for t in range(tm):
        for kk in range(TOP_K):
            row_scatter(prev, t, kk).start(priority=kk % DMA_QUEUES)

    u = cc_ref[...].astype(F32) * ch_ref[...].astype(F32)
    if per_token_hist:
        u2 = h0_ref[...]
        u1 = h1_ref[...]
        u_out[...] = u
    else:
        hm2 = hist_ref[0:1, :]
        hm1 = hist_ref[1:2, :]
        row = lax.broadcasted_iota(I32, u.shape, 0)
        u1 = jnp.where(row == 0, hm1, pltpu.roll(u, 1, axis=0))
        u2 = jnp.where(row == 0, hm2, jnp.where(row == 1, hm1, pltpu.roll(u, 2, axis=0)))
        last2 = u[tm - (CONV_W - 1):, :]
        hist_ref[...] = last2
        u_out[0] = last2
    conv = cw_ref[0:1, :] * u2 + cw_ref[1:2, :] * u1 + cw_ref[2:3, :] * u
    z = (cb_ref[...].astype(F32) * conv).astype(BF16)
    groups = [slice(g * tm // MERGE_GROUPS, (g + 1) * tm // MERGE_GROUPS) for g in range(MERGE_GROUPS)]
    nt = (((1,), (1,)), ((), ()))
    o_ab = [(jnp.dot(z[rows], wa_ref[...], preferred_element_type=F32),
             jnp.dot(r_ref[rows, :].astype(BF16), wb_ref[...], preferred_element_type=F32)) for rows in groups]
    ms = []
    for rows, (o_a, o_b) in zip(groups, o_ab):
        merged = (jax.nn.sigmoid(ga_ref[rows, :].astype(F32)) * o_a
                  + jax.nn.sigmoid(gb_ref[rows, :].astype(F32)) * o_b)
        ms.append(jnp.dot(merged.astype(BF16), wo_ref[...], preferred_element_type=F32))
    logit_groups = []
    for rows, m in zip(groups, ms):
        h = x_ref[rows, :] + m
        h_out[rows, :] = h
        hn = (h * lax.rsqrt(jnp.mean(h * h, axis=-1, keepdims=True) + NORM_EPS)) * n2_ref[...]
        hn_hi = hn.astype(BF16)
        hn_lo = (hn - hn_hi.astype(F32)).astype(BF16)
        part = lax.dot_general(wr_ref[...], hn_hi, nt, preferred_element_type=F32)
        logit_groups.append(part[:N_EXPERTS] + part[N_EXPERTS:]
                            + lax.dot_general(wr_ref[:N_EXPERTS, :], hn_lo, nt, preferred_element_type=F32))
        for c in range(D_MODEL // LANES):
            hn_st[cur, pl.ds(rows.start * SUBLANES + c, rows.stop - rows.start, stride=SUBLANES), :] = (
                hn[:, c * LANES:(c + 1) * LANES])
    logits = jnp.concatenate(logit_groups, axis=1) + br_ref[...]
    expert = lax.broadcasted_iota(I32, logits.shape, 0)
    vals, idxs = [], []
    for _ in range(TOP_K):
        top = jnp.max(logits, axis=0, keepdims=True)
        arg = jnp.min(jnp.where(logits == top, expert, N_EXPERTS), axis=0, keepdims=True)
        vals.append(top)
        idxs.append(arg)
        logits = jnp.where(expert == arg, -jnp.inf, logits)
    exps = [jnp.exp(val - vals[0]) for val in vals]
    denom = exps[0] + exps[1] + exps[2] + exps[3]

    picks = [expert == idx for idx in idxs]
    chosen = jnp.zeros(logits.shape, F32)
    for pick in picks:
        chosen = jnp.where(pick, 1.0, chosen)
    before = (lax.broadcasted_iota(I32, (tm, tm), 0) < lax.broadcasted_iota(I32, (tm, tm), 1))
    prefix = jnp.dot(chosen.astype(BF16), jnp.where(before, 1.0, 0.0).astype(BF16),
                     preferred_element_type=F32)
    counts = cnt_ref[...]
    base = jnp.concatenate([counts] * (tm // LANES), axis=1) + prefix
    ranks = [jnp.sum(jnp.where(pick, base, 0.0), axis=0, keepdims=True).astype(I32) for pick in picks]
    counts = counts + jnp.sum(chosen, axis=1, keepdims=True)
    cnt_ref[...] = counts
    cnt_out[...] = counts

    slot_row = lax.broadcasted_iota(I32, (SUBLANES, tm), 0)
    route = jnp.zeros((SUBLANES, tm), I32)
    gates = jnp.zeros((SUBLANES, tm), F32)
    dests = jnp.zeros((SUBLANES, tm), I32)
    for kk in range(TOP_K):
        route = jnp.where(slot_row == kk, idxs[kk], route)
        dest = (idxs[kk] * region_rows + ranks[kk]) * SUBLANES
        route = jnp.where(slot_row == TOP_K + kk, dest, route)
        gates = jnp.where(slot_row == kk, exps[kk] / denom, gates)
        dests = jnp.where(slot_row == kk, dest, dests)
    route_out[...] = route
    gate_out[...] = gates

    dvm[...] = dests
    dests_to_smem(cur).start()

    @pl.when(i == last)
    def _():
        dests_to_smem(cur).wait()
        wait_scatters(prev)

        def issue(g, carry):
            for uu in range(ISSUE_UNROLL):
                for kk in range(TOP_K):
                    row_scatter(cur, g * ISSUE_UNROLL + uu, kk).start()
            return carry

        lax.fori_loop(0, tm // ISSUE_UNROLL, issue, 0)
        wait_scatters(cur)


def _merge(proj, r, x, hist, conv_w, wa, wb, wo, norm2, w_router, b_router, counts_in, xs_prev, region_rows,
           tm, tiles_per_seq):
    t = x.shape[0]
    per_token = hist is not None
    chained = xs_prev is not None
    col = lambda c: (lambda i: (i, c // D_MODEL))
    const2 = lambda i: (0, 0)
    tile = pl.BlockSpec((tm, D_MODEL), lambda i: (i, 0))
    in_specs = [
        pl.BlockSpec((tm, D_MODEL), col(COL_CB)),
        pl.BlockSpec((tm, D_MODEL), col(COL_CC)),
        pl.BlockSpec((tm, D_MODEL), col(COL_CH)),
        pl.BlockSpec((tm, D_MODEL), col(COL_GA)),
        pl.BlockSpec((tm, D_MODEL), col(COL_GB)),
        pl.BlockSpec((tm, RET_V), lambda i: (i, 0)),
        tile,
    ]
    args = [proj, proj, proj, proj, proj, r, x]
    if per_token:
        in_specs += [tile, tile]
        args += list(hist)
    in_specs += [
        pl.BlockSpec((CONV_W, D_MODEL), const2),
        pl.BlockSpec((D_MODEL, D_MODEL), const2),
        pl.BlockSpec((RET_V, D_MODEL), const2),
        pl.BlockSpec((D_MODEL, D_MODEL), const2),
        pl.BlockSpec((1, D_MODEL), const2),
        pl.BlockSpec((2 * N_EXPERTS, D_MODEL), const2),
        pl.BlockSpec((N_EXPERTS, tm), const2),
        pl.BlockSpec((N_EXPERTS, LANES), const2),
    ]
    args += [conv_w, wa, wb, wo, norm2, w_router, b_router, counts_in]
    aliases = {}
    if chained:
        aliases = {len(args): 5}
        in_specs.append(pl.BlockSpec(memory_space=pl.ANY))
        args.append(xs_prev)
    lanes_tile = pl.BlockSpec((SUBLANES, tm), lambda i: (0, i))
    if per_token:
        u_spec = tile
        u_shape = jax.ShapeDtypeStruct((t, D_MODEL), F32)
        scratch = []
    else:
        n_seq = t // (tm * tiles_per_seq)
        u_spec = pl.BlockSpec((1, CONV_W - 1, D_MODEL), lambda i: (i // tiles_per_seq, 0, 0))
        u_shape = jax.ShapeDtypeStruct((n_seq, CONV_W - 1, D_MODEL), F32)
        scratch = [pltpu.VMEM((CONV_W - 1, D_MODEL), F32)]
    out_specs = [tile, lanes_tile, lanes_tile, u_spec, pl.BlockSpec((N_EXPERTS, LANES), const2),
                 pl.BlockSpec(memory_space=pl.ANY)]
    out_shape = [
        jax.ShapeDtypeStruct((t, D_MODEL), F32),
        jax.ShapeDtypeStruct((SUBLANES, t), I32),
        jax.ShapeDtypeStruct((SUBLANES, t), F32),
        u_shape,
        jax.ShapeDtypeStruct((N_EXPERTS, LANES), F32),
        jax.ShapeDtypeStruct(((N_EXPERTS * region_rows + TOP_K * MERGE_TM) * SUBLANES, LANES), F32),
    ]
    scratch += [
        pltpu.VMEM((N_EXPERTS, LANES), F32),
        pltpu.VMEM((2, tm * SUBLANES, LANES), F32),
        pltpu.VMEM((SUBLANES, tm), I32),
        pltpu.SMEM((2, SUBLANES, tm), I32),
        pltpu.SemaphoreType.DMA((3,)),
    ]
    return pl.pallas_call(
        functools.partial(_merge_kernel, per_token, chained, tiles_per_seq, region_rows),
        grid=(t // tm,),
        in_specs=in_specs,
        out_specs=out_specs,
        out_shape=out_shape,
        scratch_shapes=scratch,
        input_output_aliases=aliases,
        compiler_params=_params(("arbitrary",)),
        name="merge_sample" if per_token else "merge_prompt",
    )(*args)


def _rows_from_tiles(ref, n_rows):
    return jnp.concatenate([ref[pl.ds(c, n_rows, stride=SUBLANES), :] for c in range(D_MODEL // LANES)], axis=1)


def _rows_to_tiles(ref, rows):
    for c in range(D_MODEL // LANES):
        ref[pl.ds(c, rows.shape[0], stride=SUBLANES), :] = rows[:, c * LANES:(c + 1) * LANES]


def _moe_kernel(be_ref, bx_ref, vr_ref, nused_ref, next_ref, *refs):
    x_refs = refs[:MOE_STEP_BLOCKS]
    w1_hbm, b1_ref, w2_hbm, b2_ref, y_hbm, w1f, w2f, w1b, w2b, obuf, sem, osem = refs[MOE_STEP_BLOCKS:]
    n_used = nused_ref[0]
    step = pl.program_id(0)
    rows = MOE_BM * SUBLANES

    def out_copy(at_step, sub):
        blk = at_step * MOE_STEP_BLOCKS + sub
        dst = y_hbm.at[pl.ds(pl.multiple_of(bx_ref[blk] * rows, rows), rows), :]
        return pltpu.make_async_copy(obuf.at[at_step % 2, sub], dst, osem.at[at_step % 2])

    def wait_outputs(at_step):
        for sub in range(MOE_STEP_BLOCKS):
            @pl.when((at_step >= 0) & (at_step * MOE_STEP_BLOCKS + sub < n_used))
            def _(sub=sub):
                out_copy(at_step, sub).wait()

    wait_outputs(step - 2)

    def weight_copies(e):
        return (pltpu.make_async_copy(w1_hbm.at[e], w1f, sem.at[0]),
                pltpu.make_async_copy(w2_hbm.at[e], w2f, sem.at[1]))

    def block(sub, x_ref):
        i = step * MOE_STEP_BLOCKS + sub
        o_view = obuf.at[step % 2, sub]

        @pl.when(i < n_used)
        def _():
            e = be_ref[i]
            prev = be_ref[jnp.maximum(i - 1, 0)]

            @pl.when(i == 0)
            def _():
                for cp in weight_copies(e):
                    cp.start()

            @pl.when((i == 0) | (e != prev))
            def _():
                for cp in weight_copies(e):
                    cp.wait()
                w1b[...] = w1f[...].astype(BF16)
                w2b[...] = w2f[...].astype(BF16)
                nxt = next_ref[e]

                @pl.when(nxt >= 0)
                def _():
                    for cp in weight_copies(nxt):
                        cp.start(priority=1)

            live = lax.broadcasted_iota(I32, (MOE_BM, 1), 0) < vr_ref[i]
            x = jnp.where(live, _rows_from_tiles(x_ref, MOE_BM), 0.0).astype(BF16)
            h1 = jnp.dot(x, w1b[...], preferred_element_type=F32) + b1_ref[e]
            gate = jnp.minimum(h1[:, :D_FF], SWIGLU_LIMIT)
            up = jnp.clip(h1[:, D_FF:], -SWIGLU_LIMIT, SWIGLU_LIMIT)
            glu = gate * jax.nn.sigmoid(SWIGLU_ALPHA * gate)
            act = ((up + 1.0) * glu).astype(BF16)
            _rows_to_tiles(o_view, jnp.dot(act, w2b[...], preferred_element_type=F32) + b2_ref[e])
            out_copy(step, sub).start(priority=1)

    for sub in range(MOE_STEP_BLOCKS):
        block(sub, x_refs[sub])

    @pl.when(step == pl.num_programs(0) - 1)
    def _():
        wait_outputs(step - 1)
        wait_outputs(step)


def _moe_blocks(xs, block_expert, block_xs, valid_rows, n_used, next_expert, w1, b1, w2, b2):
    n_blocks = block_expert.shape[0]
    rows = MOE_BM * SUBLANES
    x_spec = lambda sub: pl.BlockSpec(
        (rows, LANES), lambda i, be, bx, vr, nu, nx: (bx[i * MOE_STEP_BLOCKS + sub], 0))
    whole3 = lambda i, be, bx, vr, nu, nx: (0, 0, 0)
    grid_spec = pltpu.PrefetchScalarGridSpec(
        num_scalar_prefetch=5,
        grid=(n_blocks // MOE_STEP_BLOCKS,),
        in_specs=[x_spec(sub) for sub in range(MOE_STEP_BLOCKS)] + [
            pl.BlockSpec(memory_space=pl.ANY),
            pl.BlockSpec((N_EXPERTS, 1, 2 * D_FF), whole3),
            pl.BlockSpec(memory_space=pl.ANY),
            pl.BlockSpec((N_EXPERTS, 1, D_MODEL), whole3),
        ],
        out_specs=pl.BlockSpec(memory_space=pl.ANY),
        scratch_shapes=[
            pltpu.VMEM((D_MODEL, 2 * D_FF), F32),
            pltpu.VMEM((D_FF, D_MODEL), F32),
            pltpu.VMEM((D_MODEL, 2 * D_FF), BF16),
            pltpu.VMEM((D_FF, D_MODEL), BF16),
            pltpu.VMEM((2, MOE_STEP_BLOCKS, rows, LANES), F32),
            pltpu.SemaphoreType.DMA((2,)),
            pltpu.SemaphoreType.DMA((2,)),
        ],
    )
    return pl.pallas_call(
        _moe_kernel,
        grid_spec=grid_spec,
        out_shape=jax.ShapeDtypeStruct(xs.shape, F32),
        compiler_params=_params(("arbitrary",)),
        name="moe_experts",
    )(block_expert, block_xs, valid_rows, n_used, next_expert, *([xs] * MOE_STEP_BLOCKS),
      w1, b1[:, None, :], w2, b2[:, None, :])


def _block_plan(counts, n_tokens, region_rows):
    n_blocks = -(-n_tokens * TOP_K // MOE_BM) + N_EXPERTS
    n_blocks = -(-n_blocks // MOE_STEP_BLOCKS) * MOE_STEP_BLOCKS
    blocks_e = (counts + MOE_BM - 1) // MOE_BM
    blk_end = jnp.cumsum(blocks_e)
    blk_start = blk_end - blocks_e
    n_used = blk_end[-1:]
    blk = jnp.maximum(jnp.minimum(jnp.arange(n_blocks, dtype=I32), n_used - 1), 0)
    expert = jnp.minimum(jnp.sum((blk_end[None, :] <= blk[:, None]).astype(I32), axis=1), N_EXPERTS - 1)
    ids = jnp.arange(N_EXPERTS, dtype=I32)
    mine = ids[None, :] == expert[:, None]
    within = blk - jnp.sum(jnp.where(mine, blk_start[None, :], 0), axis=1)
    block_xs = expert * (region_rows // MOE_BM) + within
    valid = jnp.clip(jnp.sum(jnp.where(mine, counts[None, :], 0), axis=1) - within * MOE_BM, 0, MOE_BM)
    later = jnp.where((ids[None, :] > ids[:, None]) & (counts[None, :] > 0), ids[None, :], N_EXPERTS)
    next_expert = jnp.min(later, axis=1)
    next_expert = jnp.where(next_expert == N_EXPERTS, -1, next_expert)
    return expert, block_xs.astype(I32), valid.astype(I32), n_used.astype(I32), next_expert.astype(I32)


def _combine_kernel(n_main, n_tokens, dest_ref, hp_ref, hs_ref, gp_ref, gs_ref, nf_ref, yb_hbm, op_ref, os_ref,
                    *scratch):
    ybufs, sem = scratch[:-1], scratch[-1]
    i = pl.program_id(0)
    last = pl.num_programs(0) - 1
    ring = len(ybufs)
    ahead = ring - 1

    def row_copy(tile, sl, t, kk):
        row = pl.multiple_of(dest_ref[kk * n_tokens + tile * COMB_TT + t], SUBLANES)
        rows = t * SUBLANES if isinstance(t, int) else pl.multiple_of(t * SUBLANES, SUBLANES)
        return pltpu.make_async_copy(yb_hbm.at[pl.ds(row, SUBLANES), :],
                                     ybufs[sl].at[kk, pl.ds(rows, SUBLANES), :], sem.at[sl])

    def wait_tile(sl):
        for kk in range(TOP_K):
            pltpu.make_async_copy(yb_hbm.at[pl.ds(0, COMB_TT * SUBLANES), :], ybufs[sl].at[kk], sem.at[sl]).wait()

    @pl.when(i == 0)
    def _():
        def issue(g, carry):
            for uu in range(ISSUE_UNROLL):
                for kk in range(TOP_K):
                    for tile in range(ahead):
                        row_copy(jnp.minimum(tile, last), tile, g * ISSUE_UNROLL + uu, kk).start()
            return carry

        lax.fori_loop(0, COMB_TT // ISSUE_UNROLL, issue, 0)

    def step(cur, h_ref, gate_ref, o_ref):
        wait_tile(cur)
        later = jnp.minimum(i + ahead, last)
        for t in range(COMB_TT):
            for kk in range(TOP_K):
                row_copy(later, (cur + ahead) % ring, t, kk).start(priority=kk % DMA_QUEUES)
        pad = jnp.zeros((COMB_TT - SUBLANES, COMB_TT), F32)
        gates = jnp.concatenate([gate_ref[...], pad], axis=0).T
        y = gates[:, 0:1] * _rows_from_tiles(ybufs[cur].at[0], COMB_TT)
        for kk in range(1, TOP_K):
            y = y + gates[:, kk:kk + 1] * _rows_from_tiles(ybufs[cur].at[kk], COMB_TT)
        h = h_ref[...] + y
        o_ref[...] = (h * lax.rsqrt(jnp.mean(h * h, axis=-1, keepdims=True) + NORM_EPS)) * nf_ref[...]

    for cur in range(ring):
        @pl.when((i % ring == cur) & (i < n_main))
        def _(cur=cur):
            step(cur, hp_ref, gp_ref, op_ref)

        @pl.when((i % ring == cur) & (i >= n_main))
        def _(cur=cur):
            step(cur, hs_ref, gs_ref, os_ref)

    @pl.when(i == last)
    def _():
        for cur in range(ring):
            @pl.when(i % ring == cur)
            def _(cur=cur):
                for extra in range(1, ring):
                    wait_tile((cur + extra) % ring)


def _combine(dest, h_p, h_s, gate_p, gate_s, norm_f, yb):
    n_main = h_p.shape[0] // COMB_TT
    n_tail = h_s.shape[0] // COMB_TT
    main = lambda i, d: (jnp.minimum(i, n_main - 1), 0)
    tail = lambda i, d: (jnp.maximum(i - n_main, 0), 0)
    grid_spec = pltpu.PrefetchScalarGridSpec(
        num_scalar_prefetch=1,
        grid=(n_main + n_tail,),
        in_specs=[
            pl.BlockSpec((COMB_TT, D_MODEL), main),
            pl.BlockSpec((COMB_TT, D_MODEL), tail),
            pl.BlockSpec((SUBLANES, COMB_TT), lambda i, d: (0, jnp.minimum(i, n_main - 1))),
            pl.BlockSpec((SUBLANES, COMB_TT), lambda i, d: (0, jnp.maximum(i - n_main, 0))),
            pl.BlockSpec((1, D_MODEL), lambda i, d: (0, 0)),
            pl.BlockSpec(memory_space=pl.ANY),
        ],
        out_specs=[
            pl.BlockSpec((COMB_TT, D_MODEL), main),
            pl.BlockSpec((COMB_TT, D_MODEL), tail),
        ],
        scratch_shapes=[
            *[pltpu.VMEM((TOP_K, COMB_TT * SUBLANES, LANES), F32) for _ in range(COMB_RING)],
            pltpu.SemaphoreType.DMA((COMB_RING,)),
        ],
    )
    return pl.pallas_call(
        functools.partial(_combine_kernel, n_main, h_p.shape[0] + h_s.shape[0]),
        grid_spec=grid_spec,
        out_shape=[jax.ShapeDtypeStruct(h_p.shape, F32), jax.ShapeDtypeStruct(h_s.shape, F32)],
        compiler_params=_params(("arbitrary",)),
        name="combine",
    )(dest, h_p, h_s, gate_p, gate_s, norm_f, yb)


def kernel(x_prompt, x_sample, state_conv, state_ret, norm1, w_in, conv_w, w_a, w_b, w_o, norm2, w_router,
           b_router, w_e1, b_e1, w_e2, b_e2, norm_f):
    batch, seq_len, _ = x_prompt.shape
    n_dec, dec_seq, _ = x_sample.shape
    depth = norm1.shape[0]
    assert dec_seq == 1 and depth == 1, "single-token decode step of a one-layer trunk"
    past_len = 16384
    t_p = batch * seq_len
    n_tokens = t_p + n_dec
    region_rows = _expert_region_rows(n_tokens)
    lyr = 0
    xp = x_prompt.reshape(t_p, D_MODEL)
    xs = x_sample.reshape(n_dec, D_MODEL)
    g1 = norm1[lyr][None, :]
    w_in_b = w_in[lyr]
    wa, wb, wo = w_a[lyr].astype(BF16), w_b[lyr].astype(BF16), w_o[lyr].astype(BF16)
    g2 = norm2[lyr][None, :]
    wr_t = w_router[lyr].T
    wr_hi = wr_t.astype(BF16)
    wr_split = jnp.concatenate([wr_hi, (wr_t - wr_hi.astype(F32)).astype(BF16)], axis=0)
    br_col = b_router[lyr][:, None]

    proj_p = _inproj(xp, g1, w_in_b, INPROJ_TM, INPROJ_TN, BF16)
    r_p, ret_p = _retention_prompt(proj_p, batch, seq_len)
    h_p, route_p, gate_p, conv_p, counts, dispatch = _merge(
        proj_p, r_p, xp, None, conv_w[lyr], wa, wb, wo, g2, wr_split, jnp.broadcast_to(br_col, (N_EXPERTS, MERGE_TM)),
        jnp.zeros((N_EXPERTS, LANES), F32), None, region_rows, MERGE_TM, seq_len // MERGE_TM)
    proj_s = _inproj(xs, g1, w_in_b, n_dec, INPROJ_TN_SAMPLE, F32)
    r_s, ret_s = _retention_sample(proj_s, state_ret[lyr], past_len)
    hist = (state_conv[lyr][:, 0, :], state_conv[lyr][:, 1, :])
    h_s, route_s, gate_s, u_s, counts, dispatch = _merge(
        proj_s, r_s, xs, hist, conv_w[lyr], wa, wb, wo, g2, wr_split, jnp.broadcast_to(br_col, (N_EXPERTS, n_dec)),
        counts, dispatch, region_rows, n_dec, 1)
    conv_s = jnp.stack([state_conv[lyr][:, 1, :], u_s], axis=1)

    counts_i = counts[:, 0].astype(I32)
    block_expert, block_xs, valid_rows, n_used, next_expert = _block_plan(counts_i, n_tokens, region_rows)
    yb = _moe_blocks(dispatch, block_expert, block_xs, valid_rows, n_used, next_expert,
                     w_e1[lyr], b_e1[lyr], w_e2[lyr], b_e2[lyr])
    dest = jnp.concatenate([route_p[TOP_K:], route_s[TOP_K:]], axis=1).reshape(TOP_K * n_tokens)
    y_p, y_s = _combine(dest, h_p, h_s, gate_p, gate_s, norm_f[None, :], yb)

    y_prompt = y_p.reshape(batch, seq_len, D_MODEL)
    y_sample = y_s.reshape(n_dec, 1, D_MODEL)
    return (y_prompt, y_sample, conv_p[None], ret_p[None], conv_s[None], ret_s[None])
```

```python
import functools

import jax
import jax.numpy as jnp
from jax import lax
from jax.experimental import pallas as pl
from jax.experimental.pallas import tpu as pltpu

F32 = jnp.float32
BF16 = jnp.bfloat16
I32 = jnp.int32

D_MODEL = 1024
CONV_W = 3
N_HEADS = 4
DK = 256
DV = 512
HALF = DK // 2
RET_QK = N_HEADS * DK
RET_V = N_HEADS * DV
ROPE_BASE = 10000.0
N_EXPERTS = 32
TOP_K = 4
D_FF = 1024
SWIGLU_ALPHA = 1.702
SWIGLU_LIMIT = 7.0
NORM_EPS = 1e-5
N_IN_COLS = 3 * D_MODEL + 2 * RET_QK + 2 * RET_V + 2 * D_MODEL
COL_CB, COL_CC, COL_CH = 0, D_MODEL, 2 * D_MODEL
COL_Q = 3 * D_MODEL
COL_K = COL_Q + RET_QK
COL_V = COL_K + RET_QK
COL_G = COL_V + RET_V
COL_GA = COL_G + RET_V
COL_GB = COL_GA + D_MODEL

RET_CHUNK = 256
MERGE_TM = 256
MERGE_GROUPS = 2
INPROJ_TM = 2048
INPROJ_TN = 1024
INPROJ_TN_SAMPLE = 2816
MOE_BM = 256
MOE_STEP_BLOCKS = 4
COMB_TT = 128
COMB_RING = 3
SAMPLE_GROUP = 16
LANES = 128
SUBLANES = 8
ISSUE_UNROLL = 8
DMA_QUEUES = 2
VMEM_LIMIT = 56 * 1024 * 1024


def _params(sem):
    return pltpu.CompilerParams(dimension_semantics=sem, vmem_limit_bytes=VMEM_LIMIT)


def _expert_region_rows(n_tokens):
    return -(-n_tokens // MOE_BM) * MOE_BM


def _inproj_kernel(x_ref, g_ref, w_ref, o_ref, xn_ref):
    @pl.when(pl.program_id(1) == 0)
    def _():
        x = x_ref[...]
        ms = jnp.mean(x * x, axis=-1, keepdims=True)
        xn_ref[...] = ((x * lax.rsqrt(ms + NORM_EPS)) * g_ref[...]).astype(BF16)

    o_ref[...] = jnp.dot(xn_ref[...], w_ref[...].astype(BF16), preferred_element_type=F32).astype(o_ref.dtype)


def _inproj(x, gain, w_bf16, tm, tn, out_dtype):
    m = x.shape[0]
    n = w_bf16.shape[1]
    return pl.pallas_call(
        _inproj_kernel,
        grid=(m // tm, n // tn),
        in_specs=[
            pl.BlockSpec((tm, D_MODEL), lambda i, j: (i, 0)),
            pl.BlockSpec((1, D_MODEL), lambda i, j: (0, 0)),
            pl.BlockSpec((D_MODEL, tn), lambda i, j: (0, j)),
        ],
        out_specs=pl.BlockSpec((tm, tn), lambda i, j: (i, j)),
        out_shape=jax.ShapeDtypeStruct((m, n), out_dtype),
        scratch_shapes=[pltpu.VMEM((tm, D_MODEL), BF16)],
        compiler_params=_params(("arbitrary", "arbitrary")),
        name="inproj",
    )(x, gain, w_bf16)


def _rotary(x, cos, sin):
    x1 = x[:, :HALF]
    x2 = x[:, HALF:]
    return jnp.concatenate([x1 * cos - x2 * sin, x1 * sin + x2 * cos], axis=-1)


def _group_norm_gate(o, g):
    o = o * lax.rsqrt(jnp.mean(o * o, axis=-1, keepdims=True) + NORM_EPS)
    return (g * jax.nn.sigmoid(g)) * o


def _ret_chunk_kernel(q_ref, k_ref, v0_ref, v1_ref, g0_ref, g1_ref, cos_ref, sin_ref, mask_ref, qd_ref, kd_ref,
                      cd_ref, r_ref, s_ref):
    @pl.when(pl.program_id(1) == 0)
    def _():
        s_ref[...] = jnp.zeros_like(s_ref)

    cos = cos_ref[...]
    sin = sin_ref[...]
    v_refs = (v0_ref, v1_ref)
    g_refs = (g0_ref, g1_ref)
    for h in range(N_HEADS):
        half_cols = slice((h % 2) * DV, (h % 2 + 1) * DV)
        q = _rotary(q_ref[:, h * DK:(h + 1) * DK].astype(F32), cos, sin)
        k = _rotary(k_ref[:, h * DK:(h + 1) * DK].astype(F32), cos, sin) * (DK ** -0.5)
        v = v_refs[h // 2][:, half_cols]
        qd = jnp.concatenate([qd_ref[h], qd_ref[h]], axis=-1)
        kd = jnp.concatenate([kd_ref[h], kd_ref[h]], axis=-1)
        state = s_ref[0, h]
        scores = lax.dot_general(q.astype(BF16), k.astype(BF16), (((1,), (1,)), ((), ())),
                                 preferred_element_type=F32) * mask_ref[h]
        intra = jnp.dot(scores.astype(BF16), v, preferred_element_type=F32)
        cross = jnp.dot((q * qd).astype(BF16), state.astype(BF16), preferred_element_type=F32)
        kv = jnp.dot((k * kd).T.astype(BF16), v, preferred_element_type=F32)
        s_ref[0, h] = cd_ref[h] * state + kv
        g = g_refs[h // 2][:, half_cols]
        o = intra + cross
        o = (o * lax.rsqrt(jnp.mean(o * o, axis=-1, keepdims=True) + NORM_EPS)).astype(BF16)
        r_ref[:, h * DV:(h + 1) * DV] = (g * jax.nn.sigmoid(g)) * o


def _retention_tables(chunk, seq_len, pos0):
    log_g = jnp.log1p(-jnp.exp2(-5.0 - jnp.arange(N_HEADS, dtype=F32)))
    pos = jnp.arange(chunk, dtype=F32)
    diff = pos[:, None] - pos[None, :]
    mask = jnp.where(diff >= 0, jnp.exp(jnp.maximum(diff, 0.0)[None] * log_g[:, None, None]), 0.0)
    q_decay = jnp.exp((pos + 1.0)[None, :] * log_g[:, None])[..., None]
    k_decay = jnp.exp((chunk - 1.0 - pos)[None, :] * log_g[:, None])[..., None]
    chunk_decay = jnp.exp(chunk * log_g)[:, None, None]
    qd = jnp.broadcast_to(q_decay, (N_HEADS, chunk, LANES))
    kd = jnp.broadcast_to(k_decay, (N_HEADS, chunk, LANES))
    cd = jnp.broadcast_to(chunk_decay, (N_HEADS, 1, DV))
    positions = pos0 + jnp.arange(seq_len, dtype=jnp.int32)
    inv = ROPE_BASE ** (-jnp.linspace(0.0, 1.0, HALF, dtype=F32))
    ang = positions.astype(F32)[:, None] * inv[None, :]
    return mask, qd, kd, cd, jnp.cos(ang), jnp.sin(ang)


def _retention_prompt(proj, batch, seq_len):
    c = RET_CHUNK
    nc = seq_len // c
    mask, qd, kd, cd, cos, sin = _retention_tables(c, seq_len, 0)
    row = lambda b, j: b * nc + j
    full3 = lambda b, j: (0, 0, 0)
    return pl.pallas_call(
        _ret_chunk_kernel,
        grid=(batch, nc),
        in_specs=[
            pl.BlockSpec((c, RET_QK), lambda b, j: (row(b, j), COL_Q // RET_QK)),
            pl.BlockSpec((c, RET_QK), lambda b, j: (row(b, j), COL_K // RET_QK)),
            pl.BlockSpec((c, RET_QK), lambda b, j: (row(b, j), COL_V // RET_QK)),
            pl.BlockSpec((c, RET_QK), lambda b, j: (row(b, j), COL_V // RET_QK + 1)),
            pl.BlockSpec((c, RET_QK), lambda b, j: (row(b, j), COL_G // RET_QK)),
            pl.BlockSpec((c, RET_QK), lambda b, j: (row(b, j), COL_G // RET_QK + 1)),
            pl.BlockSpec((c, HALF), lambda b, j: (j, 0)),
            pl.BlockSpec((c, HALF), lambda b, j: (j, 0)),
            pl.BlockSpec((N_HEADS, c, c), full3),
            pl.BlockSpec((N_HEADS, c, LANES), full3),
            pl.BlockSpec((N_HEADS, c, LANES), full3),
            pl.BlockSpec((N_HEADS, 1, DV), full3),
        ],
        out_specs=[
            pl.BlockSpec((c, RET_V), lambda b, j: (row(b, j), 0)),
            pl.BlockSpec((1, N_HEADS, DK, DV), lambda b, j: (b, 0, 0, 0)),
        ],
        out_shape=[
            jax.ShapeDtypeStruct((batch * seq_len, RET_V), BF16),
            jax.ShapeDtypeStruct((batch, N_HEADS, DK, DV), F32),
        ],
        compiler_params=_params(("arbitrary", "arbitrary")),
        name="retention_chunk",
    )(proj, proj, proj, proj, proj, proj, cos, sin, mask, qd, kd, cd)


def _ret_step_kernel(q_ref, k_ref, v_ref, g_ref, cos_ref, sin_ref, gam_ref, s_ref, r_ref, so_ref):
    cos = cos_ref[...]
    sin = sin_ref[...]
    gam = gam_ref[0]
    q = _rotary(q_ref[...], cos, sin)
    k = _rotary(k_ref[...], cos, sin) * (DK ** -0.5)
    v = v_ref[...]
    intra = jnp.sum(q * k, axis=-1, keepdims=True) * v
    q_t = (q * gam[:, :DK]).T
    k_t = k.T
    cross = []
    for s in range(SAMPLE_GROUP):
        state = s_ref[s, 0]
        cross.append(jnp.sum(q_t[:, s:s + 1] * state, axis=0, keepdims=True))
        so_ref[s, 0] = gam * state + k_t[:, s:s + 1] * v[s:s + 1, :]
    o = intra + jnp.concatenate(cross, axis=0)
    r_ref[...] = _group_norm_gate(o, g_ref[...])


def _retention_sample(proj, state, pos0):
    n = proj.shape[0]
    grp = SAMPLE_GROUP
    log_g = jnp.log1p(-jnp.exp2(-5.0 - jnp.arange(N_HEADS, dtype=F32)))
    gam = jnp.broadcast_to(jnp.exp(log_g)[:, None, None], (N_HEADS, 1, DV))
    _, _, _, _, cos, sin = _retention_tables(1, 1, pos0)
    return pl.pallas_call(
        _ret_step_kernel,
        grid=(n // grp, N_HEADS),
        in_specs=[
            pl.BlockSpec((grp, DK), lambda j, h: (j, COL_Q // DK + h)),
            pl.BlockSpec((grp, DK), lambda j, h: (j, COL_K // DK + h)),
            pl.BlockSpec((grp, DV), lambda j, h: (j, COL_V // DV + h)),
            pl.BlockSpec((grp, DV), lambda j, h: (j, COL_G // DV + h)),
            pl.BlockSpec((1, HALF), lambda j, h: (0, 0)),
            pl.BlockSpec((1, HALF), lambda j, h: (0, 0)),
            pl.BlockSpec((1, 1, DV), lambda j, h: (h, 0, 0)),
            pl.BlockSpec((grp, 1, DK, DV), lambda j, h: (j, h, 0, 0)),
        ],
        out_specs=[
            pl.BlockSpec((grp, DV), lambda j, h: (j, h)),
            pl.BlockSpec((grp, 1, DK, DV), lambda j, h: (j, h, 0, 0)),
        ],
        out_shape=[
            jax.ShapeDtypeStruct((n, RET_V), F32),
            jax.ShapeDtypeStruct(state.shape, F32),
        ],
        compiler_params=_params(("arbitrary", "arbitrary")),
        name="retention_step",
    )(proj, proj, proj, proj, cos, sin, gam, state)


def _merge_kernel(per_token_hist, chained, tiles_per_seq, region_rows, *refs):
    refs = list(refs)
    cb_ref, cc_ref, ch_ref, ga_ref, gb_ref, r_ref, x_ref = refs[:7]
    del refs[:7]
    if per_token_hist:
        h0_ref, h1_ref = refs[:2]
        del refs[:2]
    cw_ref, wa_ref, wb_ref, wo_ref, n2_ref, wr_ref, br_ref, cnt_in = refs[:8]
    del refs[:8]
    if chained:
        del refs[:1]
    h_out, route_out, gate_out, u_out, cnt_out, xs_out = refs[:6]
    del refs[:6]
    if not per_token_hist:
        hist_ref = refs.pop(0)
    cnt_ref, hn_st, dvm, dsm, sem = refs

    i = pl.program_id(0)
    last = pl.num_programs(0) - 1
    tm = x_ref.shape[0]
    cur = i % 2
    prev = 1 - cur

    def wait_scatters(p):
        for _ in range(TOP_K):
            pltpu.make_async_copy(hn_st.at[p], xs_out.at[pl.ds(0, tm * SUBLANES), :], sem.at[p]).wait()

    def dests_to_smem(p):
        return pltpu.make_async_copy(dvm, dsm.at[p], sem.at[2])

    def row_scatter(p, t, kk):
        rows = t * SUBLANES if isinstance(t, int) else pl.multiple_of(t * SUBLANES, SUBLANES)
        dst = xs_out.at[pl.ds(pl.multiple_of(dsm[p, kk, t], SUBLANES), SUBLANES), :]
        return pltpu.make_async_copy(hn_st.at[p, pl.ds(rows, SUBLANES), :], dst, sem.at[p])

    @pl.when(i == 0)
    def _():
        cnt_ref[...] = cnt_in[...]
        hn_st[1] = jnp.zeros((tm * SUBLANES, LANES), F32)
        spare = lax.broadcasted_iota(I32, (SUBLANES, tm), 0) * tm + lax.broadcasted_iota(I32, (SUBLANES, tm), 1)
        dvm[...] = (N_EXPERTS * region_rows + spare) * SUBLANES
        dests_to_smem(1).start()
        dests_to_smem(1).wait()

    if not per_token_hist:
        @pl.when(i % tiles_per_seq == 0)
        def _():
            hist_ref[...] = jnp.zeros_like(hist_ref)

    @pl.when(i > 0)
    def _():
        dests_to_smem(prev).wait()
        wait_scatters(cur)

    for t in range(tm):
        for kk in range(TOP_K):
            row_scatter(prev, t, kk).start(priority=kk % DMA_QUEUES)

    u = cc_ref[...].astype(F32) * ch_ref[...].astype(F32)
    if per_token_hist:
        u2 = h0_ref[...]
        u1 = h1_ref[...]
        u_out[...] = u
    else:
        hm2 = hist_ref[0:1, :]
        hm1 = hist_ref[1:2, :]
        row = lax.broadcasted_iota(I32, u.shape, 0)
        u1 = jnp.where(row == 0, hm1, pltpu.roll(u, 1, axis=0))
        u2 = jnp.where(row == 0, hm2, jnp.where(row == 1, hm1, pltpu.roll(u, 2, axis=0)))
        last2 = u[tm - (CONV_W - 1):, :]
        hist_ref[...] = last2
        u_out[0] = last2
    conv = cw_ref[0:1, :] * u2 + cw_ref[1:2, :] * u1 + cw_ref[2:3, :] * u
    z = (cb_ref[...].astype(F32) * conv).astype(BF16)
    groups = [slice(g * tm // MERGE_GROUPS, (g + 1) * tm // MERGE_GROUPS) for g in range(MERGE_GROUPS)]
    nt = (((1,), (1,)), ((), ()))
    o_ab = [(jnp.dot(z[rows], wa_ref[...], preferred_element_type=F32),
             jnp.dot(r_ref[rows, :].astype(BF16), wb_ref[...], preferred_element_type=F32)) for rows in groups]
    ms = []
    for rows, (o_a, o_b) in zip(groups, o_ab):
        merged = (jax.nn.sigmoid(ga_ref[rows, :].astype(F32)) * o_a
                  + jax.nn.sigmoid(gb_ref[rows, :].astype(F32)) * o_b)
        ms.append(jnp.dot(merged.astype(BF16), wo_ref[...], preferred_element_type=F32))
    logit_groups = []
    for rows, m in zip(groups, ms):
        h = x_ref[rows, :] + m
        h_out[rows, :] = h
        hn = (h * lax.rsqrt(jnp.mean(h * h, axis=-1, keepdims=True) + NORM_EPS)) * n2_ref[...]
        hn_hi = hn.astype(BF16)
        hn_lo = (hn - hn_hi.astype(F32)).astype(BF16)
        part = lax.dot_general(wr_ref[...], hn_hi, nt, preferred_element_type=F32)
        logit_groups.append(part[:N_EXPERTS] + part[N_EXPERTS:]
                            + lax.dot_general(wr_ref[:N_EXPERTS, :], hn_lo, nt, preferred_element_type=F32))
        for c in range(D_MODEL // LANES):
            hn_st[cur, pl.ds(rows.start * SUBLANES + c, rows.stop - rows.start, stride=SUBLANES), :] = (
                hn[:, c * LANES:(c + 1) * LANES])
    logits = jnp.concatenate(logit_groups, axis=1) + br_ref[...]
    expert = lax.broadcasted_iota(I32, logits.shape, 0)
    vals, idxs = [], []
    for _ in range(TOP_K):
        top = jnp.max(logits, axis=0, keepdims=True)
        arg = jnp.min(jnp.where(logits == top, expert, N_EXPERTS), axis=0, keepdims=True)
        vals.append(top)
        idxs.append(arg)
        logits = jnp.where(expert == arg, -jnp.inf, logits)
    exps = [jnp.exp(val - vals[0]) for val in vals]
    denom = exps[0] + exps[1] + exps[2] + exps[3]

    picks = [expert == idx for idx in idxs]
    chosen = jnp.zeros(logits.shape, F32)
    for pick in picks:
        chosen = jnp.where(pick, 1.0, chosen)
    before = (lax.broadcasted_iota(I32, (tm, tm), 0) < lax.broadcasted_iota(I32, (tm, tm), 1))
    prefix = jnp.dot(chosen.astype(BF16), jnp.where(before, 1.0, 0.0).astype(BF16),
                     preferred_element_type=F32)
    counts = cnt_ref[...]
    base = jnp.concatenate([counts] * (tm // LANES), axis=1) + prefix
    ranks = [jnp.sum(jnp.where(pick, base, 0.0), axis=0, keepdims=True).astype(I32) for pick in picks]
    counts = counts + jnp.sum(chosen, axis=1, keepdims=True)
    cnt_ref[...] = counts
    cnt_out[...] = counts

    slot_row = lax.broadcasted_iota(I32, (SUBLANES, tm), 0)
    route = jnp.zeros((SUBLANES, tm), I32)
    gates = jnp.zeros((SUBLANES, tm), F32)
    dests = jnp.zeros((SUBLANES, tm), I32)
    for kk in range(TOP_K):
        route = jnp.where(slot_row == kk, idxs[kk], route)
        dest = (idxs[kk] * region_rows + ranks[kk]) * SUBLANES
        route = jnp.where(slot_row == TOP_K + kk, dest, route)
        gates = jnp.where(slot_row == kk, exps[kk] / denom, gates)
        dests = jnp.where(slot_row == kk, dest, dests)
    route_out[...] = route
    gate_out[...] = gates

    dvm[...] = dests
    dests_to_smem(cur).start()

    @pl.when(i == last)
    def _():
        dests_to_smem(cur).wait()
        wait_scatters(prev)

        def issue(g, carry):
            for uu in range(ISSUE_UNROLL):
                for kk in range(TOP_K):
                    row_scatter(cur, g * ISSUE_UNROLL + uu, kk).start()
            return carry

        lax.fori_loop(0, tm // ISSUE_UNROLL, issue, 0)
        wait_scatters(cur)


def _merge(proj, r, x, hist, conv_w, wa, wb, wo, norm2, w_router, b_router, counts_in, xs_prev, region_rows,
           tm, tiles_per_seq):
    t = x.shape[0]
    per_token = hist is not None
    chained = xs_prev is not None
    col = lambda c: (lambda i: (i, c // D_MODEL))
    const2 = lambda i: (0, 0)
    tile = pl.BlockSpec((tm, D_MODEL), lambda i: (i, 0))
    in_specs = [
        pl.BlockSpec((tm, D_MODEL), col(COL_CB)),
        pl.BlockSpec((tm, D_MODEL), col(COL_CC)),
        pl.BlockSpec((tm, D_MODEL), col(COL_CH)),
        pl.BlockSpec((tm, D_MODEL), col(COL_GA)),
        pl.BlockSpec((tm, D_MODEL), col(COL_GB)),
        pl.BlockSpec((tm, RET_V), lambda i: (i, 0)),
        tile,
    ]
    args = [proj, proj, proj, proj, proj, r, x]
    if per_token:
        in_specs += [tile, tile]
        args += list(hist)
    in_specs += [
        pl.BlockSpec((CONV_W, D_MODEL), const2),
        pl.BlockSpec((D_MODEL, D_MODEL), const2),
        pl.BlockSpec((RET_V, D_MODEL), const2),
        pl.BlockSpec((D_MODEL, D_MODEL), const2),
        pl.BlockSpec((1, D_MODEL), const2),
        pl.BlockSpec((2 * N_EXPERTS, D_MODEL), const2),
        pl.BlockSpec((N_EXPERTS, tm), const2),
        pl.BlockSpec((N_EXPERTS, LANES), const2),
    ]
    args += [conv_w, wa, wb, wo, norm2, w_router, b_router, counts_in]
    aliases = {}
    if chained:
        aliases = {len(args): 5}
        in_specs.append(pl.BlockSpec(memory_space=pl.ANY))
        args.append(xs_prev)
    lanes_tile = pl.BlockSpec((SUBLANES, tm), lambda i: (0, i))
    if per_token:
        u_spec = tile
        u_shape = jax.ShapeDtypeStruct((t, D_MODEL), F32)
        scratch = []
    else:
        n_seq = t // (tm * tiles_per_seq)
        u_spec = pl.BlockSpec((1, CONV_W - 1, D_MODEL), lambda i: (i // tiles_per_seq, 0, 0))
        u_shape = jax.ShapeDtypeStruct((n_seq, CONV_W - 1, D_MODEL), F32)
        scratch = [pltpu.VMEM((CONV_W - 1, D_MODEL), F32)]
    out_specs = [tile, lanes_tile, lanes_tile, u_spec, pl.BlockSpec((N_EXPERTS, LANES), const2),
                 pl.BlockSpec(memory_space=pl.ANY)]
    out_shape = [
        jax.ShapeDtypeStruct((t, D_MODEL), F32),
        jax.ShapeDtypeStruct((SUBLANES, t), I32),
        jax.ShapeDtypeStruct((SUBLANES, t), F32),
        u_shape,
        jax.ShapeDtypeStruct((N_EXPERTS, LANES), F32),
        jax.ShapeDtypeStruct(((N_EXPERTS * region_rows + TOP_K * MERGE_TM) * SUBLANES, LANES), F32),
    ]
    scratch += [
        pltpu.VMEM((N_EXPERTS, LANES), F32),
        pltpu.VMEM((2, tm * SUBLANES, LANES), F32),
        pltpu.VMEM((SUBLANES, tm), I32),
        pltpu.SMEM((2, SUBLANES, tm), I32),
        pltpu.SemaphoreType.DMA((3,)),
    ]
    return pl.pallas_call(
        functools.partial(_merge_kernel, per_token, chained, tiles_per_seq, region_rows),
        grid=(t // tm,),
        in_specs=in_specs,
        out_specs=out_specs,
        out_shape=out_shape,
        scratch_shapes=scratch,
        input_output_aliases=aliases,
        compiler_params=_params(("arbitrary",)),
        name="merge_sample" if per_token else "merge_prompt",
    )(*args)


def _rows_from_tiles(ref, n_rows):
    return jnp.concatenate([ref[pl.ds(c, n_rows, stride=SUBLANES), :] for c in range(D_MODEL // LANES)], axis=1)


def _rows_to_tiles(ref, rows):
    for c in range(D_MODEL // LANES):
        ref[pl.ds(c, rows.shape[0], stride=SUBLANES), :] = rows[:, c * LANES:(c + 1) * LANES]


def _moe_kernel(be_ref, bx_ref, vr_ref, nused_ref, next_ref, *refs):
    x_refs = refs[:MOE_STEP_BLOCKS]
    w1_hbm, b1_ref, w2_hbm, b2_ref, y_hbm, w1f, w2f, w1b, w2b, obuf, sem, osem = refs[MOE_STEP_BLOCKS:]
    n_used = nused_ref[0]
    step = pl.program_id(0)
    rows = MOE_BM * SUBLANES

    def out_copy(at_step, sub):
        blk = at_step * MOE_STEP_BLOCKS + sub
        dst = y_hbm.at[pl.ds(pl.multiple_of(bx_ref[blk] * rows, rows), rows), :]
        return pltpu.make_async_copy(obuf.at[at_step % 2, sub], dst, osem.at[at_step % 2])

    def wait_outputs(at_step):
        for sub in range(MOE_STEP_BLOCKS):
            @pl.when((at_step >= 0) & (at_step * MOE_STEP_BLOCKS + sub < n_used))
            def _(sub=sub):
                out_copy(at_step, sub).wait()

    wait_outputs(step - 2)

    def weight_copies(e):
        return (pltpu.make_async_copy(w1_hbm.at[e], w1f, sem.at[0]),
                pltpu.make_async_copy(w2_hbm.at[e], w2f, sem.at[1]))

    def block(sub, x_ref):
        i = step * MOE_STEP_BLOCKS + sub
        o_view = obuf.at[step % 2, sub]

        @pl.when(i < n_used)
        def _():
            e = be_ref[i]
            prev = be_ref[jnp.maximum(i - 1, 0)]

            @pl.when(i == 0)
            def _():
                for cp in weight_copies(e):
                    cp.start()

            @pl.when((i == 0) | (e != prev))
            def _():
                for cp in weight_copies(e):
                    cp.wait()
                w1b[...] = w1f[...].astype(BF16)
                w2b[...] = w2f[...].astype(BF16)
                nxt = next_ref[e]

                @pl.when(nxt >= 0)
                def _():
                    for cp in weight_copies(nxt):
                        cp.start(priority=1)

            live = lax.broadcasted_iota(I32, (MOE_BM, 1), 0) < vr_ref[i]
            x = jnp.where(live, _rows_from_tiles(x_ref, MOE_BM), 0.0).astype(BF16)
            h1 = jnp.dot(x, w1b[...], preferred_element_type=F32) + b1_ref[e]
            gate = jnp.minimum(h1[:, :D_FF], SWIGLU_LIMIT)
            up = jnp.clip(h1[:, D_FF:], -SWIGLU_LIMIT, SWIGLU_LIMIT)
            glu = gate * jax.nn.sigmoid(SWIGLU_ALPHA * gate)
            act = ((up + 1.0) * glu).astype(BF16)
            _rows_to_tiles(o_view, jnp.dot(act, w2b[...], preferred_element_type=F32) + b2_ref[e])
            out_copy(step, sub).start(priority=1)

    for sub in range(MOE_STEP_BLOCKS):
        block(sub, x_refs[sub])

    @pl.when(step == pl.num_programs(0) - 1)
    def _():
        wait_outputs(step - 1)
        wait_outputs(step)


def _moe_blocks(xs, block_expert, block_xs, valid_rows, n_used, next_expert, w1, b1, w2, b2):
    n_blocks = block_expert.shape[0]
    rows = MOE_BM * SUBLANES
    x_spec = lambda sub: pl.BlockSpec(
        (rows, LANES), lambda i, be, bx, vr, nu, nx: (bx[i * MOE_STEP_BLOCKS + sub], 0))
    whole3 = lambda i, be, bx, vr, nu, nx: (0, 0, 0)
    grid_spec = pltpu.PrefetchScalarGridSpec(
        num_scalar_prefetch=5,
        grid=(n_blocks // MOE_STEP_BLOCKS,),
        in_specs=[x_spec(sub) for sub in range(MOE_STEP_BLOCKS)] + [
            pl.BlockSpec(memory_space=pl.ANY),
            pl.BlockSpec((N_EXPERTS, 1, 2 * D_FF), whole3),
            pl.BlockSpec(memory_space=pl.ANY),
            pl.BlockSpec((N_EXPERTS, 1, D_MODEL), whole3),
        ],
        out_specs=pl.BlockSpec(memory_space=pl.ANY),
        scratch_shapes=[
            pltpu.VMEM((D_MODEL, 2 * D_FF), F32),
            pltpu.VMEM((D_FF, D_MODEL), F32),
            pltpu.VMEM((D_MODEL, 2 * D_FF), BF16),
            pltpu.VMEM((D_FF, D_MODEL), BF16),
            pltpu.VMEM((2, MOE_STEP_BLOCKS, rows, LANES), F32),
            pltpu.SemaphoreType.DMA((2,)),
            pltpu.SemaphoreType.DMA((2,)),
        ],
    )
    return pl.pallas_call(
        _moe_kernel,
        grid_spec=grid_spec,
        out_shape=jax.ShapeDtypeStruct(xs.shape, F32),
        compiler_params=_params(("arbitrary",)),
        name="moe_experts",
    )(block_expert, block_xs, valid_rows, n_used, next_expert, *([xs] * MOE_STEP_BLOCKS),
      w1, b1[:, None, :], w2, b2[:, None, :])


def _block_plan(counts, n_tokens, region_rows):
    n_blocks = -(-n_tokens * TOP_K // MOE_BM) + N_EXPERTS
    n_blocks = -(-n_blocks // MOE_STEP_BLOCKS) * MOE_STEP_BLOCKS
    blocks_e = (counts + MOE_BM - 1) // MOE_BM
    blk_end = jnp.cumsum(blocks_e)
    blk_start = blk_end - blocks_e
    n_used = blk_end[-1:]
    blk = jnp.maximum(jnp.minimum(jnp.arange(n_blocks, dtype=I32), n_used - 1), 0)
    expert = jnp.minimum(jnp.sum((blk_end[None, :] <= blk[:, None]).astype(I32), axis=1), N_EXPERTS - 1)
    ids = jnp.arange(N_EXPERTS, dtype=I32)
    mine = ids[None, :] == expert[:, None]
    within = blk - jnp.sum(jnp.where(mine, blk_start[None, :], 0), axis=1)
    block_xs = expert * (region_rows // MOE_BM) + within
    valid = jnp.clip(jnp.sum(jnp.where(mine, counts[None, :], 0), axis=1) - within * MOE_BM, 0, MOE_BM)
    later = jnp.where((ids[None, :] > ids[:, None]) & (counts[None, :] > 0), ids[None, :], N_EXPERTS)
    next_expert = jnp.min(later, axis=1)
    next_expert = jnp.where(next_expert == N_EXPERTS, -1, next_expert)
    return expert, block_xs.astype(I32), valid.astype(I32), n_used.astype(I32), next_expert.astype(I32)


def _combine_kernel(n_main, n_tokens, dest_ref, hp_ref, hs_ref, gp_ref, gs_ref, nf_ref, yb_hbm, op_ref, os_ref,
                    *scratch):
    ybufs, sem = scratch[:-1], scratch[-1]
    i = pl.program_id(0)
    last = pl.num_programs(0) - 1
    ring = len(ybufs)
    ahead = ring - 1

    def row_copy(tile, sl, t, kk):
        row = pl.multiple_of(dest_ref[kk * n_tokens + tile * COMB_TT + t], SUBLANES)
        rows = t * SUBLANES if isinstance(t, int) else pl.multiple_of(t * SUBLANES, SUBLANES)
        return pltpu.make_async_copy(yb_hbm.at[pl.ds(row, SUBLANES), :],
                                     ybufs[sl].at[kk, pl.ds(rows, SUBLANES), :], sem.at[sl])

    def wait_tile(sl):
        for kk in range(TOP_K):
            pltpu.make_async_copy(yb_hbm.at[pl.ds(0, COMB_TT * SUBLANES), :], ybufs[sl].at[kk], sem.at[sl]).wait()

    @pl.when(i == 0)
    def _():
        def issue(g, carry):
            for uu in range(ISSUE_UNROLL):
                for kk in range(TOP_K):
                    for tile in range(ahead):
                        row_copy(jnp.minimum(tile, last), tile, g * ISSUE_UNROLL + uu, kk).start()
            return carry

        lax.fori_loop(0, COMB_TT // ISSUE_UNROLL, issue, 0)

    def step(cur, h_ref, gate_ref, o_ref):
        wait_tile(cur)
        later = jnp.minimum(i + ahead, last)
        for t in range(COMB_TT):
            for kk in range(TOP_K):
                row_copy(later, (cur + ahead) % ring, t, kk).start(priority=kk % DMA_QUEUES)
        pad = jnp.zeros((COMB_TT - SUBLANES, COMB_TT), F32)
        gates = jnp.concatenate([gate_ref[...], pad], axis=0).T
        y = gates[:, 0:1] * _rows_from_tiles(ybufs[cur].at[0], COMB_TT)
        for kk in range(1, TOP_K):
            y = y + gates[:, kk:kk + 1] * _rows_from_tiles(ybufs[cur].at[kk], COMB_TT)
        h = h_ref[...] + y
        o_ref[...] = (h * lax.rsqrt(jnp.mean(h * h, axis=-1, keepdims=True) + NORM_EPS)) * nf_ref[...]

    for cur in range(ring):
        @pl.when((i % ring == cur) & (i < n_main))
        def _(cur=cur):
            step(cur, hp_ref, gp_ref, op_ref)

        @pl.when((i % ring == cur) & (i >= n_main))
        def _(cur=cur):
            step(cur, hs_ref, gs_ref, os_ref)

    @pl.when(i == last)
    def _():
        for cur in range(ring):
            @pl.when(i % ring == cur)
            def _(cur=cur):
                for extra in range(1, ring):
                    wait_tile((cur + extra) % ring)


def _combine(dest, h_p, h_s, gate_p, gate_s, norm_f, yb):
    n_main = h_p.shape[0] // COMB_TT
    n_tail = h_s.shape[0] // COMB_TT
    main = lambda i, d: (jnp.minimum(i, n_main - 1), 0)
    tail = lambda i, d: (jnp.maximum(i - n_main, 0), 0)
    grid_spec = pltpu.PrefetchScalarGridSpec(
        num_scalar_prefetch=1,
        grid=(n_main + n_tail,),
        in_specs=[
            pl.BlockSpec((COMB_TT, D_MODEL), main),
            pl.BlockSpec((COMB_TT, D_MODEL), tail),
            pl.BlockSpec((SUBLANES, COMB_TT), lambda i, d: (0, jnp.minimum(i, n_main - 1))),
            pl.BlockSpec((SUBLANES, COMB_TT), lambda i, d: (0, jnp.maximum(i - n_main, 0))),
            pl.BlockSpec((1, D_MODEL), lambda i, d: (0, 0)),
            pl.BlockSpec(memory_space=pl.ANY),
        ],
        out_specs=[
            pl.BlockSpec((COMB_TT, D_MODEL), main),
            pl.BlockSpec((COMB_TT, D_MODEL), tail),
        ],
        scratch_shapes=[
            *[pltpu.VMEM((TOP_K, COMB_TT * SUBLANES, LANES), F32) for _ in range(COMB_RING)],
            pltpu.SemaphoreType.DMA((COMB_RING,)),
        ],
    )
    return pl.pallas_call(
        functools.partial(_combine_kernel, n_main, h_p.shape[0] + h_s.shape[0]),
        grid_spec=grid_spec,
        out_shape=[jax.ShapeDtypeStruct(h_p.shape, F32), jax.ShapeDtypeStruct(h_s.shape, F32)],
        compiler_params=_params(("arbitrary",)),
        name="combine",
    )(dest, h_p, h_s, gate_p, gate_s, norm_f, yb)


def kernel(x_prompt, x_sample, state_conv, state_ret, norm1, w_in, conv_w, w_a, w_b, w_o, norm2, w_router,
           b_router, w_e1, b_e1, w_e2, b_e2, norm_f):
    batch, seq_len, _ = x_prompt.shape
    n_dec, dec_seq, _ = x_sample.shape
    depth = norm1.shape[0]
    assert dec_seq == 1 and depth == 1, "single-token decode step of a one-layer trunk"
    past_len = 16384
    t_p = batch * seq_len
    n_tokens = t_p + n_dec
    region_rows = _expert_region_rows(n_tokens)
    lyr = 0
    xp = x_prompt.reshape(t_p, D_MODEL)
    xs = x_sample.reshape(n_dec, D_MODEL)
    g1 = norm1[lyr][None, :]
    w_in_b = w_in[lyr]
    wa, wb, wo = w_a[lyr].astype(BF16), w_b[lyr].astype(BF16), w_o[lyr].astype(BF16)
    g2 = norm2[lyr][None, :]
    wr_t = w_router[lyr].T
    wr_hi = wr_t.astype(BF16)
    wr_split = jnp.concatenate([wr_hi, (wr_t - wr_hi.astype(F32)).astype(BF16)], axis=0)
    br_col = b_router[lyr][:, None]

    proj_p = _inproj(xp, g1, w_in_b, INPROJ_TM, INPROJ_TN, BF16)
    r_p, ret_p = _retention_prompt(proj_p, batch, seq_len)
    h_p, route_p, gate_p, conv_p, counts, dispatch = _merge(
        proj_p, r_p, xp, None, conv_w[lyr], wa, wb, wo, g2, wr_split, jnp.broadcast_to(br_col, (N_EXPERTS, MERGE_TM)),
        jnp.zeros((N_EXPERTS, LANES), F32), None, region_rows, MERGE_TM, seq_len // MERGE_TM)
    proj_s = _inproj(xs, g1, w_in_b, n_dec, INPROJ_TN_SAMPLE, F32)
    r_s, ret_s = _retention_sample(proj_s, state_ret[lyr], past_len)
    hist = (state_conv[lyr][:, 0, :], state_conv[lyr][:, 1, :])
    h_s, route_s, gate_s, u_s, counts, dispatch = _merge(
        proj_s, r_s, xs, hist, conv_w[lyr], wa, wb, wo, g2, wr_split, jnp.broadcast_to(br_col, (N_EXPERTS, n_dec)),
        counts, dispatch, region_rows, n_dec, 1)
    conv_s = jnp.stack([state_conv[lyr][:, 1, :], u_s], axis=1)

    counts_i = counts[:, 0].astype(I32)
    block_expert, block_xs, valid_rows, n_used, next_expert = _block_plan(counts_i, n_tokens, region_rows)
    yb = _moe_blocks(dispatch, block_expert, block_xs, valid_rows, n_used, next_expert,
                     w_e1[lyr], b_e1[lyr], w_e2[lyr], b_e2[lyr])
    dest = jnp.concatenate([route_p[TOP_K:], route_s[TOP_K:]], axis=1).reshape(TOP_K * n_tokens)
    y_p, y_s = _combine(dest, h_p, h_s, gate_p, gate_s, norm_f[None, :], yb)

    y_prompt = y_p.reshape(batch, seq_len, D_MODEL)
    y_sample = y_s.reshape(n_dec, 1, D_MODEL)
    return (y_prompt, y_sample, conv_p[None], ret_p[None], conv_s[None], ret_s[None])
```
